```python
import math
import jax, jax.numpy as jnp
from jax import lax
import numpy as np

D_MODEL = 2048
BATCH = 16
SEQ = 2048
DEPTH = 1
DEC_BATCH = 4
DEC_SEQ = 4096
PAST_LEN = 128

HEAD_DIM = 128
A_HEADS = 8
A_KV_HEADS = 2
B_HEADS = 8
B_KV_HEADS = 2
A_WIDTH = A_HEADS * HEAD_DIM
B_WIDTH = B_HEADS * HEAD_DIM
A_KV_WIDTH = A_KV_HEADS * HEAD_DIM
B_KV_WIDTH = B_KV_HEADS * HEAD_DIM
IN_SPLITS = (A_WIDTH, A_KV_WIDTH, A_KV_WIDTH, B_WIDTH, B_KV_WIDTH, B_KV_WIDTH, D_MODEL, D_MODEL)
IN_COLS = sum(IN_SPLITS)
Q_BLOCK = 128
WINDOW = 128
N_META = 16
GRID_W = 64
ROPE_THETA = 10000.0
N_BUCKETS = 32
MAX_DISTANCE = 128
N_GROUPS = 4
EXPERTS_PER_GROUP = 4
N_EXPERTS = N_GROUPS * EXPERTS_PER_GROUP
TOP_K_IN_GROUP = 2
EXPERT_FF = 1024
RMS_EPS = 1e-6
NEG_INF = -1e30

kernel_name = 'hybrid_gqa_swa_hmoe_encoder'


def _rmsnorm(x, g):
    x32 = x.astype(jnp.float32)
    y = x32 * lax.rsqrt(jnp.mean(x32 * x32, axis=-1, keepdims=True) + RMS_EPS)
    return (y * g.astype(jnp.float32)).astype(x.dtype)


def _axial_angles(n_tokens):
    rows = n_tokens // GRID_W
    t_row = jnp.repeat(jnp.arange(rows), GRID_W)
    t_col = jnp.tile(jnp.arange(GRID_W), rows)
    row = jnp.concatenate([jnp.full((N_META,), -1), t_row]).astype(jnp.float32)
    col = jnp.concatenate([jnp.arange(N_META), t_col]).astype(jnp.float32)
    n_freq = HEAD_DIM // 4
    inv_freq = ROPE_THETA ** (-jnp.arange(n_freq, dtype=jnp.float32) / n_freq)
    return row[:, None] * inv_freq, col[:, None] * inv_freq


def _rotate(x, ang):
    c = jnp.cos(ang)[:, None, :].astype(x.dtype)
    s = jnp.sin(ang)[:, None, :].astype(x.dtype)
    x1, x2 = jnp.split(x, 2, axis=-1)
    return jnp.concatenate([x1 * c - x2 * s, x2 * c + x1 * s], axis=-1)


def _axial_rope(x, row_ang, col_ang):
    half = HEAD_DIM // 2
    return jnp.concatenate([_rotate(x[..., :half], row_ang), _rotate(x[..., half:], col_ang)], axis=-1)


def _t5_bucket(rel):
    nb = N_BUCKETS // 2
    max_exact = nb // 2
    bucket = jnp.where(rel > 0, nb, 0)
    n = jnp.abs(rel)
    nf = jnp.maximum(n, 1).astype(jnp.float32)
    large = max_exact + (jnp.log(nf / max_exact) / math.log(MAX_DISTANCE / max_exact) * (nb - max_exact)).astype(jnp.int32)
    large = jnp.minimum(large, nb - 1)
    return bucket + jnp.where(n < max_exact, n, large)


def _rel_bias(rel, rel_bias):
    b = rel_bias[_t5_bucket(rel)].astype(jnp.float32)
    b = jnp.moveaxis(b, -1, -3)
    return b.reshape(b.shape[:-3] + (B_KV_HEADS, B_HEADS // B_KV_HEADS) + b.shape[-2:])


def _sink_softmax(s, sink):
    sk = jnp.broadcast_to(sink.astype(jnp.float32)[:, :, None, None], s.shape[:-1] + (1,))
    return jax.nn.softmax(jnp.concatenate([s, sk], axis=-1), axis=-1)[..., :-1]


def _global_attention(q, k, v):
    b_, L = q.shape[0], q.shape[1]
    S = L - N_META
    nblk = S // Q_BLOCK
    rep = A_HEADS // A_KV_HEADS
    scale = HEAD_DIM ** -0.5
    qg = q.reshape(b_, L, A_KV_HEADS, rep, HEAD_DIM)

    def attend(qb):
        s = jnp.einsum('bqgrd,bkgd->bgrqk', qb, k).astype(jnp.float32) * scale
        p = jax.nn.softmax(s, axis=-1).astype(v.dtype)
        return jnp.einsum('bgrqk,bkgd->bqgrd', p, v)

    o_meta = attend(qg[:, :N_META])
    qr = qg[:, N_META:].reshape(b_, nblk, Q_BLOCK, A_KV_HEADS, rep, HEAD_DIM).swapaxes(0, 1)
    o_real = lax.map(attend, qr).swapaxes(0, 1).reshape(b_, S, A_KV_HEADS, rep, HEAD_DIM)
    return jnp.concatenate([o_meta, o_real], axis=1).reshape(b_, L, A_WIDTH)


def _window_attention(q, k, v, rel_bias, sink):
    b_, L = q.shape[0], q.shape[1]
    S = L - N_META
    nblk = S // Q_BLOCK
    rep = B_HEADS // B_KV_HEADS
    scale = HEAD_DIM ** -0.5
    sink = sink.reshape(B_KV_HEADS, rep)
    qg = q.reshape(b_, L, B_KV_HEADS, rep, HEAD_DIM)

    n_ctx = N_META + Q_BLOCK
    rel_m = jnp.arange(n_ctx)[None, :] - jnp.arange(N_META)[:, None]
    s_m = jnp.einsum('bqgrd,bkgd->bgrqk', qg[:, :N_META], k[:, :n_ctx]).astype(jnp.float32) * scale
    s_m = jnp.where(jnp.abs(rel_m) <= WINDOW, s_m + _rel_bias(rel_m, rel_bias), NEG_INF)
    p_m = _sink_softmax(s_m, sink).astype(v.dtype)
    o_m = jnp.einsum('bgrqk,bkgd->bqgrd', p_m, v[:, :n_ctx]).reshape(b_, N_META, B_WIDTH)

    def band(a):
        ap = jnp.pad(a[:, N_META:], ((0, 0), (Q_BLOCK, Q_BLOCK), (0, 0), (0, 0)))
        ap = ap.reshape(b_, nblk + 2, Q_BLOCK, B_KV_HEADS, HEAD_DIM)
        nb_ = jnp.concatenate([ap[:, :-2], ap[:, 1:-1], ap[:, 2:]], axis=2)
        meta = jnp.broadcast_to(a[:, None, :N_META], (b_, nblk, N_META, B_KV_HEADS, HEAD_DIM))
        return jnp.concatenate([meta, nb_], axis=2)

    kb, vb = band(k), band(v)
    blk = jnp.arange(nblk)[:, None, None]
    i = jnp.arange(Q_BLOCK)[None, :, None]
    j = jnp.arange(3 * Q_BLOCK)[None, None, :]
    t_q = blk * Q_BLOCK + i
    t_k = (blk - 1) * Q_BLOCK + j
    rel_real = t_k - t_q
    valid_real = (jnp.abs(rel_real) <= WINDOW) & (t_k >= 0) & (t_k < S)
    rel_meta = jnp.arange(N_META)[None, None, :] - (N_META + t_q)
    rel = jnp.concatenate([jnp.broadcast_to(rel_meta, (nblk, Q_BLOCK, N_META)), rel_real], axis=-1)
    valid = jnp.concatenate([jnp.ones((nblk, Q_BLOCK, N_META), bool), valid_real], axis=-1)

    qr = qg[:, N_META:].reshape(b_, nblk, Q_BLOCK, B_KV_HEADS, rep, HEAD_DIM)
    s = jnp.einsum('bnqgrd,bnkgd->bngrqk', qr, kb).astype(jnp.float32) * scale
    s = jnp.where(valid[:, None, None], s + _rel_bias(rel, rel_bias), NEG_INF)
    p = _sink_softmax(s, sink).astype(v.dtype)
    o = jnp.einsum('bngrqk,bnkgd->bnqgrd', p, vb).reshape(b_, S, B_WIDTH)
    return jnp.concatenate([o_m, o], axis=1)


def _hier_moe(u, w_rg, b_rg, w_re, b_re, w_gate, w_up, w_down):
    def per_seq(us):
        lg = (us @ w_rg).astype(jnp.float32) + b_rg.astype(jnp.float32)
        pg = jax.nn.softmax(lg, axis=-1)
        gidx = jnp.argmax(lg, axis=-1)
        p_top = jnp.take_along_axis(pg, gidx[:, None], axis=-1)
        le = ((us @ w_re).astype(jnp.float32) + b_re.astype(jnp.float32)).reshape(-1, N_GROUPS, EXPERTS_PER_GROUP)
        le_sel = jnp.take_along_axis(le, gidx[:, None, None], axis=1)[:, 0]
        top_v, top_i = lax.top_k(le_sel, TOP_K_IN_GROUP)
        w_top = jax.nn.softmax(top_v, axis=-1) * p_top
        w_grp = jnp.sum(jax.nn.one_hot(top_i, EXPERTS_PER_GROUP, dtype=jnp.float32) * w_top[..., None], axis=1)
        combine = (jax.nn.one_hot(gidx, N_GROUPS, dtype=jnp.float32)[:, :, None] * w_grp[:, None, :]).reshape(-1, N_EXPERTS)
        hg = jnp.einsum('ld,edf->lef', us, w_gate)
        hu = jnp.einsum('ld,edf->lef', us, w_up)
        hid = jax.nn.silu(hg) * hu * combine[:, :, None].astype(us.dtype)
        return jnp.einsum('lef,efd->ld', hid, w_down)
    return lax.map(per_seq, u)


def _layer(h, row_ang, col_ang, rel_bias, norm_mix, w_in, q_norm, k_norm, sink, w_branch_a, w_branch_b,
           w_out, norm_ffn, w_router_g, b_router_g, w_router_e, b_router_e, w_gate, w_up, w_down):
    b_, L, _ = h.shape
    u = _rmsnorm(h, norm_mix)
    proj = u @ w_in
    cuts = [int(c) for c in np.cumsum(IN_SPLITS)[:-1]]
    qa, ka, va, qb, kb, vb, ga, gb = jnp.split(proj, cuts, axis=-1)
    qa = _axial_rope(_rmsnorm(qa.reshape(b_, L, A_HEADS, HEAD_DIM), q_norm), row_ang, col_ang)
    ka = _axial_rope(_rmsnorm(ka.reshape(b_, L, A_KV_HEADS, HEAD_DIM), k_norm), row_ang, col_ang)
    va = va.reshape(b_, L, A_KV_HEADS, HEAD_DIM)
    ya = _global_attention(qa, ka, va) @ w_branch_a
    yb = _window_attention(qb.reshape(b_, L, B_HEADS, HEAD_DIM), kb.reshape(b_, L, B_KV_HEADS, HEAD_DIM),
                           vb.reshape(b_, L, B_KV_HEADS, HEAD_DIM), rel_bias, sink) @ w_branch_b
    mixed = jax.nn.sigmoid(ga) * ya + jax.nn.sigmoid(gb) * yb
    h = h + mixed @ w_out
    h = h + _hier_moe(_rmsnorm(h, norm_ffn), w_router_g, b_router_g, w_router_e, b_router_e, w_gate, w_up, w_down)
    return h


def _encode(x, meta_tokens, rel_bias, final_norm, layer_w):
    b_, S, _ = x.shape
    meta = jnp.broadcast_to(meta_tokens.astype(x.dtype)[None], (b_, N_META, D_MODEL))
    h = jnp.concatenate([meta, x], axis=1)
    row_ang, col_ang = _axial_angles(S)
    for li in range(DEPTH):
        h = _layer(h, row_ang, col_ang, rel_bias, *[w[li] for w in layer_w])
    return _rmsnorm(h, final_norm)[:, N_META:]


def setup_inputs(seed: int = 0) -> dict:
    key = jax.random.key(seed)
    ks = jax.random.split(key, 24)

    def nrm(k, shape, scale):
        return jax.random.normal(k, shape, jnp.float32) * scale

    return {
        'x_prompt': nrm(ks[0], (BATCH, SEQ, D_MODEL), 1.0),
        'x_sample': nrm(ks[1], (DEC_BATCH, DEC_SEQ, D_MODEL), 1.0),
        'meta_tokens': nrm(ks[2], (N_META, D_MODEL), 1.0),
        'rel_bias': nrm(ks[3], (N_BUCKETS, B_HEADS), 0.5),
        'final_norm': 1.0 + nrm(ks[4], (D_MODEL,), 0.05),
        'norm_mix': 1.0 + nrm(ks[5], (DEPTH, D_MODEL), 0.05),
        'w_in': nrm(ks[6], (DEPTH, D_MODEL, IN_COLS), D_MODEL ** -0.5),
        'q_norm': 1.0 + nrm(ks[7], (DEPTH, HEAD_DIM), 0.05),
        'k_norm': 1.0 + nrm(ks[8], (DEPTH, HEAD_DIM), 0.05),
        'sink': nrm(ks[9], (DEPTH, B_HEADS), 0.5),
        'w_branch_a': nrm(ks[10], (DEPTH, A_WIDTH, D_MODEL), A_WIDTH ** -0.5),
        'w_branch_b': nrm(ks[11], (DEPTH, B_WIDTH, D_MODEL), B_WIDTH ** -0.5),
        'w_out': nrm(ks[12], (DEPTH, D_MODEL, D_MODEL), D_MODEL ** -0.5),
        'norm_ffn': 1.0 + nrm(ks[13], (DEPTH, D_MODEL), 0.05),
        'w_router_g': nrm(ks[14], (DEPTH, D_MODEL, N_GROUPS), D_MODEL ** -0.5),
        'b_router_g': nrm(ks[15], (DEPTH, N_GROUPS), 0.01),
        'w_router_e': nrm(ks[16], (DEPTH, D_MODEL, N_EXPERTS), D_MODEL ** -0.5),
        'b_router_e': nrm(ks[17], (DEPTH, N_EXPERTS), 0.01),
        'w_gate': nrm(ks[18], (DEPTH, N_EXPERTS, D_MODEL, EXPERT_FF), D_MODEL ** -0.5),
        'w_up': nrm(ks[19], (DEPTH, N_EXPERTS, D_MODEL, EXPERT_FF), D_MODEL ** -0.5),
        'w_down': nrm(ks[20], (DEPTH, N_EXPERTS, EXPERT_FF, D_MODEL), EXPERT_FF ** -0.5),
    }


def reference(x_prompt, x_sample, meta_tokens, rel_bias, final_norm, norm_mix, w_in, q_norm, k_norm, sink,
              w_branch_a, w_branch_b, w_out, norm_ffn, w_router_g, b_router_g, w_router_e, b_router_e,
              w_gate, w_up, w_down):
    layer_w = (norm_mix, w_in, q_norm, k_norm, sink, w_branch_a, w_branch_b, w_out, norm_ffn,
               w_router_g, b_router_g, w_router_e, b_router_e, w_gate, w_up, w_down)
    y_prompt = _encode(x_prompt, meta_tokens, rel_bias, final_norm, layer_w)
    y_sample = _encode(x_sample, meta_tokens, rel_bias, final_norm, layer_w)
    return (y_prompt, y_sample)
```

```python
import functools
import math

import numpy as np
import jax
import jax.numpy as jnp
from jax import lax
from jax.experimental import pallas as pl
from jax.experimental.pallas import tpu as pltpu

F32 = jnp.float32
BF16 = jnp.bfloat16

D_MODEL = 2048
HEAD_DIM = 128
A_HEADS = 8
A_KV_HEADS = 2
B_HEADS = 8
B_KV_HEADS = 2
REP = A_HEADS // A_KV_HEADS
A_WIDTH = A_HEADS * HEAD_DIM
B_WIDTH = B_HEADS * HEAD_DIM
KV_WIDTH = A_KV_HEADS * HEAD_DIM
IN_COLS = A_WIDTH + 2 * KV_WIDTH + B_WIDTH + 2 * KV_WIDTH + 2 * D_MODEL
Q_BLOCK = 128
WINDOW = 128
N_META = 16
GRID_W = 64
ROPE_THETA = 10000.0
N_BUCKETS = 32
MAX_DISTANCE = 128
N_GROUPS = 4
EXPERTS_PER_GROUP = 4
N_EXPERTS = N_GROUPS * EXPERTS_PER_GROUP
EXPERT_FF = 1024
RMS_EPS = 1e-6
NEG_INF = -1e30
SCORE_SCALE = HEAD_DIM ** -0.5

COL_QA = 0
COL_KA = A_WIDTH // HEAD_DIM
COL_VA = COL_KA + A_KV_HEADS
COL_QB = COL_VA + A_KV_HEADS
COL_KB = COL_QB + B_HEADS
COL_VB = COL_KB + B_KV_HEADS
COL_GA = COL_VB + B_KV_HEADS
COL_GB = COL_GA + D_MODEL // HEAD_DIM

LANES = 128
VMEM_LIMIT = 56 * 1024 * 1024
MOE_TILE = 512
ROUTER_COLS = 128

_NT = (((1,), (1,)), ((), ()))


def _params(sem, vmem=VMEM_LIMIT):
    return pltpu.CompilerParams(dimension_semantics=sem, vmem_limit_bytes=vmem)


def _const_spec(shape):
    nd = len(shape)
    return pl.BlockSpec(shape, lambda *_: (0,) * nd, pipeline_mode=pl.Buffered(1))


def _norm_rope(a, gain, cos, sin, out_scale):
    ms = jnp.mean(a * a, axis=-1, keepdims=True)
    y = a * lax.rsqrt(ms + RMS_EPS) * gain
    lane = lax.broadcasted_iota(jnp.int32, y.shape, 1)
    first_half = (lane & 32) == 0
    partner = jnp.where(first_half, pltpu.roll(y, 96, 1), pltpu.roll(y, 32, 1))
    return (y * cos + partner * sin) * out_scale


def _in_proj_kernel(x_ref, g_ref, w_ref, cs_ref, cos_ref, sin_ref, qn_ref, kn_ref, o_ref, u_ref):
    j = pl.program_id(1)

    @pl.when(j == 0)
    def _():
        x = x_ref[...]
        ms = jnp.mean(x * x, axis=-1, keepdims=True)
        u_ref[...] = (x * lax.rsqrt(ms + RMS_EPS) * g_ref[...]).astype(BF16)

    acc = jnp.dot(u_ref[...], w_ref[...], preferred_element_type=F32) * cs_ref[...]

    @pl.when(j == 0)
    def _():
        cos = cos_ref[...]
        sin = sin_ref[...]
        for h in range(A_HEADS):
            sl = slice(h * HEAD_DIM, (h + 1) * HEAD_DIM)
            o_ref[:, sl] = _norm_rope(acc[:, sl], qn_ref[...], cos, sin, SCORE_SCALE).astype(BF16)

    @pl.when(j == 1)
    def _():
        cos = cos_ref[...]
        sin = sin_ref[...]
        for h in range(A_KV_HEADS):
            sl = slice(h * HEAD_DIM, (h + 1) * HEAD_DIM)
            o_ref[:, sl] = _norm_rope(acc[:, sl], kn_ref[...], cos, sin, 1.0).astype(BF16)
        o_ref[:, KV_WIDTH:] = acc[:, KV_WIDTH:].astype(BF16)

    @pl.when(j >= 2)
    def _():
        o_ref[...] = acc.astype(BF16)


def _in_proj(x2d, norm_g, w_bf16, col_scale, cos, sin, q_norm, k_norm, *, tm):
    t_rows = x2d.shape[0]
    tn = A_WIDTH
    pos_tiles = cos.shape[0] // tm
    return pl.pallas_call(
        _in_proj_kernel,
        grid=(t_rows // tm, IN_COLS // tn),
        in_specs=[
            pl.BlockSpec((tm, D_MODEL), lambda i, j: (i, 0)),
            pl.BlockSpec((1, D_MODEL), lambda i, j: (0, 0)),
            pl.BlockSpec((D_MODEL, tn), lambda i, j: (0, j)),
            pl.BlockSpec((1, tn), lambda i, j: (0, j)),
            pl.BlockSpec((tm, HEAD_DIM), lambda i, j: (i % pos_tiles, 0)),
            pl.BlockSpec((tm, HEAD_DIM), lambda i, j: (i % pos_tiles, 0)),
            pl.BlockSpec((1, HEAD_DIM), lambda i, j: (0, 0)),
            pl.BlockSpec((1, HEAD_DIM), lambda i, j: (0, 0)),
        ],
        out_specs=pl.BlockSpec((tm, tn), lambda i, j: (i, j)),
        out_shape=jax.ShapeDtypeStruct((t_rows, IN_COLS), BF16),
        scratch_shapes=[pltpu.VMEM((tm, D_MODEL), BF16)],
        compiler_params=_params(("parallel", "arbitrary")),
        name="in_proj",
    )(x2d, norm_g, w_bf16, col_scale, cos, sin, q_norm, k_norm)


def _stack_heads(q):
    return jnp.concatenate([q[:, r * HEAD_DIM:(r + 1) * HEAD_DIM] for r in range(REP)], axis=0)


def _global_attn_kernel(q_ref, k_ref, v_ref, km_ref, vm_ref, o_ref, *, tq, ck):
    qs = _stack_heads(q_ref[0])
    s = lax.dot_general(qs, km_ref[0], _NT, preferred_element_type=F32)
    m = jnp.max(s, axis=-1, keepdims=True)
    p = jnp.exp(s - m)
    l = jnp.sum(p, axis=-1, keepdims=True)
    acc = jnp.dot(p.astype(BF16), vm_ref[0], preferred_element_type=F32)

    def body(c, carry):
        m, l, acc = carry
        off = pl.multiple_of(c * ck, ck)
        kc = k_ref[0, pl.ds(off, ck), :]
        vc = v_ref[0, pl.ds(off, ck), :]
        s = lax.dot_general(qs, kc, _NT, preferred_element_type=F32)
        m_new = jnp.maximum(m, jnp.max(s, axis=-1, keepdims=True))
        alpha = jnp.exp(m - m_new)
        p = jnp.exp(s - m_new)
        l = alpha * l + jnp.sum(p, axis=-1, keepdims=True)
        acc = alpha * acc + jnp.dot(p.astype(BF16), vc, preferred_element_type=F32)
        return m_new, l, acc

    m, l, acc = lax.fori_loop(0, k_ref.shape[1] // ck, body, (m, l, acc))
    o = acc / l
    for r in range(REP):
        o_ref[0, :, r * HEAD_DIM:(r + 1) * HEAD_DIM] = o[r * tq:(r + 1) * tq].astype(BF16)


def _global_attn(proj, km, vm, *, tq, ck):
    b, s, _ = proj.shape
    gw = REP * HEAD_DIM
    return pl.pallas_call(
        functools.partial(_global_attn_kernel, tq=tq, ck=ck),
        grid=(b, A_KV_HEADS, s // tq),
        in_specs=[
            pl.BlockSpec((1, tq, gw), lambda bi, g, i: (bi, i, g)),
            pl.BlockSpec((1, s, HEAD_DIM), lambda bi, g, i: (bi, 0, COL_KA + g)),
            pl.BlockSpec((1, s, HEAD_DIM), lambda bi, g, i: (bi, 0, COL_VA + g)),
            pl.BlockSpec((1, N_META, HEAD_DIM), lambda bi, g, i: (g, 0, 0)),
            pl.BlockSpec((1, N_META, HEAD_DIM), lambda bi, g, i: (g, 0, 0)),
        ],
        out_specs=pl.BlockSpec((1, tq, gw), lambda bi, g, i: (bi, i, g)),
        out_shape=jax.ShapeDtypeStruct((b, s, A_WIDTH), BF16),
        compiler_params=_params(("parallel", "parallel", "arbitrary")),
        name="global_attn",
    )(proj, proj, proj, km, vm)


def _t5_bucket_np(rel):
    nb = N_BUCKETS // 2
    max_exact = nb // 2
    bucket = np.where(rel > 0, nb, 0)
    n = np.abs(rel)
    nf = np.maximum(n, 1).astype(np.float32)
    large = max_exact + (np.log(nf / np.float32(max_exact)) / np.float32(math.log(MAX_DISTANCE / max_exact))
                         * np.float32(nb - max_exact)).astype(np.int32)
    large = np.minimum(large, nb - 1)
    return (bucket + np.where(n < max_exact, n, large)).astype(np.int32)


def _bucket_maps():
    i = np.arange(Q_BLOCK)[:, None]
    j = np.arange(3 * Q_BLOCK)[None, :]
    real = []
    for off in range(3):
        rel = j - off * Q_BLOCK - i
        real.append(np.where(np.abs(rel) <= WINDOW, _t5_bucket_np(rel), -1))
    m = np.arange(N_META)[None, :]
    first = _t5_bucket_np(m - (N_META + i))
    later = _t5_bucket_np(m - (N_META + i + Q_BLOCK))
    meta = [first, later, later]
    return np.stack(real).astype(np.int32), np.stack(meta).astype(np.int32)


def _bias_table_kernel(rb_ref, bm_ref, bmm_ref, o_ref, om_ref):
    h = pl.program_id(1)
    bm = bm_ref[0]
    bmm = bmm_ref[0]
    acc = jnp.full(bm.shape, NEG_INF, F32)
    accm = jnp.full(bmm.shape, NEG_INF, F32)
    for k in range(N_BUCKETS):
        val = rb_ref[k, h]
        acc = jnp.where(bm == k, val, acc)
        accm = jnp.where(bmm == k, val, accm)
    o_ref[0, 0] = acc
    om_ref[0, 0] = accm


def _bias_tables(rel_bias):
    bm, bmm = _bucket_maps()
    kw = 3 * Q_BLOCK
    tab, tabm = pl.pallas_call(
        _bias_table_kernel,
        grid=(3, B_HEADS),
        in_specs=[
            pl.BlockSpec(memory_space=pltpu.SMEM),
            pl.BlockSpec((1, Q_BLOCK, kw), lambda v, h: (v, 0, 0)),
            pl.BlockSpec((1, Q_BLOCK, N_META), lambda v, h: (v, 0, 0)),
        ],
        out_specs=[
            pl.BlockSpec((1, 1, Q_BLOCK, kw), lambda v, h: (v, h, 0, 0)),
            pl.BlockSpec((1, 1, Q_BLOCK, N_META), lambda v, h: (v, h, 0, 0)),
        ],
        out_shape=[
            jax.ShapeDtypeStruct((3, B_HEADS, Q_BLOCK, kw), F32),
            jax.ShapeDtypeStruct((3, B_HEADS, Q_BLOCK, N_META), F32),
        ],
        compiler_params=_params(("arbitrary", "arbitrary")),
        name="bias_tables",
    )(rel_bias, jnp.asarray(bm), jnp.asarray(bmm))
    rows = REP * Q_BLOCK
    return (tab.reshape(3, B_KV_HEADS, rows, kw), tabm.reshape(3, B_KV_HEADS, rows, N_META))


def _window_attn_kernel(q_ref, k_ref, v_ref, km_ref, vm_ref, tab_ref, tabm_ref, sink_ref, o_ref, *, nq):
    jb = pl.program_id(2)
    nblk = k_ref.shape[1] // Q_BLOCK
    km = km_ref[0]
    vm = vm_ref[0]
    sink = sink_ref[0]
    for t in range(nq):
        n = jb * nq + t
        qs = _stack_heads(q_ref[0, t * Q_BLOCK:(t + 1) * Q_BLOCK, :])
        var = jnp.where(n == 0, 0, jnp.where(n == nblk - 1, 2, 1))
        start = pl.multiple_of(jnp.clip(n - 1, 0, nblk - 3) * Q_BLOCK, Q_BLOCK)
        kb = k_ref[0, pl.ds(start, 3 * Q_BLOCK), :]
        vb = v_ref[0, pl.ds(start, 3 * Q_BLOCK), :]
        s = lax.dot_general(qs, kb, _NT, preferred_element_type=F32) + tab_ref[var, 0]
        sm = lax.dot_general(qs, km, _NT, preferred_element_type=F32) + tabm_ref[var, 0]
        m = jnp.maximum(jnp.maximum(jnp.max(s, axis=-1, keepdims=True),
                                    jnp.max(sm, axis=-1, keepdims=True)), sink)
        p = jnp.exp(s - m)
        pm = jnp.exp(sm - m)
        l = jnp.sum(p, axis=-1, keepdims=True) + jnp.sum(pm, axis=-1, keepdims=True) + jnp.exp(sink - m)
        o = (jnp.dot(p.astype(BF16), vb, preferred_element_type=F32)
             + jnp.dot(pm.astype(BF16), vm, preferred_element_type=F32)) / l
        for r in range(REP):
            o_ref[0, t * Q_BLOCK:(t + 1) * Q_BLOCK, r * HEAD_DIM:(r + 1) * HEAD_DIM] = (
                o[r * Q_BLOCK:(r + 1) * Q_BLOCK].astype(BF16))


def _window_attn(proj, km, vm, tab, tabm, sink_rows, *, nq):
    b, s, _ = proj.shape
    assert s // Q_BLOCK >= 3 and (s // Q_BLOCK) % nq == 0
    gw = REP * HEAD_DIM
    rows = REP * Q_BLOCK
    kw = 3 * Q_BLOCK
    tq = nq * Q_BLOCK
    return pl.pallas_call(
        functools.partial(_window_attn_kernel, nq=nq),
        grid=(b, B_KV_HEADS, s // tq),
        in_specs=[
            pl.BlockSpec((1, tq, gw), lambda bi, g, i: (bi, i, COL_QB // REP + g)),
            pl.BlockSpec((1, s, HEAD_DIM), lambda bi, g, i: (bi, 0, COL_KB + g)),
            pl.BlockSpec((1, s, HEAD_DIM), lambda bi, g, i: (bi, 0, COL_VB + g)),
            pl.BlockSpec((1, N_META, HEAD_DIM), lambda bi, g, i: (g, 0, 0)),
            pl.BlockSpec((1, N_META, HEAD_DIM), lambda bi, g, i: (g, 0, 0)),
            pl.BlockSpec((3, 1, rows, kw), lambda bi, g, i: (0, g, 0, 0)),
            pl.BlockSpec((3, 1, rows, N_META), lambda bi, g, i: (0, g, 0, 0)),
            pl.BlockSpec((1, rows, 1), lambda bi, g, i: (g, 0, 0)),
        ],
        out_specs=pl.BlockSpec((1, tq, gw), lambda bi, g, i: (bi, i, g)),
        out_shape=jax.ShapeDtypeStruct((b, s, B_WIDTH), BF16),
        compiler_params=_params(("parallel", "parallel", "arbitrary")),
        name="window_attn",
    )(proj, proj, proj, km, vm, tab, tabm, sink_rows)


def _sigmoid(x):
    return 1.0 / (1.0 + jnp.exp(-x))


def _route(logits):
    lane = lax.broadcasted_iota(jnp.int32, logits.shape, 1).astype(F32)
    ninf = jnp.float32(-jnp.inf)
    big = jnp.float32(ROUTER_COLS)
    is_g = lane < N_GROUPS
    lg = jnp.where(is_g, logits, ninf)
    mg = jnp.max(lg, axis=-1, keepdims=True)
    gidx = jnp.min(jnp.where(lg == mg, lane, big), axis=-1, keepdims=True)
    p_top = 1.0 / jnp.sum(jnp.where(is_g, jnp.exp(lg - mg), 0.0), axis=-1, keepdims=True)
    lo = N_GROUPS + EXPERTS_PER_GROUP * gidx
    sel = (lane >= lo) & (lane < lo + EXPERTS_PER_GROUP)
    le = jnp.where(sel, logits, ninf)
    v1 = jnp.max(le, axis=-1, keepdims=True)
    i1 = jnp.min(jnp.where(sel & (le == v1), lane, big), axis=-1, keepdims=True)
    rest = sel & (lane != i1)
    le2 = jnp.where(rest, logits, ninf)
    v2 = jnp.max(le2, axis=-1, keepdims=True)
    i2 = jnp.min(jnp.where(rest & (le2 == v2), lane, big), axis=-1, keepdims=True)
    t = jnp.exp(v2 - v1)
    w1 = p_top / (1.0 + t)
    w2 = p_top * t / (1.0 + t)
    return i1 - N_GROUPS, i2 - N_GROUPS, w1, w2


def _merge_kernel(oa_ref, ob_ref, ga0_ref, ga1_ref, gb0_ref, gb1_ref, x_ref, wa_ref, wb_ref, wo_ref, gn_ref,
                  wr_ref, br_ref, h2_ref, u2_ref, info_ref):
    ya = jnp.dot(oa_ref[...], wa_ref[...], preferred_element_type=F32)
    yb = jnp.dot(ob_ref[...], wb_ref[...], preferred_element_type=F32)
    ga = jnp.concatenate([ga0_ref[...], ga1_ref[...]], axis=1).astype(F32)
    gb = jnp.concatenate([gb0_ref[...], gb1_ref[...]], axis=1).astype(F32)
    mixed = _sigmoid(ga) * ya + _sigmoid(gb) * yb
    h2 = x_ref[...] + jnp.dot(mixed.astype(BF16), wo_ref[...], preferred_element_type=F32)
    h2_ref[...] = h2
    ms = jnp.mean(h2 * h2, axis=-1, keepdims=True)
    u = h2 * lax.rsqrt(ms + RMS_EPS) * gn_ref[...]
    u_hi = u.astype(BF16)
    u_hi32 = u_hi.astype(F32)
    u_lo = (u - u_hi32).astype(BF16)
    lg = (jnp.dot(u_hi, wr_ref[...], preferred_element_type=F32)
          + jnp.dot(u_lo, wr_ref[...], preferred_element_type=F32))
    logits = lg[:, :ROUTER_COLS] + lg[:, ROUTER_COLS:] + br_ref[...]
    e1, e2, w1, w2 = _route(logits)
    lane = lax.broadcasted_iota(jnp.int32, logits.shape, 1)
    info_ref[...] = jnp.where(lane == 0, e1, jnp.where(lane == 1, e2, jnp.where(lane == 2, w1,
                              jnp.where(lane == 3, w2, 0.0))))
    half = D_MODEL // 2
    lo_bits = pltpu.bitcast(u_hi32[:, :half], jnp.uint32) >> 16
    hi_bits = pltpu.bitcast(u_hi32[:, half:], jnp.uint32)
    u2_ref[...] = hi_bits | lo_bits


def _merge_route(oa, ob, proj2d, x2d, wa, wb, wo, norm_ffn, wr, br, *, tm):
    t_rows = x2d.shape[0]
    gw = D_MODEL // 2
    ga_blk = COL_GA * HEAD_DIM // gw
    gb_blk = COL_GB * HEAD_DIM // gw
    return pl.pallas_call(
        _merge_kernel,
        grid=(t_rows // tm,),
        in_specs=[
            pl.BlockSpec((tm, A_WIDTH), lambda i: (i, 0)),
            pl.BlockSpec((tm, B_WIDTH), lambda i: (i, 0)),
            pl.BlockSpec((tm, gw), lambda i: (i, ga_blk)),
            pl.BlockSpec((tm, gw), lambda i: (i, ga_blk + 1)),
            pl.BlockSpec((tm, gw), lambda i: (i, gb_blk)),
            pl.BlockSpec((tm, gw), lambda i: (i, gb_blk + 1)),
            pl.BlockSpec((tm, D_MODEL), lambda i: (i, 0)),
            _const_spec((A_WIDTH, D_MODEL)),
            _const_spec((B_WIDTH, D_MODEL)),
            _const_spec((D_MODEL, D_MODEL)),
            _const_spec((1, D_MODEL)),
            _const_spec((D_MODEL, 2 * ROUTER_COLS)),
            _const_spec((1, ROUTER_COLS)),
        ],
        out_specs=[
            pl.BlockSpec((tm, D_MODEL), lambda i: (i, 0)),
            pl.BlockSpec((tm, D_MODEL // 2), lambda i: (i, 0)),
            pl.BlockSpec((tm, ROUTER_COLS), lambda i: (i, 0)),
        ],
        out_shape=[
            jax.ShapeDtypeStruct((t_rows, D_MODEL), F32),
            jax.ShapeDtypeStruct((t_rows, D_MODEL // 2), jnp.uint32),
            jax.ShapeDtypeStruct((t_rows, ROUTER_COLS), F32),
        ],
        compiler_params=_params(("parallel",)),
        name="merge_route",
    )(oa, ob, proj2d, proj2d, proj2d, proj2d, x2d, wa, wb, wo, norm_ffn, wr, br)


SEG_NTILES = 2 * N_EXPERTS


def _positions_kernel(e1_ref, e2_ref, pos1_ref, pos2_ref, te_ref, seg_ref):
    e1 = e1_ref[...]
    e2 = e2_ref[...]
    rows = e1.shape[0]
    r_i = lax.broadcasted_iota(jnp.int32, (LANES, LANES), 0)
    c_i = lax.broadcasted_iota(jnp.int32, (LANES, LANES), 1)
    upper = (r_i < c_i).astype(BF16)
    rr = lax.broadcasted_iota(jnp.int32, (rows, rows), 0)
    rc = lax.broadcasted_iota(jnp.int32, (rows, rows), 1)
    lower = (rc < rr).astype(BF16)
    seg_lane = lax.broadcasted_iota(jnp.int32, seg_ref.shape, 1)
    tile_row = lax.broadcasted_iota(jnp.int32, te_ref.shape, 1).astype(F32) * MOE_TILE
    base = jnp.zeros((1, 1), F32)
    pos1 = jnp.zeros(e1.shape, F32)
    pos2 = jnp.zeros(e1.shape, F32)
    seg = jnp.zeros(seg_ref.shape, F32)
    tile_expert = jnp.zeros(te_ref.shape, F32)
    for e in range(N_EXPERTS):
        m1 = e1 == e
        m2 = e2 == e
        m = jnp.where(m1 | m2, 1.0, 0.0)
        lane_pre = jnp.dot(m.astype(BF16), upper, preferred_element_type=F32)
        row_tot = jnp.broadcast_to(jnp.sum(m, axis=-1, keepdims=True), m.shape)
        row_pre = jnp.dot(lower, row_tot.astype(BF16), preferred_element_type=F32)
        total = jnp.sum(row_tot[:, 0:1], axis=0, keepdims=True)
        p = base + row_pre + lane_pre
        pos1 = jnp.where(m1, p, pos1)
        pos2 = jnp.where(m2, p, pos2)
        padded = jnp.floor((total + (MOE_TILE - 1)) * (1.0 / MOE_TILE)) * MOE_TILE
        base = base + padded
        seg = seg + jnp.where(seg_lane == e, base, 0.0) + jnp.where(seg_lane == N_EXPERTS + e, total, 0.0)
        tile_expert = tile_expert + jnp.where(tile_row >= base, 1.0, 0.0)
    seg = seg + jnp.where(seg_lane == SEG_NTILES, base * (1.0 / MOE_TILE), 0.0)
    pos1_ref[...] = pos1.astype(jnp.int32)
    pos2_ref[...] = pos2.astype(jnp.int32)
    te_ref[...] = jnp.minimum(tile_expert, N_EXPERTS - 1).astype(jnp.int32)
    seg_ref[...] = seg.astype(jnp.int32)


def _positions(e1, e2, n_tiles):
    rows = e1.shape[0]
    ntp = -(-n_tiles // LANES) * LANES
    full = lambda shape: pl.BlockSpec(shape, lambda: (0,) * len(shape))
    return pl.pallas_call(
        _positions_kernel,
        in_specs=[full((rows, LANES)), full((rows, LANES))],
        out_specs=[full((rows, LANES)), full((rows, LANES)), full((1, ntp)), full((1, LANES))],
        out_shape=[
            jax.ShapeDtypeStruct((rows, LANES), jnp.int32),
            jax.ShapeDtypeStruct((rows, LANES), jnp.int32),
            jax.ShapeDtypeStruct((1, ntp), jnp.int32),
            jax.ShapeDtypeStruct((1, LANES), jnp.int32),
        ],
        compiler_params=pltpu.CompilerParams(vmem_limit_bytes=VMEM_LIMIT),
        name="positions",
    )(e1, e2)


def _dispatch_kernel(seg_ref, pos1_ref, pos2_ref, u_ref, xs_ref, z_ref, zsem, sem, *, tb):
    i = pl.program_id(0)

    def zero_copy(e):
        end = seg_ref[e]
        start = pl.multiple_of(end - MOE_TILE, MOE_TILE)
        return pltpu.make_async_copy(z_ref, xs_ref.at[pl.ds(start, MOE_TILE)], zsem)

    def nonempty(e):
        return seg_ref[e] > (seg_ref[e - 1] if e > 0 else 0)

    n_tiles = xs_ref.shape[0] // MOE_TILE

    def tail_copy(k):
        start = pl.multiple_of((seg_ref[SEG_NTILES] + k) * MOE_TILE, MOE_TILE)
        return pltpu.make_async_copy(z_ref, xs_ref.at[pl.ds(start, MOE_TILE)], zsem)

    def tail_exists(k):
        return seg_ref[SEG_NTILES] + k < n_tiles

    @pl.when(i == 0)
    def _():
        z_ref[...] = jnp.zeros(z_ref.shape, z_ref.dtype)
        for e in range(N_EXPERTS):
            @pl.when(nonempty(e))
            def _():
                zero_copy(e).start()

            @pl.when(tail_exists(e))
            def _():
                tail_copy(e).start()
        for e in range(N_EXPERTS):
            @pl.when(nonempty(e))
            def _():
                zero_copy(e).wait()

            @pl.when(tail_exists(e))
            def _():
                tail_copy(e).wait()

    def row_copy(t, pos_ref):
        return pltpu.make_async_copy(u_ref.at[pl.ds(i * tb + t, 1)], xs_ref.at[pl.ds(pos_ref[t], 1)], sem)

    def issue(t, carry):
        row_copy(t, pos1_ref).start()
        row_copy(t, pos2_ref).start()
        return carry

    lax.fori_loop(0, tb, issue, 0)

    def drain(t, carry):
        row_copy(t, pos1_ref).wait()
        row_copy(t, pos2_ref).wait()
        return carry

    lax.fori_loop(0, tb, drain, 0)


def _dispatch(seg, pos1, pos2, u2p, n_tiles, *, tb):
    t_rows, width = u2p.shape
    return pl.pallas_call(
        functools.partial(_dispatch_kernel, tb=tb),
        grid_spec=pltpu.PrefetchScalarGridSpec(
            num_scalar_prefetch=1,
            grid=(t_rows // tb,),
            in_specs=[
                pl.BlockSpec((tb,), lambda i, seg: (i,), memory_space=pltpu.SMEM),
                pl.BlockSpec((tb,), lambda i, seg: (i,), memory_space=pltpu.SMEM),
                pl.BlockSpec(memory_space=pl.ANY),
            ],
            out_specs=pl.BlockSpec(memory_space=pl.ANY),
            scratch_shapes=[
                pltpu.VMEM((MOE_TILE, width), u2p.dtype),
                pltpu.SemaphoreType.DMA(()),
                pltpu.SemaphoreType.DMA(()),
            ],
        ),
        out_shape=jax.ShapeDtypeStruct((n_tiles * MOE_TILE, width), u2p.dtype),
        compiler_params=_params(("arbitrary",)),
        name="dispatch",
    )(seg, pos1, pos2, u2p)


def _moe_kernel(te_ref, seg_ref, x_ref, wg_ref, wu_ref, wd_ref, y_ref):
    i = pl.program_id(0)

    @pl.when(i < seg_ref[SEG_NTILES])
    def _():
        xw = x_ref[...]
        half = D_MODEL // 2
        lo = pltpu.bitcast(xw << 16, F32).astype(BF16)
        hi = pltpu.bitcast(xw & jnp.uint32(0xFFFF0000), F32).astype(BF16)
        hg = (jnp.dot(lo, wg_ref[0, :half, :], preferred_element_type=F32)
              + jnp.dot(hi, wg_ref[0, half:, :], preferred_element_type=F32))
        hu = (jnp.dot(lo, wu_ref[0, :half, :], preferred_element_type=F32)
              + jnp.dot(hi, wu_ref[0, half:, :], preferred_element_type=F32))
        hid = hg * _sigmoid(hg) * hu
        y_ref[...] = jnp.dot(hid.astype(BF16), wd_ref[0], preferred_element_type=F32)

    @pl.when(i >= seg_ref[SEG_NTILES])
    def _():
        y_ref[...] = jnp.zeros(y_ref.shape, y_ref.dtype)


def _moe(tile_expert, seg, xs, wg, wu, wd):
    n_tiles = xs.shape[0] // MOE_TILE

    def row_map(i, te, seg):
        return (jnp.minimum(i, seg[SEG_NTILES] - 1), 0)

    def w_map(i, te, seg):
        return (te[jnp.minimum(i, seg[SEG_NTILES] - 1)], 0, 0)

    return pl.pallas_call(
        _moe_kernel,
        grid_spec=pltpu.PrefetchScalarGridSpec(
            num_scalar_prefetch=2,
            grid=(n_tiles,),
            in_specs=[
                pl.BlockSpec((MOE_TILE, D_MODEL // 2), row_map),
                pl.BlockSpec((1, D_MODEL, EXPERT_FF), w_map),
                pl.BlockSpec((1, D_MODEL, EXPERT_FF), w_map),
                pl.BlockSpec((1, EXPERT_FF, D_MODEL), w_map),
            ],
            out_specs=pl.BlockSpec((MOE_TILE, D_MODEL), lambda i, te, seg: (i, 0)),
        ),
        out_shape=jax.ShapeDtypeStruct((n_tiles * MOE_TILE, D_MODEL), F32),
        compiler_params=_params(("arbitrary",)),
        name="moe",
    )(tile_expert, seg, xs, wg, wu, wd)


def _final_kernel(pos1_ref, pos2_ref, h2_ref, info_ref, g_ref, ys_ref, o_ref, ybuf, sem, *, tm):
    def row_copy(t, k, pos_ref):
        return pltpu.make_async_copy(ys_ref.at[pl.ds(pos_ref[t], 1)], ybuf.at[k, pl.ds(t, 1)], sem)

    def issue(t, carry):
        row_copy(t, 0, pos1_ref).start()
        row_copy(t, 1, pos2_ref).start()
        return carry

    lax.fori_loop(0, tm, issue, 0)

    def drain(t, carry):
        row_copy(t, 0, pos1_ref).wait()
        row_copy(t, 1, pos2_ref).wait()
        return carry

    lax.fori_loop(0, tm, drain, 0)

    info = info_ref[...]
    h = h2_ref[...] + info[:, 2:3] * ybuf[0] + info[:, 3:4] * ybuf[1]
    ms = jnp.mean(h * h, axis=-1, keepdims=True)
    o_ref[...] = h * lax.rsqrt(ms + RMS_EPS) * g_ref[...]


def _final(pos1, pos2, h2, info, final_norm, ys, *, tm):
    t_rows = h2.shape[0]
    return pl.pallas_call(
        functools.partial(_final_kernel, tm=tm),
        grid=(t_rows // tm,),
        in_specs=[
            pl.BlockSpec((tm,), lambda i: (i,), memory_space=pltpu.SMEM),
            pl.BlockSpec((tm,), lambda i: (i,), memory_space=pltpu.SMEM),
            pl.BlockSpec((tm, D_MODEL), lambda i: (i, 0)),
            pl.BlockSpec((tm, ROUTER_COLS), lambda i: (i, 0)),
            pl.BlockSpec((1, D_MODEL), lambda i: (0, 0)),
            pl.BlockSpec(memory_space=pl.ANY),
        ],
        out_specs=pl.BlockSpec((tm, D_MODEL), lambda i: (i, 0)),
        out_shape=jax.ShapeDtypeStruct((t_rows, D_MODEL), F32),
        scratch_shapes=[pltpu.VMEM((2, tm, D_MODEL), F32), pltpu.SemaphoreType.DMA(())],
        compiler_params=_params(("arbitrary",)),
        name="final",
    )(pos1, pos2, h2, info, final_norm, ys)


def _rope_tables(row, col):
    n_freq = HEAD_DIM // 4
    inv_freq = ROPE_THETA ** (-jnp.arange(n_freq, dtype=F32) / n_freq)
    ra = row.astype(F32)[:, None] * inv_freq
    ca = col.astype(F32)[:, None] * inv_freq
    cos = jnp.concatenate([jnp.cos(ra), jnp.cos(ra), jnp.cos(ca), jnp.cos(ca)], axis=-1)
    sin = jnp.concatenate([-jnp.sin(ra), jnp.sin(ra), -jnp.sin(ca), jnp.sin(ca)], axis=-1)
    return cos, sin


def _split_bf16(w):
    hi = w.astype(BF16)
    lo = (w - hi.astype(F32)).astype(BF16)
    return hi, lo


def _tile(n, pref):
    t = min(n, pref)
    assert n % t == 0, (n, pref)
    return t


def _encode_group(x, shared):
    (norm_mix, w_in, col_scale, q_norm, k_norm, meta_kv, tab, tabm, sink_rows, wa, wb, wo, norm_ffn, wr, br,
     wg, wu, wd, final_norm) = shared
    b, s, _ = x.shape
    t_rows = b * s
    x2d = x.reshape(t_rows, D_MODEL)
    tok = jnp.arange(s)
    cos, sin = _rope_tables(tok // GRID_W, tok % GRID_W)
    proj2d = _in_proj(x2d, norm_mix, w_in, col_scale, cos, sin, q_norm, k_norm, tm=_tile(s, 1024))
    proj = proj2d.reshape(b, s, IN_COLS)
    ka_m, va_m, kb_m, vb_m = meta_kv
    oa = _global_attn(proj, ka_m, va_m, tq=_tile(s, 256), ck=_tile(s, 1024))
    nblk = s // Q_BLOCK
    ob = _window_attn(proj, kb_m, vb_m, tab, tabm, sink_rows, nq=4 if nblk % 4 == 0 else 1)
    h2, u2p, info = _merge_route(oa.reshape(t_rows, A_WIDTH), ob.reshape(t_rows, B_WIDTH), proj2d, x2d,
                                 wa, wb, wo, norm_ffn, wr, br, tm=_tile(t_rows, 256))
    e1 = info[:, 0].reshape(t_rows // LANES, LANES)
    e2 = info[:, 1].reshape(t_rows // LANES, LANES)
    n_tiles = 2 * t_rows // MOE_TILE + N_EXPERTS
    pos1, pos2, tile_expert, seg = _positions(e1, e2, n_tiles)
    pos1 = pos1.reshape(t_rows)
    pos2 = pos2.reshape(t_rows)
    seg = seg.reshape(LANES)
    xs = _dispatch(seg, pos1, pos2, u2p, n_tiles, tb=_tile(t_rows, 1024))
    ys = _moe(tile_expert.reshape(-1)[:n_tiles], seg, xs, wg, wu, wd)
    out = _final(pos1, pos2, h2, info, final_norm, ys, tm=_tile(t_rows, 256))
    return out.reshape(b, s, D_MODEL)


def kernel(x_prompt, x_sample, meta_tokens, rel_bias, final_norm, norm_mix, w_in, q_norm, k_norm, sink,
           w_branch_a, w_branch_b, w_out, norm_ffn, w_router_g, b_router_g, w_router_e, b_router_e,
           w_gate, w_up, w_down):
    assert norm_mix.shape[0] == 1, "single-layer encoder"
    w_in_b = w_in[0].astype(BF16)
    norm_mix2 = norm_mix[0].reshape(1, D_MODEL)
    q_norm2 = q_norm[0].reshape(1, HEAD_DIM)
    k_norm2 = k_norm[0].reshape(1, HEAD_DIM)
    col = jnp.arange(IN_COLS)
    is_qb = (col >= COL_QB * HEAD_DIM) & (col < COL_KB * HEAD_DIM)
    col_scale = jnp.where(is_qb, SCORE_SCALE, 1.0).astype(F32).reshape(1, IN_COLS)

    cos_m, sin_m = _rope_tables(jnp.full((N_META,), -1), jnp.arange(N_META))
    proj_m = _in_proj(meta_tokens, norm_mix2, w_in_b, col_scale, cos_m, sin_m, q_norm2, k_norm2, tm=N_META)

    def meta_heads(c0):
        blk = proj_m[:, c0 * HEAD_DIM:(c0 + A_KV_HEADS) * HEAD_DIM]
        return blk.reshape(N_META, A_KV_HEADS, HEAD_DIM).transpose(1, 0, 2)

    meta_kv = tuple(meta_heads(c) for c in (COL_KA, COL_VA, COL_KB, COL_VB))

    tab, tabm = _bias_tables(rel_bias)
    sink_rows = jnp.repeat(sink[0].astype(F32), Q_BLOCK).reshape(B_KV_HEADS, REP * Q_BLOCK, 1)

    wr_full = jnp.zeros((D_MODEL, ROUTER_COLS), F32)
    wr_full = wr_full.at[:, :N_GROUPS].set(w_router_g[0]).at[:, N_GROUPS:N_GROUPS + N_EXPERTS].set(w_router_e[0])
    wr_hi, wr_lo = _split_bf16(wr_full)
    wr = jnp.concatenate([wr_hi, wr_lo], axis=1)
    br = jnp.zeros((1, ROUTER_COLS), F32)
    br = br.at[0, :N_GROUPS].set(b_router_g[0]).at[0, N_GROUPS:N_GROUPS + N_EXPERTS].set(b_router_e[0])

    shared = (norm_mix2, w_in_b, col_scale, q_norm2, k_norm2, meta_kv, tab, tabm, sink_rows,
              w_branch_a[0].astype(BF16), w_branch_b[0].astype(BF16), w_out[0].astype(BF16),
              norm_ffn[0].reshape(1, D_MODEL), wr, br,
              w_gate[0].astype(BF16), w_up[0].astype(BF16), w_down[0].astype(BF16),
              final_norm.reshape(1, D_MODEL))
    return (_encode_group(x_prompt, shared), _encode_group(x_sample, shared))
```

```python
import functools
import math

import numpy as np
import jax
import jax.numpy as jnp
from jax import lax
from jax.experimental import pallas as pl
from jax.experimental.pallas import tpu as pltpu

F32 = jnp.float32
BF16 = jnp.bfloat16

D_MODEL = 2048
HEAD_DIM = 128
A_HEADS = 8
A_KV_HEADS = 2
B_HEADS = 8
B_KV_HEADS = 2
REP = A_HEADS // A_KV_HEADS
A_WIDTH = A_HEADS * HEAD_DIM
B_WIDTH = B_HEADS * HEAD_DIM
KV_WIDTH = A_KV_HEADS * HEAD_DIM
IN_COLS = A_WIDTH + 2 * KV_WIDTH + B_WIDTH + 2 * KV_WIDTH + 2 * D_MODEL
Q_BLOCK = 128
WINDOW = 128
N_META = 16
GRID_W = 64
ROPE_THETA = 10000.0
N_BUCKETS = 32
MAX_DISTANCE = 128
N_GROUPS = 4
EXPERTS_PER_GROUP = 4
N_EXPERTS = N_GROUPS * EXPERTS_PER_GROUP
EXPERT_FF = 1024
RMS_EPS = 1e-6
NEG_INF = -1e30
SCORE_SCALE = HEAD_DIM ** -0.5
LOG2E = math.log2(math.e)

COL_QA = 0
COL_KA = A_WIDTH // HEAD_DIM
COL_VA = COL_KA + A_KV_HEADS
COL_QB = COL_VA + A_KV_HEADS
COL_KB = COL_QB + B_HEADS
COL_VB = COL_KB + B_KV_HEADS
COL_GA = COL_VB + B_KV_HEADS
COL_GB = COL_GA + D_MODEL // HEAD_DIM

LANES = 128
VMEM_LIMIT = 56 * 1024 * 1024
MOE_TILE = 512
ROUTER_COLS = 128

_NT = (((1,), (1,)), ((), ()))


def _params(sem, vmem=VMEM_LIMIT):
    return pltpu.CompilerParams(dimension_semantics=sem, vmem_limit_bytes=vmem)


def _const_spec(shape):
    nd = len(shape)
    return pl.BlockSpec(shape, lambda *_: (0,) * nd, pipeline_mode=pl.Buffered(1))


def _norm_rope(a, gain, cos, sin, out_scale):
    ms = jnp.mean(a * a, axis=-1, keepdims=True)
    y = a * lax.rsqrt(ms + RMS_EPS) * gain
    lane = lax.broadcasted_iota(jnp.int32, y.shape, 1)
    first_half = (lane & 32) == 0
    partner = jnp.where(first_half, pltpu.roll(y, 96, 1), pltpu.roll(y, 32, 1))
    return (y * cos + partner * sin) * out_scale


def _in_proj_kernel(x_ref, g_ref, w_ref, cs_ref, cos_ref, sin_ref, qn_ref, kn_ref, o_ref, u_ref):
    j = pl.program_id(1)

    @pl.when(j == 0)
    def _():
        x = x_ref[...]
        ms = jnp.mean(x * x, axis=-1, keepdims=True)
        u_ref[...] = (x * lax.rsqrt(ms + RMS_EPS) * g_ref[...]).astype(BF16)

    acc = jnp.dot(u_ref[...], w_ref[...], preferred_element_type=F32) * cs_ref[...]

    @pl.when(j == 0)
    def _():
        cos = cos_ref[...]
        sin = sin_ref[...]
        for h in range(A_HEADS):
            sl = slice(h * HEAD_DIM, (h + 1) * HEAD_DIM)
            o_ref[:, sl] = _norm_rope(acc[:, sl], qn_ref[...], cos, sin, SCORE_SCALE * LOG2E).astype(BF16)

    @pl.when(j == 1)
    def _():
        cos = cos_ref[...]
        sin = sin_ref[...]
        for h in range(A_KV_HEADS):
            sl = slice(h * HEAD_DIM, (h + 1) * HEAD_DIM)
            o_ref[:, sl] = _norm_rope(acc[:, sl], kn_ref[...], cos, sin, 1.0).astype(BF16)
        o_ref[:, KV_WIDTH:] = acc[:, KV_WIDTH:].astype(BF16)

    @pl.when(j >= 2)
    def _():
        o_ref[...] = acc.astype(BF16)


def _in_proj(x2d, norm_g, w_bf16, col_scale, cos, sin, q_norm, k_norm, *, tm):
    t_rows = x2d.shape[0]
    tn = A_WIDTH
    pos_tiles = cos.shape[0] // tm
    return pl.pallas_call(
        _in_proj_kernel,
        grid=(t_rows // tm, IN_COLS // tn),
        in_specs=[
            pl.BlockSpec((tm, D_MODEL), lambda i, j: (i, 0)),
            pl.BlockSpec((1, D_MODEL), lambda i, j: (0, 0)),
            pl.BlockSpec((D_MODEL, tn), lambda i, j: (0, j)),
            pl.BlockSpec((1, tn), lambda i, j: (0, j)),
            pl.BlockSpec((tm, HEAD_DIM), lambda i, j: (i % pos_tiles, 0)),
            pl.BlockSpec((tm, HEAD_DIM), lambda i, j: (i % pos_tiles, 0)),
            pl.BlockSpec((1, HEAD_DIM), lambda i, j: (0, 0)),
            pl.BlockSpec((1, HEAD_DIM), lambda i, j: (0, 0)),
        ],
        out_specs=pl.BlockSpec((tm, tn), lambda i, j: (i, j)),
        out_shape=jax.ShapeDtypeStruct((t_rows, IN_COLS), BF16),
        scratch_shapes=[pltpu.VMEM((tm, D_MODEL), BF16)],
        compiler_params=_params(("parallel", "arbitrary")),
        name="in_proj",
    )(x2d, norm_g, w_bf16, col_scale, cos, sin, q_norm, k_norm)


def _stack_heads(q):
    return jnp.concatenate([q[:, r * HEAD_DIM:(r + 1) * HEAD_DIM] for r in range(REP)], axis=0)


def _global_attn_kernel(q_ref, k_ref, v_ref, km_ref, vm_ref, o_ref, vx_ref, vmx_ref, *, tq, ck):
    @pl.when(pl.program_id(2) == 0)
    def _():
        vx_ref[:, :HEAD_DIM] = v_ref[0]
        vx_ref[:, HEAD_DIM:] = jnp.ones((v_ref.shape[1], HEAD_DIM), BF16)
        vmx_ref[:, :HEAD_DIM] = vm_ref[0]
        vmx_ref[:, HEAD_DIM:] = jnp.ones((N_META, HEAD_DIM), BF16)

    qs = _stack_heads(q_ref[0])
    n_chunks = k_ref.shape[1] // ck

    def scores(c):
        return lax.dot_general(qs, k_ref[0, c * ck:(c + 1) * ck, :], _NT, preferred_element_type=F32)

    s = lax.dot_general(qs, km_ref[0], _NT, preferred_element_type=F32)
    s_next = scores(0)
    m = jnp.max(s, axis=-1, keepdims=True)
    acc = jnp.dot(jnp.exp2(s - m).astype(BF16), vmx_ref[...], preferred_element_type=F32)
    for c in range(n_chunks):
        s = s_next
        if c + 1 < n_chunks:
            s_next = scores(c + 1)
        m_new = jnp.maximum(m, jnp.max(s, axis=-1, keepdims=True))
        p = jnp.exp2(s - m_new).astype(BF16)
        acc = jnp.exp2(m - m_new) * acc + jnp.dot(p, vx_ref[c * ck:(c + 1) * ck, :],
                                                  preferred_element_type=F32)
        m = m_new
    o = acc[:, :HEAD_DIM] / acc[:, HEAD_DIM:]
    for r in range(REP):
        o_ref[0, :, r * HEAD_DIM:(r + 1) * HEAD_DIM] = o[r * tq:(r + 1) * tq].astype(BF16)


def _global_attn(proj, km, vm, *, tq, ck):
    b, s, _ = proj.shape
    gw = REP * HEAD_DIM
    return pl.pallas_call(
        functools.partial(_global_attn_kernel, tq=tq, ck=ck),
        grid=(b, A_KV_HEADS, s // tq),
        in_specs=[
            pl.BlockSpec((1, tq, gw), lambda bi, g, i: (bi, i, g)),
            pl.BlockSpec((1, s, HEAD_DIM), lambda bi, g, i: (bi, 0, COL_KA + g)),
            pl.BlockSpec((1, s, HEAD_DIM), lambda bi, g, i: (bi, 0, COL_VA + g)),
            pl.BlockSpec((1, N_META, HEAD_DIM), lambda bi, g, i: (g, 0, 0)),
            pl.BlockSpec((1, N_META, HEAD_DIM), lambda bi, g, i: (g, 0, 0)),
        ],
        out_specs=pl.BlockSpec((1, tq, gw), lambda bi, g, i: (bi, i, g)),
        out_shape=jax.ShapeDtypeStruct((b, s, A_WIDTH), BF16),
        scratch_shapes=[pltpu.VMEM((s, 2 * HEAD_DIM), BF16), pltpu.VMEM((N_META, 2 * HEAD_DIM), BF16)],
        compiler_params=_params(("parallel", "parallel", "arbitrary")),
        name="global_attn",
    )(proj, proj, proj, km, vm)


def _t5_bucket_np(rel):
    nb = N_BUCKETS // 2
    max_exact = nb // 2
    bucket = np.where(rel > 0, nb, 0)
    n = np.abs(rel)
    nf = np.maximum(n, 1).astype(np.float32)
    large = max_exact + (np.log(nf / np.float32(max_exact)) / np.float32(math.log(MAX_DISTANCE / max_exact))
                         * np.float32(nb - max_exact)).astype(np.int32)
    large = np.minimum(large, nb - 1)
    return (bucket + np.where(n < max_exact, n, large)).astype(np.int32)


def _bucket_maps():
    i = np.arange(Q_BLOCK)[:, None]
    j = np.arange(3 * Q_BLOCK)[None, :]
    real = []
    for off in range(3):
        rel = j - off * Q_BLOCK - i
        real.append(np.where(np.abs(rel) <= WINDOW, _t5_bucket_np(rel), -1))
    m = np.arange(N_META)[None, :]
    first = _t5_bucket_np(m - (N_META + i))
    later = _t5_bucket_np(m - (N_META + i + Q_BLOCK))
    meta = [first, later, later]
    return np.stack(real).astype(np.int32), np.stack(meta).astype(np.int32)


def _bias_table_kernel(rb_ref, bm_ref, bmm_ref, o_ref, om_ref):
    h = pl.program_id(1)
    bm = bm_ref[0]
    bmm = bmm_ref[0]
    acc = jnp.full(bm.shape, NEG_INF, F32)
    accm = jnp.full(bmm.shape, NEG_INF, F32)
    for k in range(N_BUCKETS):
        val = rb_ref[k, h]
        acc = jnp.where(bm == k, val, acc)
        accm = jnp.where(bmm == k, val, accm)
    o_ref[0, 0] = acc
    om_ref[0, 0] = accm


def _bias_tables(rel_bias):
    bm, bmm = _bucket_maps()
    kw = 3 * Q_BLOCK
    tab, tabm = pl.pallas_call(
        _bias_table_kernel,
        grid=(3, B_HEADS),
        in_specs=[
            pl.BlockSpec(memory_space=pltpu.SMEM),
            pl.BlockSpec((1, Q_BLOCK, kw), lambda v, h: (v, 0, 0)),
            pl.BlockSpec((1, Q_BLOCK, N_META), lambda v, h: (v, 0, 0)),
        ],
        out_specs=[
            pl.BlockSpec((1, 1, Q_BLOCK, kw), lambda v, h: (v, h, 0, 0)),
            pl.BlockSpec((1, 1, Q_BLOCK, N_META), lambda v, h: (v, h, 0, 0)),
        ],
        out_shape=[
            jax.ShapeDtypeStruct((3, B_HEADS, Q_BLOCK, kw), F32),
            jax.ShapeDtypeStruct((3, B_HEADS, Q_BLOCK, N_META), F32),
        ],
        compiler_params=_params(("arbitrary", "arbitrary")),
        name="bias_tables",
    )(rel_bias, jnp.asarray(bm), jnp.asarray(bmm))
    rows = REP * Q_BLOCK
    return (tab.reshape(3, B_KV_HEADS, rows, kw), tabm.reshape(3, B_KV_HEADS, rows, N_META))


def _window_attn_kernel(q_ref, k_ref, v_ref, km_ref, vm_ref, tab_ref, tabm_ref, sink_ref, o_ref, *, nq):
    jb = pl.program_id(2)
    nblk = k_ref.shape[1] // Q_BLOCK
    km = km_ref[0]
    vm = vm_ref[0]
    sink = sink_ref[0]
    for t in range(nq):
        n = jb * nq + t
        qs = _stack_heads(q_ref[0, t * Q_BLOCK:(t + 1) * Q_BLOCK, :])
        var = jnp.where(n == 0, 0, jnp.where(n == nblk - 1, 2, 1))
        start = pl.multiple_of(jnp.clip(n - 1, 0, nblk - 3) * Q_BLOCK, Q_BLOCK)
        kb = k_ref[0, pl.ds(start, 3 * Q_BLOCK), :]
        vb = v_ref[0, pl.ds(start, 3 * Q_BLOCK), :]
        s = lax.dot_general(qs, kb, _NT, preferred_element_type=F32) + tab_ref[var, 0]
        sm = lax.dot_general(qs, km, _NT, preferred_element_type=F32) + tabm_ref[var, 0]
        m = jnp.maximum(jnp.maximum(jnp.max(s, axis=-1, keepdims=True),
                                    jnp.max(sm, axis=-1, keepdims=True)), sink)
        p = jnp.exp(s - m)
        pm = jnp.exp(sm - m)
        l = jnp.sum(p, axis=-1, keepdims=True) + jnp.sum(pm, axis=-1, keepdims=True) + jnp.exp(sink - m)
        o = (jnp.dot(p.astype(BF16), vb, preferred_element_type=F32)
             + jnp.dot(pm.astype(BF16), vm, preferred_element_type=F32)) / l
        for r in range(REP):
            o_ref[0, t * Q_BLOCK:(t + 1) * Q_BLOCK, r * HEAD_DIM:(r + 1) * HEAD_DIM] = (
                o[r * Q_BLOCK:(r + 1) * Q_BLOCK].astype(BF16))


def _window_attn(proj, km, vm, tab, tabm, sink_rows, *, nq):
    b, s, _ = proj.shape
    assert s // Q_BLOCK >= 3 and (s // Q_BLOCK) % nq == 0
    gw = REP * HEAD_DIM
    rows = REP * Q_BLOCK
    kw = 3 * Q_BLOCK
    tq = nq * Q_BLOCK
    return pl.pallas_call(
        functools.partial(_window_attn_kernel, nq=nq),
        grid=(b, B_KV_HEADS, s // tq),
        in_specs=[
            pl.BlockSpec((1, tq, gw), lambda bi, g, i: (bi, i, COL_QB // REP + g)),
            pl.BlockSpec((1, s, HEAD_DIM), lambda bi, g, i: (bi, 0, COL_KB + g)),
            pl.BlockSpec((1, s, HEAD_DIM), lambda bi, g, i: (bi, 0, COL_VB + g)),
            pl.BlockSpec((1, N_META, HEAD_DIM), lambda bi, g, i: (g, 0, 0)),
            pl.BlockSpec((1, N_META, HEAD_DIM), lambda bi, g, i: (g, 0, 0)),
            pl.BlockSpec((3, 1, rows, kw), lambda bi, g, i: (0, g, 0, 0)),
            pl.BlockSpec((3, 1, rows, N_META), lambda bi, g, i: (0, g, 0, 0)),
            pl.BlockSpec((1, rows, 1), lambda bi, g, i: (g, 0, 0)),
        ],
        out_specs=pl.BlockSpec((1, tq, gw), lambda bi, g, i: (bi, i, g)),
        out_shape=jax.ShapeDtypeStruct((b, s, B_WIDTH), BF16),
        compiler_params=_params(("parallel", "parallel", "arbitrary")),
        name="window_attn",
    )(proj, proj, proj, km, vm, tab, tabm, sink_rows)


def _sigmoid(x):
    return 1.0 / (1.0 + jnp.exp(-x))


def _route(logits):
    lane = lax.broadcasted_iota(jnp.int32, logits.shape, 1).astype(F32)
    ninf = jnp.float32(-jnp.inf)
    big = jnp.float32(ROUTER_COLS)
    is_g = lane < N_GROUPS
    lg = jnp.where(is_g, logits, ninf)
    mg = jnp.max(lg, axis=-1, keepdims=True)
    gidx = jnp.min(jnp.where(lg == mg, lane, big), axis=-1, keepdims=True)
    p_top = 1.0 / jnp.sum(jnp.where(is_g, jnp.exp(lg - mg), 0.0), axis=-1, keepdims=True)
    lo = N_GROUPS + EXPERTS_PER_GROUP * gidx
    sel = (lane >= lo) & (lane < lo + EXPERTS_PER_GROUP)
    le = jnp.where(sel, logits, ninf)
    v1 = jnp.max(le, axis=-1, keepdims=True)
    i1 = jnp.min(jnp.where(sel & (le == v1), lane, big), axis=-1, keepdims=True)
    rest = sel & (lane != i1)
    le2 = jnp.where(rest, logits, ninf)
    v2 = jnp.max(le2, axis=-1, keepdims=True)
    i2 = jnp.min(jnp.where(rest & (le2 == v2), lane, big), axis=-1, keepdims=True)
    t = jnp.exp(v2 - v1)
    w1 = p_top / (1.0 + t)
    w2 = p_top * t / (1.0 + t)
    return i1 - N_GROUPS, i2 - N_GROUPS, w1, w2


def _merge_kernel(oa_ref, ob_ref, ga0_ref, ga1_ref, gb0_ref, gb1_ref, x_ref, wa_ref, wb_ref, wo_ref, gn_ref,
                  wr_ref, br_ref, h2_ref, u2_ref, info_ref):
    ya = jnp.dot(oa_ref[...], wa_ref[...], preferred_element_type=F32)
    yb = jnp.dot(ob_ref[...], wb_ref[...], preferred_element_type=F32)
    ga = jnp.concatenate([ga0_ref[...], ga1_ref[...]], axis=1).astype(F32)
    gb = jnp.concatenate([gb0_ref[...], gb1_ref[...]], axis=1).astype(F32)
    mixed = _sigmoid(ga) * ya + _sigmoid(gb) * yb
    h2 = x_ref[...] + jnp.dot(mixed.astype(BF16), wo_ref[...], preferred_element_type=F32)
    h2_ref[...] = h2
    ms = jnp.mean(h2 * h2, axis=-1, keepdims=True)
    u = h2 * lax.rsqrt(ms + RMS_EPS) * gn_ref[...]
    u_hi = u.astype(BF16)
    u_hi32 = u_hi.astype(F32)
    u_lo = (u - u_hi32).astype(BF16)
    lg = (jnp.dot(u_hi, wr_ref[...], preferred_element_type=F32)
          + jnp.dot(u_lo, wr_ref[...], preferred_element_type=F32))
    logits = lg[:, :ROUTER_COLS] + lg[:, ROUTER_COLS:] + br_ref[...]
    e1, e2, w1, w2 = _route(logits)
    lane = lax.broadcasted_iota(jnp.int32, logits.shape, 1)
    info_ref[...] = jnp.where(lane == 0, e1, jnp.where(lane == 1, e2, jnp.where(lane == 2, w1,
                              jnp.where(lane == 3, w2, 0.0))))
    half = D_MODEL // 2
    lo_bits = pltpu.bitcast(u_hi32[:, :half], jnp.uint32) >> 16
    hi_bits = pltpu.bitcast(u_hi32[:, half:], jnp.uint32)
    u2_ref[...] = hi_bits | lo_bits


def _merge_route(oa, ob, proj2d, x2d, wa, wb, wo, norm_ffn, wr, br, *, tm):
    t_rows = x2d.shape[0]
    gw = D_MODEL // 2
    ga_blk = COL_GA * HEAD_DIM // gw
    gb_blk = COL_GB * HEAD_DIM // gw
    return pl.pallas_call(
        _merge_kernel,
        grid=(t_rows // tm,),
        in_specs=[
            pl.BlockSpec((tm, A_WIDTH), lambda i: (i, 0)),
            pl.BlockSpec((tm, B_WIDTH), lambda i: (i, 0)),
            pl.BlockSpec((tm, gw), lambda i: (i, ga_blk)),
            pl.BlockSpec((tm, gw), lambda i: (i, ga_blk + 1)),
            pl.BlockSpec((tm, gw), lambda i: (i, gb_blk)),
            pl.BlockSpec((tm, gw), lambda i: (i, gb_blk + 1)),
            pl.BlockSpec((tm, D_MODEL), lambda i: (i, 0)),
            _const_spec((A_WIDTH, D_MODEL)),
            _const_spec((B_WIDTH, D_MODEL)),
            _const_spec((D_MODEL, D_MODEL)),
            _const_spec((1, D_MODEL)),
            _const_spec((D_MODEL, 2 * ROUTER_COLS)),
            _const_spec((1, ROUTER_COLS)),
        ],
        out_specs=[
            pl.BlockSpec((tm, D_MODEL), lambda i: (i, 0)),
            pl.BlockSpec((tm, D_MODEL // 2), lambda i: (i, 0)),
            pl.BlockSpec((tm, ROUTER_COLS), lambda i: (i, 0)),
        ],
        out_shape=[
            jax.ShapeDtypeStruct((t_rows, D_MODEL), F32),
            jax.ShapeDtypeStruct((t_rows, D_MODEL // 2), jnp.uint32),
            jax.ShapeDtypeStruct((t_rows, ROUTER_COLS), F32),
        ],
        compiler_params=_params(("parallel",)),
        name="merge_route",
    )(oa, ob, proj2d, proj2d, proj2d, proj2d, x2d, wa, wb, wo, norm_ffn, wr, br)


SEG_NTILES = 2 * N_EXPERTS


def _positions_kernel(e1_ref, e2_ref, pos1_ref, pos2_ref, te_ref, seg_ref):
    e1 = e1_ref[...]
    e2 = e2_ref[...]
    rows = e1.shape[0]
    r_i = lax.broadcasted_iota(jnp.int32, (LANES, LANES), 0)
    c_i = lax.broadcasted_iota(jnp.int32, (LANES, LANES), 1)
    upper = (r_i < c_i).astype(BF16)
    rr = lax.broadcasted_iota(jnp.int32, (rows, rows), 0)
    rc = lax.broadcasted_iota(jnp.int32, (rows, rows), 1)
    lower = (rc < rr).astype(BF16)
    seg_lane = lax.broadcasted_iota(jnp.int32, seg_ref.shape, 1)
    tile_row = lax.broadcasted_iota(jnp.int32, te_ref.shape, 1).astype(F32) * MOE_TILE
    base = jnp.zeros((1, 1), F32)
    pos1 = jnp.zeros(e1.shape, F32)
    pos2 = jnp.zeros(e1.shape, F32)
    seg = jnp.zeros(seg_ref.shape, F32)
    tile_expert = jnp.zeros(te_ref.shape, F32)
    for e in range(N_EXPERTS):
        m1 = e1 == e
        m2 = e2 == e
        m = jnp.where(m1 | m2, 1.0, 0.0)
        lane_pre = jnp.dot(m.astype(BF16), upper, preferred_element_type=F32)
        row_tot = jnp.broadcast_to(jnp.sum(m, axis=-1, keepdims=True), m.shape)
        row_pre = jnp.dot(lower, row_tot.astype(BF16), preferred_element_type=F32)
        total = jnp.sum(row_tot[:, 0:1], axis=0, keepdims=True)
        p = base + row_pre + lane_pre
        pos1 = jnp.where(m1, p, pos1)
        pos2 = jnp.where(m2, p, pos2)
        padded = jnp.floor((total + (MOE_TILE - 1)) * (1.0 / MOE_TILE)) * MOE_TILE
        base = base + padded
        seg = seg + jnp.where(seg_lane == e, base, 0.0) + jnp.where(seg_lane == N_EXPERTS + e, total, 0.0)
        tile_expert = tile_expert + jnp.where(tile_row >= base, 1.0, 0.0)
    seg = seg + jnp.where(seg_lane == SEG_NTILES, base * (1.0 / MOE_TILE), 0.0)
    pos1_ref[...] = pos1.astype(jnp.int32)
    pos2_ref[...] = pos2.astype(jnp.int32)
    te_ref[...] = jnp.minimum(tile_expert, N_EXPERTS - 1).astype(jnp.int32)
    seg_ref[...] = seg.astype(jnp.int32)


def _positions(e1, e2, n_tiles):
    rows = e1.shape[0]
    ntp = -(-n_tiles // LANES) * LANES
    full = lambda shape: pl.BlockSpec(shape, lambda: (0,) * len(shape))
    return pl.pallas_call(
        _positions_kernel,
        in_specs=[full((rows, LANES)), full((rows, LANES))],
        out_specs=[full((rows, LANES)), full((rows, LANES)), full((1, ntp)), full((1, LANES))],
        out_shape=[
            jax.ShapeDtypeStruct((rows, LANES), jnp.int32),
            jax.ShapeDtypeStruct((rows, LANES), jnp.int32),
            jax.ShapeDtypeStruct((1, ntp), jnp.int32),
            jax.ShapeDtypeStruct((1, LANES), jnp.int32),
        ],
        compiler_params=pltpu.CompilerParams(vmem_limit_bytes=VMEM_LIMIT),
        name="positions",
    )(e1, e2)


def _dispatch_kernel(seg_ref, pos1_ref, pos2_ref, u_ref, xs_ref, z_ref, zsem, sem, *, tb):
    i = pl.program_id(0)

    def zero_copy(e):
        end = seg_ref[e]
        start = pl.multiple_of(end - MOE_TILE, MOE_TILE)
        return pltpu.make_async_copy(z_ref, xs_ref.at[pl.ds(start, MOE_TILE)], zsem)

    def nonempty(e):
        return seg_ref[e] > (seg_ref[e - 1] if e > 0 else 0)

    n_tiles = xs_ref.shape[0] // MOE_TILE

    def tail_copy(k):
        start = pl.multiple_of((seg_ref[SEG_NTILES] + k) * MOE_TILE, MOE_TILE)
        return pltpu.make_async_copy(z_ref, xs_ref.at[pl.ds(start, MOE_TILE)], zsem)

    def tail_exists(k):
        return seg_ref[SEG_NTILES] + k < n_tiles

    @pl.when(i == 0)
    def _():
        z_ref[...] = jnp.zeros(z_ref.shape, z_ref.dtype)
        for e in range(N_EXPERTS):
            @pl.when(nonempty(e))
            def _():
                zero_copy(e).start()

            @pl.when(tail_exists(e))
            def _():
                tail_copy(e).start()
        for e in range(N_EXPERTS):
            @pl.when(nonempty(e))
            def _():
                zero_copy(e).wait()

            @pl.when(tail_exists(e))
            def _():
                tail_copy(e).wait()

    def row_copy(t, pos_ref):
        return pltpu.make_async_copy(u_ref.at[pl.ds(t, 1)], xs_ref.at[pl.ds(pos_ref[t], 1)], sem)

    def issue(t, carry):
        row_copy(t, pos1_ref).start()
        row_copy(t, pos2_ref).start()
        return carry

    lax.fori_loop(0, tb, issue, 0)

    for _ in range(2):
        pltpu.make_async_copy(u_ref, xs_ref.at[pl.ds(0, tb)], sem).wait()


def _dispatch(seg, pos1, pos2, u2p, n_tiles, *, tb):
    t_rows, width = u2p.shape
    return pl.pallas_call(
        functools.partial(_dispatch_kernel, tb=tb),
        grid_spec=pltpu.PrefetchScalarGridSpec(
            num_scalar_prefetch=1,
            grid=(t_rows // tb,),
            in_specs=[
                pl.BlockSpec((tb,), lambda i, seg: (i,), memory_space=pltpu.SMEM),
                pl.BlockSpec((tb,), lambda i, seg: (i,), memory_space=pltpu.SMEM),
                pl.BlockSpec((tb, width), lambda i, seg: (i, 0)),
            ],
            out_specs=pl.BlockSpec(memory_space=pl.ANY),
            scratch_shapes=[
                pltpu.VMEM((MOE_TILE, width), u2p.dtype),
                pltpu.SemaphoreType.DMA(()),
                pltpu.SemaphoreType.DMA(()),
            ],
        ),
        out_shape=jax.ShapeDtypeStruct((n_tiles * MOE_TILE, width), u2p.dtype),
        compiler_params=_params(("arbitrary",)),
        name="dispatch",
    )(seg, pos1, pos2, u2p)


def _moe_kernel(te_ref, seg_ref, x_ref, wg_ref, wu_ref, wd_ref, y_ref):
    i = pl.program_id(0)

    @pl.when(i < seg_ref[SEG_NTILES])
    def _():
        xw = x_ref[...]
        half = D_MODEL // 2
        lo = pltpu.bitcast(xw << 16, F32).astype(BF16)
        hi = pltpu.bitcast(xw & jnp.uint32(0xFFFF0000), F32).astype(BF16)
        hg = (jnp.dot(lo, wg_ref[0, :half, :], preferred_element_type=F32)
              + jnp.dot(hi, wg_ref[0, half:, :], preferred_element_type=F32))
        hu = (jnp.dot(lo, wu_ref[0, :half, :], preferred_element_type=F32)
              + jnp.dot(hi, wu_ref[0, half:, :], preferred_element_type=F32))
        hid = hg * _sigmoid(hg) * hu
        y_ref[...] = jnp.dot(hid.astype(BF16), wd_ref[0], preferred_element_type=F32)

    @pl.when(i >= seg_ref[SEG_NTILES])
    def _():
        y_ref[...] = jnp.zeros(y_ref.shape, y_ref.dtype)


def _moe(tile_expert, seg, xs, wg, wu, wd):
    n_tiles = xs.shape[0] // MOE_TILE

    def row_map(i, te, seg):
        return (jnp.minimum(i, seg[SEG_NTILES] - 1), 0)

    def w_map(i, te, seg):
        return (te[jnp.minimum(i, seg[SEG_NTILES] - 1)], 0, 0)

    return pl.pallas_call(
        _moe_kernel,
        grid_spec=pltpu.PrefetchScalarGridSpec(
            num_scalar_prefetch=2,
            grid=(n_tiles,),
            in_specs=[
                pl.BlockSpec((MOE_TILE, D_MODEL // 2), row_map),
                pl.BlockSpec((1, D_MODEL, EXPERT_FF), w_map),
                pl.BlockSpec((1, D_MODEL, EXPERT_FF), w_map),
                pl.BlockSpec((1, EXPERT_FF, D_MODEL), w_map),
            ],
            out_specs=pl.BlockSpec((MOE_TILE, D_MODEL), lambda i, te, seg: (i, 0)),
        ),
        out_shape=jax.ShapeDtypeStruct((n_tiles * MOE_TILE, D_MODEL), F32),
        compiler_params=_params(("arbitrary",)),
        name="moe",
    )(tile_expert, seg, xs, wg, wu, wd)


def _final_kernel(pos1_ref, pos2_ref, h2_ref, info_ref, g_ref, ys_ref, o_ref, ybuf, sem, *, tm):
    def row_copy(t, k, pos_ref):
        return pltpu.make_async_copy(ys_ref.at[pl.ds(pos_ref[t], 1)], ybuf.at[k, pl.ds(t, 1)], sem)

    def issue(t, carry):
        row_copy(t, 0, pos1_ref).start()
        row_copy(t, 1, pos2_ref).start()
        return carry

    lax.fori_loop(0, tm, issue, 0)

    for k in range(2):
        pltpu.make_async_copy(ys_ref.at[pl.ds(0, tm)], ybuf.at[k], sem).wait()

    info = info_ref[...]
    h = h2_ref[...] + info[:, 2:3] * ybuf[0] + info[:, 3:4] * ybuf[1]
    ms = jnp.mean(h * h, axis=-1, keepdims=True)
    o_ref[...] = h * lax.rsqrt(ms + RMS_EPS) * g_ref[...]


def _final(pos1, pos2, h2, info, final_norm, ys, *, tm):
    t_rows = h2.shape[0]
    return pl.pallas_call(
        functools.partial(_final_kernel, tm=tm),
        grid=(t_rows // tm,),
        in_specs=[
            pl.BlockSpec((tm,), lambda i: (i,), memory_space=pltpu.SMEM),
            pl.BlockSpec((tm,), lambda i: (i,), memory_space=pltpu.SMEM),
            pl.BlockSpec((tm, D_MODEL), lambda i: (i, 0)),
            pl.BlockSpec((tm, ROUTER_COLS), lambda i: (i, 0)),
            pl.BlockSpec((1, D_MODEL), lambda i: (0, 0)),
            pl.BlockSpec(memory_space=pl.ANY),
        ],
        out_specs=pl.BlockSpec((tm, D_MODEL), lambda i: (i, 0)),
        out_shape=jax.ShapeDtypeStruct((t_rows, D_MODEL), F32),
        scratch_shapes=[pltpu.VMEM((2, tm, D_MODEL), F32), pltpu.SemaphoreType.DMA(())],
        compiler_params=_params(("arbitrary",)),
        name="final",
    )(pos1, pos2, h2, info, final_norm, ys)


def _rope_tables(row, col):
    n_freq = HEAD_DIM // 4
    inv_freq = ROPE_THETA ** (-jnp.arange(n_freq, dtype=F32) / n_freq)
    ra = row.astype(F32)[:, None] * inv_freq
    ca = col.astype(F32)[:, None] * inv_freq
    cos = jnp.concatenate([jnp.cos(ra), jnp.cos(ra), jnp.cos(ca), jnp.cos(ca)], axis=-1)
    sin = jnp.concatenate([-jnp.sin(ra), jnp.sin(ra), -jnp.sin(ca), jnp.sin(ca)], axis=-1)
    return cos, sin


def _split_bf16(w):
    hi = w.astype(BF16)
    lo = (w - hi.astype(F32)).astype(BF16)
    return hi, lo


def _tile(n, pref):
    t = min(n, pref)
    assert n % t == 0, (n, pref)
    return t


def _encode_group(x, shared):
    (norm_mix, w_in, col_scale, q_norm, k_norm, meta_kv, tab, tabm, sink_rows, wa, wb, wo, norm_ffn, wr, br,
     wg, wu, wd, final_norm) = shared
    b, s, _ = x.shape
    t_rows = b * s
    x2d = x.reshape(t_rows, D_MODEL)
    tok = jnp.arange(s)
    cos, sin = _rope_tables(tok // GRID_W, tok % GRID_W)
    proj2d = _in_proj(x2d, norm_mix, w_in, col_scale, cos, sin, q_norm, k_norm, tm=_tile(s, 1024))
    proj = proj2d.reshape(b, s, IN_COLS)
    ka_m, va_m, kb_m, vb_m = meta_kv
    oa = _global_attn(proj, ka_m, va_m, tq=_tile(s, 256), ck=_tile(s, 1024))
    nblk = s // Q_BLOCK
    ob = _window_attn(proj, kb_m, vb_m, tab, tabm, sink_rows, nq=4 if nblk % 4 == 0 else 1)
    h2, u2p, info = _merge_route(oa.reshape(t_rows, A_WIDTH), ob.reshape(t_rows, B_WIDTH), proj2d, x2d,
                                 wa, wb, wo, norm_ffn, wr, br, tm=_tile(t_rows, 256))
    e1 = info[:, 0].reshape(t_rows // LANES, LANES)
    e2 = info[:, 1].reshape(t_rows // LANES, LANES)
    n_tiles = 2 * t_rows // MOE_TILE + N_EXPERTS
    pos1, pos2, tile_expert, seg = _positions(e1, e2, n_tiles)
    pos1 = pos1.reshape(t_rows)
    pos2 = pos2.reshape(t_rows)
    seg = seg.reshape(LANES)
    xs = _dispatch(seg, pos1, pos2, u2p, n_tiles, tb=_tile(t_rows, 1024))
    ys = _moe(tile_expert.reshape(-1)[:n_tiles], seg, xs, wg, wu, wd)
    out = _final(pos1, pos2, h2, info, final_norm, ys, tm=_tile(t_rows, 256))
    return out.reshape(b, s, D_MODEL)


def kernel(x_prompt, x_sample, meta_tokens, rel_bias, final_norm, norm_mix, w_in, q_norm, k_norm, sink,
           w_branch_a, w_branch_b, w_out, norm_ffn, w_router_g, b_router_g, w_router_e, b_router_e,
           w_gate, w_up, w_down):
    assert norm_mix.shape[0] == 1, "single-layer encoder"
    w_in_b = w_in[0].astype(BF16)
    norm_mix2 = norm_mix[0].reshape(1, D_MODEL)
    q_norm2 = q_norm[0].reshape(1, HEAD_DIM)
    k_norm2 = k_norm[0].reshape(1, HEAD_DIM)
    col = jnp.arange(IN_COLS)
    is_qb = (col >= COL_QB * HEAD_DIM) & (col < COL_KB * HEAD_DIM)
    col_scale = jnp.where(is_qb, SCORE_SCALE, 1.0).astype(F32).reshape(1, IN_COLS)

    cos_m, sin_m = _rope_tables(jnp.full((N_META,), -1), jnp.arange(N_META))
    proj_m = _in_proj(meta_tokens, norm_mix2, w_in_b, col_scale, cos_m, sin_m, q_norm2, k_norm2, tm=N_META)

    def meta_heads(c0):
        blk = proj_m[:, c0 * HEAD_DIM:(c0 + A_KV_HEADS) * HEAD_DIM]
        return blk.reshape(N_META, A_KV_HEADS, HEAD_DIM).transpose(1, 0, 2)

    meta_kv = tuple(meta_heads(c) for c in (COL_KA, COL_VA, COL_KB, COL_VB))

    tab, tabm = _bias_tables(rel_bias)
    sink_rows = jnp.repeat(sink[0].astype(F32), Q_BLOCK).reshape(B_KV_HEADS, REP * Q_BLOCK, 1)

    wr_full = jnp.zeros((D_MODEL, ROUTER_COLS), F32)
    wr_full = wr_full.at[:, :N_GROUPS].set(w_router_g[0]).at[:, N_GROUPS:N_GROUPS + N_EXPERTS].set(w_router_e[0])
    wr_hi, wr_lo = _split_bf16(wr_full)
    wr = jnp.concatenate([wr_hi, wr_lo], axis=1)
    br = jnp.zeros((1, ROUTER_COLS), F32)
    br = br.at[0, :N_GROUPS].set(b_router_g[0]).at[0, N_GROUPS:N_GROUPS + N_EXPERTS].set(b_router_e[0])

    shared = (norm_mix2, w_in_b, col_scale, q_norm2, k_norm2, meta_kv, tab, tabm, sink_rows,
              w_branch_a[0].astype(BF16), w_branch_b[0].astype(BF16), w_out[0].astype(BF16),
              norm_ffn[0].reshape(1, D_MODEL), wr, br,
              w_gate[0].astype(BF16), w_up[0].astype(BF16), w_down[0].astype(BF16),
              final_norm.reshape(1, D_MODEL))
    return (_encode_group(x_prompt, shared), _encode_group(x_sample, shared))
```

```python
import functools
import math

import numpy as np
import jax
import jax.numpy as jnp
from jax import lax
from jax.experimental import pallas as pl
from jax.experimental.pallas import tpu as pltpu

F32 = jnp.float32
BF16 = jnp.bfloat16

D_MODEL = 2048
HEAD_DIM = 128
A_HEADS = 8
A_KV_HEADS = 2
B_HEADS = 8
B_KV_HEADS = 2
REP = A_HEADS // A_KV_HEADS
A_WIDTH = A_HEADS * HEAD_DIM
B_WIDTH = B_HEADS * HEAD_DIM
KV_WIDTH = A_KV_HEADS * HEAD_DIM
IN_COLS = A_WIDTH + 2 * KV_WIDTH + B_WIDTH + 2 * KV_WIDTH + 2 * D_MODEL
Q_BLOCK = 128
WINDOW = 128
N_META = 16
GRID_W = 64
ROPE_THETA = 10000.0
N_BUCKETS = 32
MAX_DISTANCE = 128
N_GROUPS = 4
EXPERTS_PER_GROUP = 4
N_EXPERTS = N_GROUPS * EXPERTS_PER_GROUP
EXPERT_FF = 1024
RMS_EPS = 1e-6
NEG_INF = -1e30
SCORE_SCALE = HEAD_DIM ** -0.5
LOG2E = math.log2(math.e)

COL_QA = 0
COL_KA = A_WIDTH // HEAD_DIM
COL_VA = COL_KA + A_KV_HEADS
COL_QB = COL_VA + A_KV_HEADS
COL_KB = COL_QB + B_HEADS
COL_VB = COL_KB + B_KV_HEADS
COL_GA = COL_VB + B_KV_HEADS
COL_GB = COL_GA + D_MODEL // HEAD_DIM

LANES = 128
VMEM_LIMIT = 56 * 1024 * 1024
MOE_TILE = 512
ROUTER_COLS = 128

_NT = (((1,), (1,)), ((), ()))


def _params(sem, vmem=VMEM_LIMIT):
    return pltpu.CompilerParams(dimension_semantics=sem, vmem_limit_bytes=vmem)


def _const_spec(shape):
    nd = len(shape)
    return pl.BlockSpec(shape, lambda *_: (0,) * nd, pipeline_mode=pl.Buffered(1))


def _swap_rotary_sections(w):
    q = HEAD_DIM // 4
    parts = []
    for h in range(w.shape[-1] // HEAD_DIM):
        b = h * HEAD_DIM
        parts += [w[..., b:b + q], w[..., b + 2 * q:b + 3 * q], w[..., b + q:b + 2 * q], w[..., b + 3 * q:b + 4 * q]]
    return jnp.concatenate(parts, axis=-1)


def _norm_rope(a, cg, sg):
    ones = jnp.ones((HEAD_DIM, HEAD_DIM), BF16)
    ssq = jnp.dot((a * a).astype(BF16), ones, preferred_element_type=F32)
    rinv = lax.rsqrt(ssq * (1.0 / HEAD_DIM) + RMS_EPS)
    return rinv * (a * cg + pltpu.roll(a, HEAD_DIM // 2, 1) * sg)


def _in_proj_kernel(x_ref, g_ref, w_ref, cs_ref, cos_ref, sin_ref, qn_ref, kn_ref, o_ref, u_ref):
    j = pl.program_id(1)

    @pl.when(j == 0)
    def _():
        x = x_ref[...]
        ms = jnp.mean(x * x, axis=-1, keepdims=True)
        u_ref[...] = (x * lax.rsqrt(ms + RMS_EPS) * g_ref[...]).astype(BF16)

    acc = jnp.dot(u_ref[...], w_ref[...], preferred_element_type=F32) * cs_ref[...]

    def rope_factors(gain_ref, scale):
        return cos_ref[...] * (gain_ref[0:1, :] * scale), sin_ref[...] * (gain_ref[1:2, :] * scale)

    @pl.when(j == 0)
    def _():
        cg, sg = rope_factors(qn_ref, SCORE_SCALE * LOG2E)
        for h in range(A_HEADS):
            sl = slice(h * HEAD_DIM, (h + 1) * HEAD_DIM)
            o_ref[:, sl] = _norm_rope(acc[:, sl], cg, sg).astype(BF16)

    @pl.when(j == 1)
    def _():
        cg, sg = rope_factors(kn_ref, 1.0)
        for h in range(A_KV_HEADS):
            sl = slice(h * HEAD_DIM, (h + 1) * HEAD_DIM)
            o_ref[:, sl] = _norm_rope(acc[:, sl], cg, sg).astype(BF16)
        o_ref[:, KV_WIDTH:] = acc[:, KV_WIDTH:].astype(BF16)

    @pl.when(j >= 2)
    def _():
        o_ref[...] = acc.astype(BF16)


def _in_proj(x2d, norm_g, w_bf16, col_scale, cos, sin, q_norm, k_norm, *, tm):
    t_rows = x2d.shape[0]
    tn = A_WIDTH
    pos_tiles = cos.shape[0] // tm
    return pl.pallas_call(
        _in_proj_kernel,
        grid=(t_rows // tm, IN_COLS // tn),
        in_specs=[
            pl.BlockSpec((tm, D_MODEL), lambda i, j: (i, 0)),
            pl.BlockSpec((1, D_MODEL), lambda i, j: (0, 0)),
            pl.BlockSpec((D_MODEL, tn), lambda i, j: (0, j)),
            pl.BlockSpec((1, tn), lambda i, j: (0, j)),
            pl.BlockSpec((tm, HEAD_DIM), lambda i, j: (i % pos_tiles, 0)),
            pl.BlockSpec((tm, HEAD_DIM), lambda i, j: (i % pos_tiles, 0)),
            pl.BlockSpec((2, HEAD_DIM), lambda i, j: (0, 0)),
            pl.BlockSpec((2, HEAD_DIM), lambda i, j: (0, 0)),
        ],
        out_specs=pl.BlockSpec((tm, tn), lambda i, j: (i, j)),
        out_shape=jax.ShapeDtypeStruct((t_rows, IN_COLS), BF16),
        scratch_shapes=[pltpu.VMEM((tm, D_MODEL), BF16)],
        compiler_params=_params(("parallel", "arbitrary")),
        name="in_proj",
    )(x2d, norm_g, w_bf16, col_scale, cos, sin, q_norm, k_norm)


def _stack_heads(q):
    return jnp.concatenate([q[:, r * HEAD_DIM:(r + 1) * HEAD_DIM] for r in range(REP)], axis=0)


def _global_attn_kernel(q_ref, k_ref, v_ref, km_ref, vm_ref, o_ref, vx_ref, vmx_ref, *, tq, ck):
    @pl.when(pl.program_id(2) == 0)
    def _():
        vx_ref[:, :HEAD_DIM] = v_ref[0]
        vx_ref[:, HEAD_DIM:] = jnp.ones((v_ref.shape[1], HEAD_DIM), BF16)
        vmx_ref[:, :HEAD_DIM] = vm_ref[0]
        vmx_ref[:, HEAD_DIM:] = jnp.ones((N_META, HEAD_DIM), BF16)

    qs = _stack_heads(q_ref[0])
    n_chunks = k_ref.shape[1] // ck

    def scores(c):
        return lax.dot_general(qs, k_ref[0, c * ck:(c + 1) * ck, :], _NT, preferred_element_type=F32)

    s = lax.dot_general(qs, km_ref[0], _NT, preferred_element_type=F32)
    s_next = scores(0)
    m = jnp.max(s, axis=-1, keepdims=True)
    acc = jnp.dot(jnp.exp2(s - m).astype(BF16), vmx_ref[...], preferred_element_type=F32)
    for c in range(n_chunks):
        s = s_next
        if c + 1 < n_chunks:
            s_next = scores(c + 1)
        m_new = jnp.maximum(m, jnp.max(s, axis=-1, keepdims=True))
        p = jnp.exp2(s - m_new).astype(BF16)
        acc = jnp.exp2(m - m_new) * acc + jnp.dot(p, vx_ref[c * ck:(c + 1) * ck, :],
                                                  preferred_element_type=F32)
        m = m_new
    o = acc[:, :HEAD_DIM] / acc[:, HEAD_DIM:]
    for r in range(REP):
        o_ref[0, :, r * HEAD_DIM:(r + 1) * HEAD_DIM] = o[r * tq:(r + 1) * tq].astype(BF16)


def _global_attn(proj, km, vm, *, tq, ck):
    b, s, _ = proj.shape
    gw = REP * HEAD_DIM
    return pl.pallas_call(
        functools.partial(_global_attn_kernel, tq=tq, ck=ck),
        grid=(b, A_KV_HEADS, s // tq),
        in_specs=[
            pl.BlockSpec((1, tq, gw), lambda bi, g, i: (bi, i, g)),
            pl.BlockSpec((1, s, HEAD_DIM), lambda bi, g, i: (bi, 0, COL_KA + g)),
            pl.BlockSpec((1, s, HEAD_DIM), lambda bi, g, i: (bi, 0, COL_VA + g)),
            pl.BlockSpec((1, N_META, HEAD_DIM), lambda bi, g, i: (g, 0, 0)),
            pl.BlockSpec((1, N_META, HEAD_DIM), lambda bi, g, i: (g, 0, 0)),
        ],
        out_specs=pl.BlockSpec((1, tq, gw), lambda bi, g, i: (bi, i, g)),
        out_shape=jax.ShapeDtypeStruct((b, s, A_WIDTH), BF16),
        scratch_shapes=[pltpu.VMEM((s, 2 * HEAD_DIM), BF16), pltpu.VMEM((N_META, 2 * HEAD_DIM), BF16)],
        compiler_params=_params(("parallel", "parallel", "arbitrary")),
        name="global_attn",
    )(proj, proj, proj, km, vm)


def _t5_bucket_np(rel):
    nb = N_BUCKETS // 2
    max_exact = nb // 2
    bucket = np.where(rel > 0, nb, 0)
    n = np.abs(rel)
    nf = np.maximum(n, 1).astype(np.float32)
    large = max_exact + (np.log(nf / np.float32(max_exact)) / np.float32(math.log(MAX_DISTANCE / max_exact))
                         * np.float32(nb - max_exact)).astype(np.int32)
    large = np.minimum(large, nb - 1)
    return (bucket + np.where(n < max_exact, n, large)).astype(np.int32)


def _bucket_maps():
    i = np.arange(Q_BLOCK)[:, None]
    j = np.arange(3 * Q_BLOCK)[None, :]
    real = []
    for off in range(3):
        rel = j - off * Q_BLOCK - i
        real.append(np.where(np.abs(rel) <= WINDOW, _t5_bucket_np(rel), -1))
    m = np.arange(N_META)[None, :]
    first = _t5_bucket_np(m - (N_META + i))
    later = _t5_bucket_np(m - (N_META + i + Q_BLOCK))
    meta = [first, later, later]
    return np.stack(real).astype(np.int32), np.stack(meta).astype(np.int32)


def _bias_table_kernel(rb_ref, bm_ref, bmm_ref, o_ref, om_ref):
    h = pl.program_id(1)
    bm = bm_ref[0]
    bmm = bmm_ref[0]
    acc = jnp.full(bm.shape, NEG_INF, F32)
    accm = jnp.full(bmm.shape, NEG_INF, F32)
    for k in range(N_BUCKETS):
        val = rb_ref[k, h] * LOG2E
        acc = jnp.where(bm == k, val, acc)
        accm = jnp.where(bmm == k, val, accm)
    o_ref[0, 0] = acc
    om_ref[0, 0] = accm


def _bias_tables(rel_bias):
    bm, bmm = _bucket_maps()
    kw = 3 * Q_BLOCK
    tab, tabm = pl.pallas_call(
        _bias_table_kernel,
        grid=(3, B_HEADS),
        in_specs=[
            pl.BlockSpec(memory_space=pltpu.SMEM),
            pl.BlockSpec((1, Q_BLOCK, kw), lambda v, h: (v, 0, 0)),
            pl.BlockSpec((1, Q_BLOCK, N_META), lambda v, h: (v, 0, 0)),
        ],
        out_specs=[
            pl.BlockSpec((1, 1, Q_BLOCK, kw), lambda v, h: (v, h, 0, 0)),
            pl.BlockSpec((1, 1, Q_BLOCK, N_META), lambda v, h: (v, h, 0, 0)),
        ],
        out_shape=[
            jax.ShapeDtypeStruct((3, B_HEADS, Q_BLOCK, kw), F32),
            jax.ShapeDtypeStruct((3, B_HEADS, Q_BLOCK, N_META), F32),
        ],
        compiler_params=_params(("arbitrary", "arbitrary")),
        name="bias_tables",
    )(rel_bias, jnp.asarray(bm), jnp.asarray(bmm))
    rows = REP * Q_BLOCK
    return (tab.reshape(3, B_KV_HEADS, rows, kw), tabm.reshape(3, B_KV_HEADS, rows, N_META))


def _window_attn_kernel(q_ref, k_ref, v_ref, km_ref, vm_ref, tab_ref, tabm_ref, sink_ref, o_ref, *, nq):
    jb = pl.program_id(2)
    nblk = k_ref.shape[1] // Q_BLOCK
    km = km_ref[0]
    vm = vm_ref[0]
    sink = sink_ref[0] * LOG2E
    for t in range(nq):
        n = jb * nq + t
        qs = _stack_heads(q_ref[0, t * Q_BLOCK:(t + 1) * Q_BLOCK, :])
        var = jnp.where(n == 0, 0, jnp.where(n == nblk - 1, 2, 1))
        start = pl.multiple_of(jnp.clip(n - 1, 0, nblk - 3) * Q_BLOCK, Q_BLOCK)
        kb = k_ref[0, pl.ds(start, 3 * Q_BLOCK), :]
        vb = v_ref[0, pl.ds(start, 3 * Q_BLOCK), :]
        s = lax.dot_general(qs, kb, _NT, preferred_element_type=F32) + tab_ref[var, 0]
        sm = lax.dot_general(qs, km, _NT, preferred_element_type=F32) + tabm_ref[var, 0]
        m = jnp.maximum(jnp.maximum(jnp.max(s, axis=-1, keepdims=True),
                                    jnp.max(sm, axis=-1, keepdims=True)), sink)
        p = jnp.exp2(s - m)
        pm = jnp.exp2(sm - m)
        l = jnp.sum(p, axis=-1, keepdims=True) + jnp.sum(pm, axis=-1, keepdims=True) + jnp.exp2(sink - m)
        o = (jnp.dot(p.astype(BF16), vb, preferred_element_type=F32)
             + jnp.dot(pm.astype(BF16), vm, preferred_element_type=F32)) / l
        for r in range(REP):
            o_ref[0, t * Q_BLOCK:(t + 1) * Q_BLOCK, r * HEAD_DIM:(r + 1) * HEAD_DIM] = (
                o[r * Q_BLOCK:(r + 1) * Q_BLOCK].astype(BF16))


def _window_attn(proj, km, vm, tab, tabm, sink_rows, *, nq):
    b, s, _ = proj.shape
    assert s // Q_BLOCK >= 3 and (s // Q_BLOCK) % nq == 0
    gw = REP * HEAD_DIM
    rows = REP * Q_BLOCK
    kw = 3 * Q_BLOCK
    tq = nq * Q_BLOCK
    return pl.pallas_call(
        functools.partial(_window_attn_kernel, nq=nq),
        grid=(b, B_KV_HEADS, s // tq),
        in_specs=[
            pl.BlockSpec((1, tq, gw), lambda bi, g, i: (bi, i, COL_QB // REP + g)),
            pl.BlockSpec((1, s, HEAD_DIM), lambda bi, g, i: (bi, 0, COL_KB + g)),
            pl.BlockSpec((1, s, HEAD_DIM), lambda bi, g, i: (bi, 0, COL_VB + g)),
            pl.BlockSpec((1, N_META, HEAD_DIM), lambda bi, g, i: (g, 0, 0)),
            pl.BlockSpec((1, N_META, HEAD_DIM), lambda bi, g, i: (g, 0, 0)),
            pl.BlockSpec((3, 1, rows, kw), lambda bi, g, i: (0, g, 0, 0)),
            pl.BlockSpec((3, 1, rows, N_META), lambda bi, g, i: (0, g, 0, 0)),
            pl.BlockSpec((1, rows, 1), lambda bi, g, i: (g, 0, 0)),
        ],
        out_specs=pl.BlockSpec((1, tq, gw), lambda bi, g, i: (bi, i, g)),
        out_shape=jax.ShapeDtypeStruct((b, s, B_WIDTH), BF16),
        compiler_params=_params(("parallel", "parallel", "arbitrary")),
        name="window_attn",
    )(proj, proj, proj, km, vm, tab, tabm, sink_rows)


def _sigmoid(x):
    return 1.0 / (1.0 + jnp.exp(-x))


def _route(logits):
    lane = lax.broadcasted_iota(jnp.int32, logits.shape, 1).astype(F32)
    ninf = jnp.float32(-jnp.inf)
    big = jnp.float32(ROUTER_COLS)
    is_g = lane < N_GROUPS
    lg = jnp.where(is_g, logits, ninf)
    mg = jnp.max(lg, axis=-1, keepdims=True)
    gidx = jnp.min(jnp.where(lg == mg, lane, big), axis=-1, keepdims=True)
    p_top = 1.0 / jnp.sum(jnp.where(is_g, jnp.exp(lg - mg), 0.0), axis=-1, keepdims=True)
    lo = N_GROUPS + EXPERTS_PER_GROUP * gidx
    sel = (lane >= lo) & (lane < lo + EXPERTS_PER_GROUP)
    le = jnp.where(sel, logits, ninf)
    v1 = jnp.max(le, axis=-1, keepdims=True)
    i1 = jnp.min(jnp.where(sel & (le == v1), lane, big), axis=-1, keepdims=True)
    rest = sel & (lane != i1)
    le2 = jnp.where(rest, logits, ninf)
    v2 = jnp.max(le2, axis=-1, keepdims=True)
    i2 = jnp.min(jnp.where(rest & (le2 == v2), lane, big), axis=-1, keepdims=True)
    t = jnp.exp(v2 - v1)
    w1 = p_top / (1.0 + t)
    w2 = p_top * t / (1.0 + t)
    return i1 - N_GROUPS, i2 - N_GROUPS, w1, w2


def _merge_kernel(oa_ref, ob_ref, ga0_ref, ga1_ref, gb0_ref, gb1_ref, x_ref, wa_ref, wb_ref, wo_ref, gn_ref,
                  wr_ref, br_ref, h2_ref, u2_ref, logits_ref):
    ya = jnp.dot(oa_ref[...], wa_ref[...], preferred_element_type=F32)
    yb = jnp.dot(ob_ref[...], wb_ref[...], preferred_element_type=F32)
    ga = jnp.concatenate([ga0_ref[...], ga1_ref[...]], axis=1).astype(F32)
    gb = jnp.concatenate([gb0_ref[...], gb1_ref[...]], axis=1).astype(F32)
    mixed = _sigmoid(ga) * ya + _sigmoid(gb) * yb
    h2 = x_ref[...] + jnp.dot(mixed.astype(BF16), wo_ref[...], preferred_element_type=F32)
    h2_ref[...] = h2
    ms = jnp.mean(h2 * h2, axis=-1, keepdims=True)
    u = h2 * lax.rsqrt(ms + RMS_EPS) * gn_ref[...]
    u_hi = u.astype(BF16)
    u_hi32 = u_hi.astype(F32)
    u_lo = (u - u_hi32).astype(BF16)
    lg = (jnp.dot(u_hi, wr_ref[...], preferred_element_type=F32)
          + jnp.dot(u_lo, wr_ref[...], preferred_element_type=F32))
    logits_ref[...] = lg[:, :ROUTER_COLS] + lg[:, ROUTER_COLS:] + br_ref[...]
    half = D_MODEL // 2
    lo_bits = pltpu.bitcast(u_hi32[:, :half], jnp.uint32) >> 16
    hi_bits = pltpu.bitcast(u_hi32[:, half:], jnp.uint32)
    u2_ref[...] = hi_bits | lo_bits


def _merge_route(oa, ob, proj2d, x2d, wa, wb, wo, norm_ffn, wr, br, *, tm):
    t_rows = x2d.shape[0]
    gw = D_MODEL // 2
    ga_blk = COL_GA * HEAD_DIM // gw
    gb_blk = COL_GB * HEAD_DIM // gw
    return pl.pallas_call(
        _merge_kernel,
        grid=(t_rows // tm,),
        in_specs=[
            pl.BlockSpec((tm, A_WIDTH), lambda i: (i, 0)),
            pl.BlockSpec((tm, B_WIDTH), lambda i: (i, 0)),
            pl.BlockSpec((tm, gw), lambda i: (i, ga_blk)),
            pl.BlockSpec((tm, gw), lambda i: (i, ga_blk + 1)),
            pl.BlockSpec((tm, gw), lambda i: (i, gb_blk)),
            pl.BlockSpec((tm, gw), lambda i: (i, gb_blk + 1)),
            pl.BlockSpec((tm, D_MODEL), lambda i: (i, 0)),
            _const_spec((A_WIDTH, D_MODEL)),
            _const_spec((B_WIDTH, D_MODEL)),
            _const_spec((D_MODEL, D_MODEL)),
            _const_spec((1, D_MODEL)),
            _const_spec((D_MODEL, 2 * ROUTER_COLS)),
            _const_spec((1, ROUTER_COLS)),
        ],
        out_specs=[
            pl.BlockSpec((tm, D_MODEL), lambda i: (i, 0)),
            pl.BlockSpec((tm, D_MODEL // 2), lambda i: (i, 0)),
            pl.BlockSpec((tm, ROUTER_COLS), lambda i: (i, 0)),
        ],
        out_shape=[
            jax.ShapeDtypeStruct((t_rows, D_MODEL), F32),
            jax.ShapeDtypeStruct((t_rows, D_MODEL // 2), jnp.uint32),
            jax.ShapeDtypeStruct((t_rows, ROUTER_COLS), F32),
        ],
        compiler_params=_params(("parallel",)),
        name="merge_route",
    )(oa, ob, proj2d, proj2d, proj2d, proj2d, x2d, wa, wb, wo, norm_ffn, wr, br)


def _route_kernel(logits_ref, info_ref):
    logits = logits_ref[...]
    e1, e2, w1, w2 = _route(logits)
    lane = lax.broadcasted_iota(jnp.int32, logits.shape, 1)
    info_ref[...] = jnp.where(lane == 0, e1, jnp.where(lane == 1, e2, jnp.where(lane == 2, w1,
                              jnp.where(lane == 3, w2, 0.0))))


def _route_call(logits, *, tm):
    t_rows = logits.shape[0]
    return pl.pallas_call(
        _route_kernel,
        grid=(t_rows // tm,),
        in_specs=[pl.BlockSpec((tm, ROUTER_COLS), lambda i: (i, 0))],
        out_specs=pl.BlockSpec((tm, ROUTER_COLS), lambda i: (i, 0)),
        out_shape=jax.ShapeDtypeStruct((t_rows, ROUTER_COLS), F32),
        compiler_params=_params(("parallel",)),
        name="route",
    )(logits)


SEG_NTILES = 2 * N_EXPERTS


def _positions_kernel(e1_ref, e2_ref, pos1_ref, pos2_ref, te_ref, seg_ref):
    e1 = e1_ref[...]
    e2 = e2_ref[...]
    rows = e1.shape[0]
    r_i = lax.broadcasted_iota(jnp.int32, (LANES, LANES), 0)
    c_i = lax.broadcasted_iota(jnp.int32, (LANES, LANES), 1)
    upper = (r_i < c_i).astype(BF16)
    rr = lax.broadcasted_iota(jnp.int32, (rows, rows), 0)
    rc = lax.broadcasted_iota(jnp.int32, (rows, rows), 1)
    lower = (rc < rr).astype(BF16)
    seg_lane = lax.broadcasted_iota(jnp.int32, seg_ref.shape, 1)
    tile_row = lax.broadcasted_iota(jnp.int32, te_ref.shape, 1).astype(F32) * MOE_TILE
    base = jnp.zeros((1, 1), F32)
    pos1 = jnp.zeros(e1.shape, F32)
    pos2 = jnp.zeros(e1.shape, F32)
    seg = jnp.zeros(seg_ref.shape, F32)
    tile_expert = jnp.zeros(te_ref.shape, F32)
    for e in range(N_EXPERTS):
        m1 = e1 == e
        m2 = e2 == e
        m = jnp.where(m1 | m2, 1.0, 0.0)
        lane_pre = jnp.dot(m.astype(BF16), upper, preferred_element_type=F32)
        row_tot = jnp.broadcast_to(jnp.sum(m, axis=-1, keepdims=True), m.shape)
        row_pre = jnp.dot(lower, row_tot.astype(BF16), preferred_element_type=F32)
        total = jnp.sum(row_tot[:, 0:1], axis=0, keepdims=True)
        p = base + row_pre + lane_pre
        pos1 = jnp.where(m1, p, pos1)
        pos2 = jnp.where(m2, p, pos2)
        padded = jnp.floor((total + (MOE_TILE - 1)) * (1.0 / MOE_TILE)) * MOE_TILE
        base = base + padded
        seg = seg + jnp.where(seg_lane == e, base, 0.0) + jnp.where(seg_lane == N_EXPERTS + e, total, 0.0)
        tile_expert = tile_expert + jnp.where(tile_row >= base, 1.0, 0.0)
    seg = seg + jnp.where(seg_lane == SEG_NTILES, base * (1.0 / MOE_TILE), 0.0)
    pos1_ref[...] = pos1.astype(jnp.int32)
    pos2_ref[...] = pos2.astype(jnp.int32)
    te_ref[...] = jnp.minimum(tile_expert, N_EXPERTS - 1).astype(jnp.int32)
    seg_ref[...] = seg.astype(jnp.int32)


def _positions(e1, e2, n_tiles):
    rows = e1.shape[0]
    ntp = -(-n_tiles // LANES) * LANES
    full = lambda shape: pl.BlockSpec(shape, lambda: (0,) * len(shape))
    return pl.pallas_call(
        _positions_kernel,
        in_specs=[full((rows, LANES)), full((rows, LANES))],
        out_specs=[full((rows, LANES)), full((rows, LANES)), full((1, ntp)), full((1, LANES))],
        out_shape=[
            jax.ShapeDtypeStruct((rows, LANES), jnp.int32),
            jax.ShapeDtypeStruct((rows, LANES), jnp.int32),
            jax.ShapeDtypeStruct((1, ntp), jnp.int32),
            jax.ShapeDtypeStruct((1, LANES), jnp.int32),
        ],
        compiler_params=pltpu.CompilerParams(vmem_limit_bytes=VMEM_LIMIT),
        name="positions",
    )(e1, e2)


def _dispatch_kernel(seg_ref, pos1_ref, pos2_ref, u_ref, xs_ref, z_ref, zsem, sem, *, tb):
    i = pl.program_id(0)

    def zero_copy(e):
        end = seg_ref[e]
        start = pl.multiple_of(end - MOE_TILE, MOE_TILE)
        return pltpu.make_async_copy(z_ref, xs_ref.at[pl.ds(start, MOE_TILE)], zsem)

    def nonempty(e):
        return seg_ref[e] > (seg_ref[e - 1] if e > 0 else 0)

    n_tiles = xs_ref.shape[0] // MOE_TILE

    def tail_copy(k):
        start = pl.multiple_of((seg_ref[SEG_NTILES] + k) * MOE_TILE, MOE_TILE)
        return pltpu.make_async_copy(z_ref, xs_ref.at[pl.ds(start, MOE_TILE)], zsem)

    def tail_exists(k):
        return seg_ref[SEG_NTILES] + k < n_tiles

    @pl.when(i == 0)
    def _():
        z_ref[...] = jnp.zeros(z_ref.shape, z_ref.dtype)
        for e in range(N_EXPERTS):
            @pl.when(nonempty(e))
            def _():
                zero_copy(e).start()

            @pl.when(tail_exists(e))
            def _():
                tail_copy(e).start()
        for e in range(N_EXPERTS):
            @pl.when(nonempty(e))
            def _():
                zero_copy(e).wait()

            @pl.when(tail_exists(e))
            def _():
                tail_copy(e).wait()

    def row_copy(t, pos_ref):
        return pltpu.make_async_copy(u_ref.at[pl.ds(t, 1)], xs_ref.at[pl.ds(pos_ref[t], 1)], sem)

    def issue(t, carry):
        row_copy(t, pos1_ref).start()
        row_copy(t, pos2_ref).start()
        return carry

    lax.fori_loop(0, tb, issue, 0)

    for _ in range(2):
        pltpu.make_async_copy(u_ref, xs_ref.at[pl.ds(0, tb)], sem).wait()


def _dispatch(seg, pos1, pos2, u2p, n_tiles, *, tb):
    t_rows, width = u2p.shape
    return pl.pallas_call(
        functools.partial(_dispatch_kernel, tb=tb),
        grid_spec=pltpu.PrefetchScalarGridSpec(
            num_scalar_prefetch=1,
            grid=(t_rows // tb,),
            in_specs=[
                pl.BlockSpec((tb,), lambda i, seg: (i,), memory_space=pltpu.SMEM),
                pl.BlockSpec((tb,), lambda i, seg: (i,), memory_space=pltpu.SMEM),
                pl.BlockSpec((tb, width), lambda i, seg: (i, 0)),
            ],
            out_specs=pl.BlockSpec(memory_space=pl.ANY),
            scratch_shapes=[
                pltpu.VMEM((MOE_TILE, width), u2p.dtype),
                pltpu.SemaphoreType.DMA(()),
                pltpu.SemaphoreType.DMA(()),
            ],
        ),
        out_shape=jax.ShapeDtypeStruct((n_tiles * MOE_TILE, width), u2p.dtype),
        compiler_params=_params(("arbitrary",)),
        name="dispatch",
    )(seg, pos1, pos2, u2p)


def _moe_kernel(te_ref, seg_ref, x_ref, wg_ref, wu_ref, wd_ref, y_ref):
    i = pl.program_id(0)

    @pl.when(i < seg_ref[SEG_NTILES])
    def _():
        xw = x_ref[...]
        half = D_MODEL // 2
        lo = pltpu.bitcast(xw << 16, F32).astype(BF16)
        hi = pltpu.bitcast(xw & jnp.uint32(0xFFFF0000), F32).astype(BF16)
        hg = (jnp.dot(lo, wg_ref[0, :half, :], preferred_element_type=F32)
              + jnp.dot(hi, wg_ref[0, half:, :], preferred_element_type=F32))
        hu = (jnp.dot(lo, wu_ref[0, :half, :], preferred_element_type=F32)
              + jnp.dot(hi, wu_ref[0, half:, :], preferred_element_type=F32))
        hid = hg * _sigmoid(hg) * hu
        y_ref[...] = jnp.dot(hid.astype(BF16), wd_ref[0], preferred_element_type=F32)

    @pl.when(i >= seg_ref[SEG_NTILES])
    def _():
        y_ref[...] = jnp.zeros(y_ref.shape, y_ref.dtype)


def _moe(tile_expert, seg, xs, wg, wu, wd):
    n_tiles = xs.shape[0] // MOE_TILE

    def row_map(i, te, seg):
        return (jnp.minimum(i, seg[SEG_NTILES] - 1), 0)

    def w_map(i, te, seg):
        return (te[jnp.minimum(i, seg[SEG_NTILES] - 1)], 0, 0)

    return pl.pallas_call(
        _moe_kernel,
        grid_spec=pltpu.PrefetchScalarGridSpec(
            num_scalar_prefetch=2,
            grid=(n_tiles,),
            in_specs=[
                pl.BlockSpec((MOE_TILE, D_MODEL // 2), row_map),
                pl.BlockSpec((1, D_MODEL, EXPERT_FF), w_map),
                pl.BlockSpec((1, D_MODEL, EXPERT_FF), w_map),
                pl.BlockSpec((1, EXPERT_FF, D_MODEL), w_map),
            ],
            out_specs=pl.BlockSpec((MOE_TILE, D_MODEL), lambda i, te, seg: (i, 0)),
        ),
        out_shape=jax.ShapeDtypeStruct((n_tiles * MOE_TILE, D_MODEL), F32),
        compiler_params=_params(("arbitrary",)),
        name="moe",
    )(tile_expert, seg, xs, wg, wu, wd)


def _final_kernel(pos1_ref, pos2_ref, nxt1_ref, nxt2_ref, h2_ref, info_ref, g_ref, ys_ref, o_ref, ybuf, sems, *, tm):
    i = pl.program_id(0)
    n = pl.num_programs(0)
    slot = i % 2

    def gather(p1_ref, p2_ref, s):
        def issue(t, carry):
            pltpu.make_async_copy(ys_ref.at[pl.ds(p1_ref[t], 1)], ybuf.at[s, 0, pl.ds(t, 1)], sems.at[s]).start()
            pltpu.make_async_copy(ys_ref.at[pl.ds(p2_ref[t], 1)], ybuf.at[s, 1, pl.ds(t, 1)], sems.at[s]).start()
            return carry

        lax.fori_loop(0, tm, issue, 0)

    @pl.when(i == 0)
    def _():
        gather(pos1_ref, pos2_ref, 0)

    @pl.when(i + 1 < n)
    def _():
        gather(nxt1_ref, nxt2_ref, 1 - slot)

    for k in range(2):
        pltpu.make_async_copy(ys_ref.at[pl.ds(0, tm)], ybuf.at[slot, k], sems.at[slot]).wait()

    info = info_ref[...]
    h = h2_ref[...] + info[:, 2:3] * ybuf[slot, 0] + info[:, 3:4] * ybuf[slot, 1]
    ms = jnp.mean(h * h, axis=-1, keepdims=True)
    o_ref[...] = h * lax.rsqrt(ms + RMS_EPS) * g_ref[...]


def _final(pos1, pos2, h2, info, final_norm, ys, *, tm):
    t_rows = h2.shape[0]
    n_steps = t_rows // tm
    nxt = lambda i: (jnp.minimum(i + 1, n_steps - 1),)
    return pl.pallas_call(
        functools.partial(_final_kernel, tm=tm),
        grid=(n_steps,),
        in_specs=[
            pl.BlockSpec((tm,), lambda i: (i,), memory_space=pltpu.SMEM),
            pl.BlockSpec((tm,), lambda i: (i,), memory_space=pltpu.SMEM),
            pl.BlockSpec((tm,), nxt, memory_space=pltpu.SMEM),
            pl.BlockSpec((tm,), nxt, memory_space=pltpu.SMEM),
            pl.BlockSpec((tm, D_MODEL), lambda i: (i, 0)),
            pl.BlockSpec((tm, ROUTER_COLS), lambda i: (i, 0)),
            pl.BlockSpec((1, D_MODEL), lambda i: (0, 0)),
            pl.BlockSpec(memory_space=pl.ANY),
        ],
        out_specs=pl.BlockSpec((tm, D_MODEL), lambda i: (i, 0)),
        out_shape=jax.ShapeDtypeStruct((t_rows, D_MODEL), F32),
        scratch_shapes=[pltpu.VMEM((2, 2, tm, D_MODEL), F32), pltpu.SemaphoreType.DMA((2,))],
        compiler_params=_params(("arbitrary",)),
        name="final",
    )(pos1, pos2, pos1, pos2, h2, info, final_norm, ys)


def _rope_tables(row, col):
    n_freq = HEAD_DIM // 4
    inv_freq = ROPE_THETA ** (-jnp.arange(n_freq, dtype=F32) / n_freq)
    ra = row.astype(F32)[:, None] * inv_freq
    ca = col.astype(F32)[:, None] * inv_freq
    cos = jnp.concatenate([jnp.cos(ra), jnp.cos(ca), jnp.cos(ra), jnp.cos(ca)], axis=-1)
    sin = jnp.concatenate([-jnp.sin(ra), -jnp.sin(ca), jnp.sin(ra), jnp.sin(ca)], axis=-1)
    return cos, sin


def _gain_rows(g):
    gs = _swap_rotary_sections(g.astype(F32))
    return jnp.stack([gs, jnp.roll(gs, HEAD_DIM // 2)])


def _split_bf16(w):
    hi = w.astype(BF16)
    lo = (w - hi.astype(F32)).astype(BF16)
    return hi, lo


def _tile(n, pref):
    t = min(n, pref)
    assert n % t == 0, (n, pref)
    return t


def _encode_group(x, shared):
    (norm_mix, w_in, col_scale, q_norm, k_norm, meta_kv, tab, tabm, sink_rows, wa, wb, wo, norm_ffn, wr, br,
     wg, wu, wd, final_norm) = shared
    b, s, _ = x.shape
    t_rows = b * s
    x2d = x.reshape(t_rows, D_MODEL)
    tok = jnp.arange(s)
    cos, sin = _rope_tables(tok // GRID_W, tok % GRID_W)
    proj2d = _in_proj(x2d, norm_mix, w_in, col_scale, cos, sin, q_norm, k_norm, tm=_tile(s, 1024))
    proj = proj2d.reshape(b, s, IN_COLS)
    ka_m, va_m, kb_m, vb_m = meta_kv
    oa = _global_attn(proj, ka_m, va_m, tq=_tile(s, 256), ck=_tile(s, 1024))
    nblk = s // Q_BLOCK
    ob = _window_attn(proj, kb_m, vb_m, tab, tabm, sink_rows, nq=4 if nblk % 4 == 0 else 1)
    h2, u2p, logits = _merge_route(oa.reshape(t_rows, A_WIDTH), ob.reshape(t_rows, B_WIDTH), proj2d, x2d,
                                   wa, wb, wo, norm_ffn, wr, br, tm=_tile(t_rows, 256))
    info = _route_call(logits, tm=_tile(t_rows, 2048))
    e1 = info[:, 0].reshape(t_rows // LANES, LANES)
    e2 = info[:, 1].reshape(t_rows // LANES, LANES)
    n_tiles = 2 * t_rows // MOE_TILE + N_EXPERTS
    pos1, pos2, tile_expert, seg = _positions(e1, e2, n_tiles)
    pos1 = pos1.reshape(t_rows)
    pos2 = pos2.reshape(t_rows)
    seg = seg.reshape(LANES)
    xs = _dispatch(seg, pos1, pos2, u2p, n_tiles, tb=_tile(t_rows, 1024))
    ys = _moe(tile_expert.reshape(-1)[:n_tiles], seg, xs, wg, wu, wd)
    out = _final(pos1, pos2, h2, info, final_norm, ys, tm=_tile(t_rows, 256))
    return out.reshape(b, s, D_MODEL)


def kernel(x_prompt, x_sample, meta_tokens, rel_bias, final_norm, norm_mix, w_in, q_norm, k_norm, sink,
           w_branch_a, w_branch_b, w_out, norm_ffn, w_router_g, b_router_g, w_router_e, b_router_e,
           w_gate, w_up, w_down):
    assert norm_mix.shape[0] == 1, "single-layer encoder"
    rot_cols = (A_HEADS + A_KV_HEADS) * HEAD_DIM
    w_in_b = jnp.concatenate([_swap_rotary_sections(w_in[0][:, :rot_cols]), w_in[0][:, rot_cols:]],
                             axis=1).astype(BF16)
    norm_mix2 = norm_mix[0].reshape(1, D_MODEL)
    q_norm2 = _gain_rows(q_norm[0])
    k_norm2 = _gain_rows(k_norm[0])
    col = jnp.arange(IN_COLS)
    is_qb = (col >= COL_QB * HEAD_DIM) & (col < COL_KB * HEAD_DIM)
    col_scale = jnp.where(is_qb, SCORE_SCALE * LOG2E, 1.0).astype(F32).reshape(1, IN_COLS)

    cos_m, sin_m = _rope_tables(jnp.full((N_META,), -1), jnp.arange(N_META))
    proj_m = _in_proj(meta_tokens, norm_mix2, w_in_b, col_scale, cos_m, sin_m, q_norm2, k_norm2, tm=N_META)

    def meta_heads(c0):
        blk = proj_m[:, c0 * HEAD_DIM:(c0 + A_KV_HEADS) * HEAD_DIM]
        return blk.reshape(N_META, A_KV_HEADS, HEAD_DIM).transpose(1, 0, 2)

    meta_kv = tuple(meta_heads(c) for c in (COL_KA, COL_VA, COL_KB, COL_VB))

    tab, tabm = _bias_tables(rel_bias)
    sink_rows = jnp.repeat(sink[0].astype(F32), Q_BLOCK).reshape(B_KV_HEADS, REP * Q_BLOCK, 1)

    wr_full = jnp.zeros((D_MODEL, ROUTER_COLS), F32)
    wr_full = wr_full.at[:, :N_GROUPS].set(w_router_g[0]).at[:, N_GROUPS:N_GROUPS + N_EXPERTS].set(w_router_e[0])
    wr_hi, wr_lo = _split_bf16(wr_full)
    wr = jnp.concatenate([wr_hi, wr_lo], axis=1)
    br = jnp.zeros((1, ROUTER_COLS), F32)
    br = br.at[0, :N_GROUPS].set(b_router_g[0]).at[0, N_GROUPS:N_GROUPS + N_EXPERTS].set(b_router_e[0])

    shared = (norm_mix2, w_in_b, col_scale, q_norm2, k_norm2, meta_kv, tab, tabm, sink_rows,
              w_branch_a[0].astype(BF16), w_branch_b[0].astype(BF16), w_out[0].astype(BF16),
              norm_ffn[0].reshape(1, D_MODEL), wr, br,
              w_gate[0].astype(BF16), w_up[0].astype(BF16), w_down[0].astype(BF16),
              final_norm.reshape(1, D_MODEL))
    return (_encode_group(x_prompt, shared), _encode_group(x_sample, shared))
```

```python
import functools
import math

import numpy as np
import jax
import jax.numpy as jnp
from jax import lax
from jax.experimental import pallas as pl
from jax.experimental.pallas import tpu as pltpu

F32 = jnp.float32
BF16 = jnp.bfloat16

D_MODEL = 2048
HEAD_DIM = 128
A_HEADS = 8
A_KV_HEADS = 2
B_HEADS = 8
B_KV_HEADS = 2
REP = A_HEADS // A_KV_HEADS
A_WIDTH = A_HEADS * HEAD_DIM
B_WIDTH = B_HEADS * HEAD_DIM
KV_WIDTH = A_KV_HEADS * HEAD_DIM
IN_COLS = A_WIDTH + 2 * KV_WIDTH + B_WIDTH + 2 * KV_WIDTH + 2 * D_MODEL
Q_BLOCK = 128
WINDOW = 128
N_META = 16
GRID_W = 64
ROPE_THETA = 10000.0
N_BUCKETS = 32
MAX_DISTANCE = 128
N_GROUPS = 4
EXPERTS_PER_GROUP = 4
N_EXPERTS = N_GROUPS * EXPERTS_PER_GROUP
EXPERT_FF = 1024
RMS_EPS = 1e-6
NEG_INF = -1e30
SCORE_SCALE = HEAD_DIM ** -0.5
LOG2E = math.log2(math.e)

COL_QA = 0
COL_KA = A_WIDTH // HEAD_DIM
COL_VA = COL_KA + A_KV_HEADS
COL_QB = COL_VA + A_KV_HEADS
COL_KB = COL_QB + B_HEADS
COL_VB = COL_KB + B_KV_HEADS
COL_GA = COL_VB + B_KV_HEADS
COL_GB = COL_GA + D_MODEL // HEAD_DIM

LANES = 128
VMEM_LIMIT = 56 * 1024 * 1024
MOE_TILE = 512
ROUTER_COLS = 128

_NT = (((1,), (1,)), ((), ()))


def _params(sem, vmem=VMEM_LIMIT):
    return pltpu.CompilerParams(dimension_semantics=sem, vmem_limit_bytes=vmem)


def _const_spec(shape):
    nd = len(shape)
    return pl.BlockSpec(shape, lambda *_: (0,) * nd, pipeline_mode=pl.Buffered(1))


def _swap_rotary_sections(w):
    lead = w.shape[:-1]
    heads = w.shape[-1] // HEAD_DIM
    nd = len(lead)
    w5 = w.reshape(lead + (heads, 2, 2, HEAD_DIM // 4))
    return jnp.swapaxes(w5, nd + 1, nd + 2).reshape(w.shape)


def _norm_rope(a, cg, sg):
    ones = jnp.ones((HEAD_DIM, HEAD_DIM), BF16)
    ssq = jnp.dot((a * a).astype(BF16), ones, preferred_element_type=F32)
    rinv = lax.rsqrt(ssq * (1.0 / HEAD_DIM) + RMS_EPS)
    return rinv * (a * cg + pltpu.roll(a, HEAD_DIM // 2, 1) * sg)


def _in_proj_kernel(x_ref, g_ref, w_ref, cs_ref, cos_ref, sin_ref, qn_ref, kn_ref, o_ref, u_ref):
    j = pl.program_id(1)

    @pl.when(j == 0)
    def _():
        x = x_ref[...]
        ms = jnp.mean(x * x, axis=-1, keepdims=True)
        u_ref[...] = (x * lax.rsqrt(ms + RMS_EPS) * g_ref[...]).astype(BF16)

    acc = jnp.dot(u_ref[...], w_ref[...], preferred_element_type=F32) * cs_ref[...]

    def rope_factors(gain_ref, scale):
        return cos_ref[...] * (gain_ref[0:1, :] * scale), sin_ref[...] * (gain_ref[1:2, :] * scale)

    @pl.when(j == 0)
    def _():
        cg, sg = rope_factors(qn_ref, SCORE_SCALE * LOG2E)
        for h in range(A_HEADS):
            sl = slice(h * HEAD_DIM, (h + 1) * HEAD_DIM)
            o_ref[:, sl] = _norm_rope(acc[:, sl], cg, sg).astype(BF16)

    @pl.when(j == 1)
    def _():
        cg, sg = rope_factors(kn_ref, 1.0)
        for h in range(A_KV_HEADS):
            sl = slice(h * HEAD_DIM, (h + 1) * HEAD_DIM)
            o_ref[:, sl] = _norm_rope(acc[:, sl], cg, sg).astype(BF16)
        o_ref[:, KV_WIDTH:] = acc[:, KV_WIDTH:].astype(BF16)

    @pl.when(j >= 2)
    def _():
        o_ref[...] = acc.astype(BF16)


def _in_proj(x2d, norm_g, w_bf16, col_scale, cos, sin, q_norm, k_norm, *, tm):
    t_rows = x2d.shape[0]
    tn = A_WIDTH
    pos_tiles = cos.shape[0] // tm
    return pl.pallas_call(
        _in_proj_kernel,
        grid=(t_rows // tm, IN_COLS // tn),
        in_specs=[
            pl.BlockSpec((tm, D_MODEL), lambda i, j: (i, 0)),
            pl.BlockSpec((1, D_MODEL), lambda i, j: (0, 0)),
            pl.BlockSpec((D_MODEL, tn), lambda i, j: (0, j)),
            pl.BlockSpec((1, tn), lambda i, j: (0, j)),
            pl.BlockSpec((tm, HEAD_DIM), lambda i, j: (i % pos_tiles, 0)),
            pl.BlockSpec((tm, HEAD_DIM), lambda i, j: (i % pos_tiles, 0)),
            pl.BlockSpec((2, HEAD_DIM), lambda i, j: (0, 0)),
            pl.BlockSpec((2, HEAD_DIM), lambda i, j: (0, 0)),
        ],
        out_specs=pl.BlockSpec((tm, tn), lambda i, j: (i, j)),
        out_shape=jax.ShapeDtypeStruct((t_rows, IN_COLS), BF16),
        scratch_shapes=[pltpu.VMEM((tm, D_MODEL), BF16)],
        compiler_params=_params(("parallel", "arbitrary")),
        name="in_proj",
    )(x2d, norm_g, w_bf16, col_scale, cos, sin, q_norm, k_norm)


def _stack_heads(q):
    return jnp.concatenate([q[:, r * HEAD_DIM:(r + 1) * HEAD_DIM] for r in range(REP)], axis=0)


def _global_attn_kernel(q_ref, k_ref, v_ref, km_ref, vm_ref, o_ref, vx_ref, vmx_ref, *, tq, ck):
    @pl.when(pl.program_id(2) == 0)
    def _():
        vx_ref[:, :HEAD_DIM] = v_ref[0]
        vx_ref[:, HEAD_DIM:] = jnp.ones((v_ref.shape[1], HEAD_DIM), BF16)
        vmx_ref[:, :HEAD_DIM] = vm_ref[0]
        vmx_ref[:, HEAD_DIM:] = jnp.ones((N_META, HEAD_DIM), BF16)

    qs = _stack_heads(q_ref[0])
    n_chunks = k_ref.shape[1] // ck

    def scores(c):
        return lax.dot_general(qs, k_ref[0, c * ck:(c + 1) * ck, :], _NT, preferred_element_type=F32)

    s = lax.dot_general(qs, km_ref[0], _NT, preferred_element_type=F32)
    s_next = scores(0)
    m = jnp.max(s, axis=-1, keepdims=True)
    acc = jnp.dot(jnp.exp2(s - m).astype(BF16), vmx_ref[...], preferred_element_type=F32)
    for c in range(n_chunks):
        s = s_next
        if c + 1 < n_chunks:
            s_next = scores(c + 1)
        m_new = jnp.maximum(m, jnp.max(s, axis=-1, keepdims=True))
        p = jnp.exp2(s - m_new).astype(BF16)
        acc = jnp.exp2(m - m_new) * acc + jnp.dot(p, vx_ref[c * ck:(c + 1) * ck, :],
                                                  preferred_element_type=F32)
        m = m_new
    o = acc[:, :HEAD_DIM] / acc[:, HEAD_DIM:]
    for r in range(REP):
        o_ref[0, :, r * HEAD_DIM:(r + 1) * HEAD_DIM] = o[r * tq:(r + 1) * tq].astype(BF16)


def _global_attn(proj, km, vm, *, tq, ck):
    b, s, _ = proj.shape
    gw = REP * HEAD_DIM
    return pl.pallas_call(
        functools.partial(_global_attn_kernel, tq=tq, ck=ck),
        grid=(b, A_KV_HEADS, s // tq),
        in_specs=[
            pl.BlockSpec((1, tq, gw), lambda bi, g, i: (bi, i, g)),
            pl.BlockSpec((1, s, HEAD_DIM), lambda bi, g, i: (bi, 0, COL_KA + g)),
            pl.BlockSpec((1, s, HEAD_DIM), lambda bi, g, i: (bi, 0, COL_VA + g)),
            pl.BlockSpec((1, N_META, HEAD_DIM), lambda bi, g, i: (g, 0, 0)),
            pl.BlockSpec((1, N_META, HEAD_DIM), lambda bi, g, i: (g, 0, 0)),
        ],
        out_specs=pl.BlockSpec((1, tq, gw), lambda bi, g, i: (bi, i, g)),
        out_shape=jax.ShapeDtypeStruct((b, s, A_WIDTH), BF16),
        scratch_shapes=[pltpu.VMEM((s, 2 * HEAD_DIM), BF16), pltpu.VMEM((N_META, 2 * HEAD_DIM), BF16)],
        compiler_params=_params(("parallel", "parallel", "arbitrary")),
        name="global_attn",
    )(proj, proj, proj, km, vm)


def _t5_bucket_np(rel):
    nb = N_BUCKETS // 2
    max_exact = nb // 2
    bucket = np.where(rel > 0, nb, 0)
    n = np.abs(rel)
    nf = np.maximum(n, 1).astype(np.float32)
    large = max_exact + (np.log(nf / np.float32(max_exact)) / np.float32(math.log(MAX_DISTANCE / max_exact))
                         * np.float32(nb - max_exact)).astype(np.int32)
    large = np.minimum(large, nb - 1)
    return (bucket + np.where(n < max_exact, n, large)).astype(np.int32)


def _bucket_maps():
    i = np.arange(Q_BLOCK)[:, None]
    j = np.arange(3 * Q_BLOCK)[None, :]
    real = []
    for off in range(3):
        rel = j - off * Q_BLOCK - i
        real.append(np.where(np.abs(rel) <= WINDOW, _t5_bucket_np(rel), -1))
    m = np.arange(N_META)[None, :]
    first = _t5_bucket_np(m - (N_META + i))
    later = _t5_bucket_np(m - (N_META + i + Q_BLOCK))
    meta = [first, later, later]
    return np.stack(real).astype(np.int32), np.stack(meta).astype(np.int32)


def _bias_table_kernel(rb_ref, bm_ref, bmm_ref, o_ref, om_ref):
    h = pl.program_id(1)
    bm = bm_ref[0]
    bmm = bmm_ref[0]
    acc = jnp.full(bm.shape, NEG_INF, F32)
    accm = jnp.full(bmm.shape, NEG_INF, F32)
    for k in range(N_BUCKETS):
        val = rb_ref[k, h] * LOG2E
        acc = jnp.where(bm == k, val, acc)
        accm = jnp.where(bmm == k, val, accm)
    o_ref[0, 0] = acc
    om_ref[0, 0] = accm


def _bias_tables(rel_bias):
    bm, bmm = _bucket_maps()
    kw = 3 * Q_BLOCK
    tab, tabm = pl.pallas_call(
        _bias_table_kernel,
        grid=(3, B_HEADS),
        in_specs=[
            pl.BlockSpec(memory_space=pltpu.SMEM),
            pl.BlockSpec((1, Q_BLOCK, kw), lambda v, h: (v, 0, 0)),
            pl.BlockSpec((1, Q_BLOCK, N_META), lambda v, h: (v, 0, 0)),
        ],
        out_specs=[
            pl.BlockSpec((1, 1, Q_BLOCK, kw), lambda v, h: (v, h, 0, 0)),
            pl.BlockSpec((1, 1, Q_BLOCK, N_META), lambda v, h: (v, h, 0, 0)),
        ],
        out_shape=[
            jax.ShapeDtypeStruct((3, B_HEADS, Q_BLOCK, kw), F32),
            jax.ShapeDtypeStruct((3, B_HEADS, Q_BLOCK, N_META), F32),
        ],
        compiler_params=_params(("arbitrary", "arbitrary")),
        name="bias_tables",
    )(rel_bias, jnp.asarray(bm), jnp.asarray(bmm))
    rows = REP * Q_BLOCK
    return (tab.reshape(3, B_KV_HEADS, rows, kw), tabm.reshape(3, B_KV_HEADS, rows, N_META))


def _window_attn_kernel(q_ref, k_ref, v_ref, km_ref, vm_ref, tab_ref, tabm_ref, sink_ref, o_ref, *, nq):
    jb = pl.program_id(2)
    nblk = k_ref.shape[1] // Q_BLOCK
    km = km_ref[0]
    vm = vm_ref[0]
    sink = sink_ref[0] * LOG2E
    for t in range(nq):
        n = jb * nq + t
        qs = _stack_heads(q_ref[0, t * Q_BLOCK:(t + 1) * Q_BLOCK, :])
        var = jnp.where(n == 0, 0, jnp.where(n == nblk - 1, 2, 1))
        start = pl.multiple_of(jnp.clip(n - 1, 0, nblk - 3) * Q_BLOCK, Q_BLOCK)
        kb = k_ref[0, pl.ds(start, 3 * Q_BLOCK), :]
        vb = v_ref[0, pl.ds(start, 3 * Q_BLOCK), :]
        s = lax.dot_general(qs, kb, _NT, preferred_element_type=F32) + tab_ref[var, 0]
        sm = lax.dot_general(qs, km, _NT, preferred_element_type=F32) + tabm_ref[var, 0]
        m = jnp.maximum(jnp.maximum(jnp.max(s, axis=-1, keepdims=True),
                                    jnp.max(sm, axis=-1, keepdims=True)), sink)
        p = jnp.exp2(s - m)
        pm = jnp.exp2(sm - m)
        l = jnp.sum(p, axis=-1, keepdims=True) + jnp.sum(pm, axis=-1, keepdims=True) + jnp.exp2(sink - m)
        o = (jnp.dot(p.astype(BF16), vb, preferred_element_type=F32)
             + jnp.dot(pm.astype(BF16), vm, preferred_element_type=F32)) / l
        for r in range(REP):
            o_ref[0, t * Q_BLOCK:(t + 1) * Q_BLOCK, r * HEAD_DIM:(r + 1) * HEAD_DIM] = (
                o[r * Q_BLOCK:(r + 1) * Q_BLOCK].astype(BF16))


def _window_attn(proj, km, vm, tab, tabm, sink_rows, *, nq):
    b, s, _ = proj.shape
    assert s // Q_BLOCK >= 3 and (s // Q_BLOCK) % nq == 0
    gw = REP * HEAD_DIM
    rows = REP * Q_BLOCK
    kw = 3 * Q_BLOCK
    tq = nq * Q_BLOCK
    return pl.pallas_call(
        functools.partial(_window_attn_kernel, nq=nq),
        grid=(b, B_KV_HEADS, s // tq),
        in_specs=[
            pl.BlockSpec((1, tq, gw), lambda bi, g, i: (bi, i, COL_QB // REP + g)),
            pl.BlockSpec((1, s, HEAD_DIM), lambda bi, g, i: (bi, 0, COL_KB + g)),
            pl.BlockSpec((1, s, HEAD_DIM), lambda bi, g, i: (bi, 0, COL_VB + g)),
            pl.BlockSpec((1, N_META, HEAD_DIM), lambda bi, g, i: (g, 0, 0)),
            pl.BlockSpec((1, N_META, HEAD_DIM), lambda bi, g, i: (g, 0, 0)),
            pl.BlockSpec((3, 1, rows, kw), lambda bi, g, i: (0, g, 0, 0)),
            pl.BlockSpec((3, 1, rows, N_META), lambda bi, g, i: (0, g, 0, 0)),
            pl.BlockSpec((1, rows, 1), lambda bi, g, i: (g, 0, 0)),
        ],
        out_specs=pl.BlockSpec((1, tq, gw), lambda bi, g, i: (bi, i, g)),
        out_shape=jax.ShapeDtypeStruct((b, s, B_WIDTH), BF16),
        compiler_params=_params(("parallel", "parallel", "arbitrary")),
        name="window_attn",
    )(proj, proj, proj, km, vm, tab, tabm, sink_rows)


def _sigmoid(x):
    return 1.0 / (1.0 + jnp.exp(-x))


def _route(logits):
    lane = lax.broadcasted_iota(jnp.int32, logits.shape, 1).astype(F32)
    ninf = jnp.float32(-jnp.inf)
    big = jnp.float32(ROUTER_COLS)
    is_g = lane < N_GROUPS
    lg = jnp.where(is_g, logits, ninf)
    mg = jnp.max(lg, axis=-1, keepdims=True)
    gidx = jnp.min(jnp.where(lg == mg, lane, big), axis=-1, keepdims=True)
    p_top = 1.0 / jnp.sum(jnp.where(is_g, jnp.exp(lg - mg), 0.0), axis=-1, keepdims=True)
    lo = N_GROUPS + EXPERTS_PER_GROUP * gidx
    sel = (lane >= lo) & (lane < lo + EXPERTS_PER_GROUP)
    le = jnp.where(sel, logits, ninf)
    v1 = jnp.max(le, axis=-1, keepdims=True)
    i1 = jnp.min(jnp.where(sel & (le == v1), lane, big), axis=-1, keepdims=True)
    rest = sel & (lane != i1)
    le2 = jnp.where(rest, logits, ninf)
    v2 = jnp.max(le2, axis=-1, keepdims=True)
    i2 = jnp.min(jnp.where(rest & (le2 == v2), lane, big), axis=-1, keepdims=True)
    t = jnp.exp(v2 - v1)
    w1 = p_top / (1.0 + t)
    w2 = p_top * t / (1.0 + t)
    return i1 - N_GROUPS, i2 - N_GROUPS, w1, w2


def _merge_kernel(oa_ref, ob_ref, ga0_ref, ga1_ref, gb0_ref, gb1_ref, x_ref, wa_ref, wb_ref, wo_ref, gn_ref,
                  wr_ref, br_ref, h2_ref, u2_ref, logits_ref, *, sub):
    for st in range(oa_ref.shape[0] // sub):
        r = slice(st * sub, (st + 1) * sub)
        ya = jnp.dot(oa_ref[r, :], wa_ref[...], preferred_element_type=F32)
        yb = jnp.dot(ob_ref[r, :], wb_ref[...], preferred_element_type=F32)
        ga = jnp.concatenate([ga0_ref[r, :], ga1_ref[r, :]], axis=1).astype(F32)
        gb = jnp.concatenate([gb0_ref[r, :], gb1_ref[r, :]], axis=1).astype(F32)
        mixed = _sigmoid(ga) * ya + _sigmoid(gb) * yb
        h2 = x_ref[r, :] + jnp.dot(mixed.astype(BF16), wo_ref[...], preferred_element_type=F32)
        h2_ref[r, :] = h2
        ms = jnp.mean(h2 * h2, axis=-1, keepdims=True)
        u = h2 * lax.rsqrt(ms + RMS_EPS) * gn_ref[...]
        u_hi = u.astype(BF16)
        u_hi32 = u_hi.astype(F32)
        u_lo = (u - u_hi32).astype(BF16)
        lg = (jnp.dot(u_hi, wr_ref[...], preferred_element_type=F32)
              + jnp.dot(u_lo, wr_ref[...], preferred_element_type=F32))
        logits_ref[r, :] = lg[:, :ROUTER_COLS] + lg[:, ROUTER_COLS:] + br_ref[...]
        half = D_MODEL // 2
        lo_bits = pltpu.bitcast(u_hi32[:, :half], jnp.uint32) >> 16
        hi_bits = pltpu.bitcast(u_hi32[:, half:], jnp.uint32)
        u2_ref[r, :] = hi_bits | lo_bits


def _merge_route(oa, ob, proj2d, x2d, wa, wb, wo, norm_ffn, wr, br, *, tm):
    t_rows = x2d.shape[0]
    gw = D_MODEL // 2
    ga_blk = COL_GA * HEAD_DIM // gw
    gb_blk = COL_GB * HEAD_DIM // gw
    return pl.pallas_call(
        functools.partial(_merge_kernel, sub=min(tm, 256)),
        grid=(t_rows // tm,),
        in_specs=[
            pl.BlockSpec((tm, A_WIDTH), lambda i: (i, 0)),
            pl.BlockSpec((tm, B_WIDTH), lambda i: (i, 0)),
            pl.BlockSpec((tm, gw), lambda i: (i, ga_blk)),
            pl.BlockSpec((tm, gw), lambda i: (i, ga_blk + 1)),
            pl.BlockSpec((tm, gw), lambda i: (i, gb_blk)),
            pl.BlockSpec((tm, gw), lambda i: (i, gb_blk + 1)),
            pl.BlockSpec((tm, D_MODEL), lambda i: (i, 0)),
            _const_spec((A_WIDTH, D_MODEL)),
            _const_spec((B_WIDTH, D_MODEL)),
            _const_spec((D_MODEL, D_MODEL)),
            _const_spec((1, D_MODEL)),
            _const_spec((D_MODEL, 2 * ROUTER_COLS)),
            _const_spec((1, ROUTER_COLS)),
        ],
        out_specs=[
            pl.BlockSpec((tm, D_MODEL), lambda i: (i, 0)),
            pl.BlockSpec((tm, D_MODEL // 2), lambda i: (i, 0)),
            pl.BlockSpec((tm, ROUTER_COLS), lambda i: (i, 0)),
        ],
        out_shape=[
            jax.ShapeDtypeStruct((t_rows, D_MODEL), F32),
            jax.ShapeDtypeStruct((t_rows, D_MODEL // 2), jnp.uint32),
            jax.ShapeDtypeStruct((t_rows, ROUTER_COLS), F32),
        ],
        compiler_params=_params(("parallel",)),
        name="merge_route",
    )(oa, ob, proj2d, proj2d, proj2d, proj2d, x2d, wa, wb, wo, norm_ffn, wr, br)


def _route_kernel(logits_ref, info_ref):
    logits = logits_ref[...]
    e1, e2, w1, w2 = _route(logits)
    lane = lax.broadcasted_iota(jnp.int32, logits.shape, 1)
    info_ref[...] = jnp.where(lane == 0, e1, jnp.where(lane == 1, e2, jnp.where(lane == 2, w1,
                              jnp.where(lane == 3, w2, 0.0))))


def _route_call(logits, *, tm):
    t_rows = logits.shape[0]
    return pl.pallas_call(
        _route_kernel,
        grid=(t_rows // tm,),
        in_specs=[pl.BlockSpec((tm, ROUTER_COLS), lambda i: (i, 0))],
        out_specs=pl.BlockSpec((tm, ROUTER_COLS), lambda i: (i, 0)),
        out_shape=jax.ShapeDtypeStruct((t_rows, ROUTER_COLS), F32),
        compiler_params=_params(("parallel",)),
        name="route",
    )(logits)


SEG_NTILES = 2 * N_EXPERTS


def _positions_kernel(e1_ref, e2_ref, pos1_ref, pos2_ref, te_ref, seg_ref):
    e1 = e1_ref[...]
    e2 = e2_ref[...]
    rows = e1.shape[0]
    r_i = lax.broadcasted_iota(jnp.int32, (LANES, LANES), 0)
    c_i = lax.broadcasted_iota(jnp.int32, (LANES, LANES), 1)
    upper = (r_i < c_i).astype(BF16)
    rr = lax.broadcasted_iota(jnp.int32, (rows, rows), 0)
    rc = lax.broadcasted_iota(jnp.int32, (rows, rows), 1)
    lower = (rc < rr).astype(BF16)
    seg_lane = lax.broadcasted_iota(jnp.int32, seg_ref.shape, 1)
    tile_row = lax.broadcasted_iota(jnp.int32, te_ref.shape, 1).astype(F32) * MOE_TILE
    base = jnp.zeros((1, 1), F32)
    pos1 = jnp.zeros(e1.shape, F32)
    pos2 = jnp.zeros(e1.shape, F32)
    seg = jnp.zeros(seg_ref.shape, F32)
    tile_expert = jnp.zeros(te_ref.shape, F32)
    for e in range(N_EXPERTS):
        m1 = e1 == e
        m2 = e2 == e
        m = jnp.where(m1 | m2, 1.0, 0.0)
        lane_pre = jnp.dot(m.astype(BF16), upper, preferred_element_type=F32)
        row_tot = jnp.broadcast_to(jnp.sum(m, axis=-1, keepdims=True), m.shape)
        row_pre = jnp.dot(lower, row_tot.astype(BF16), preferred_element_type=F32)
        total = jnp.sum(row_tot[:, 0:1], axis=0, keepdims=True)
        p = base + row_pre + lane_pre
        pos1 = jnp.where(m1, p, pos1)
        pos2 = jnp.where(m2, p, pos2)
        padded = jnp.floor((total + (MOE_TILE - 1)) * (1.0 / MOE_TILE)) * MOE_TILE
        base = base + padded
        seg = seg + jnp.where(seg_lane == e, base, 0.0) + jnp.where(seg_lane == N_EXPERTS + e, total, 0.0)
        tile_expert = tile_expert + jnp.where(tile_row >= base, 1.0, 0.0)
    seg = seg + jnp.where(seg_lane == SEG_NTILES, base * (1.0 / MOE_TILE), 0.0)
    pos1_ref[...] = pos1.astype(jnp.int32)
    pos2_ref[...] = pos2.astype(jnp.int32)
    te_ref[...] = jnp.minimum(tile_expert, N_EXPERTS - 1).astype(jnp.int32)
    seg_ref[...] = seg.astype(jnp.int32)


def _positions(e1, e2, n_tiles):
    rows = e1.shape[0]
    ntp = -(-n_tiles // LANES) * LANES
    full = lambda shape: pl.BlockSpec(shape, lambda: (0,) * len(shape))
    return pl.pallas_call(
        _positions_kernel,
        in_specs=[full((rows, LANES)), full((rows, LANES))],
        out_specs=[full((rows, LANES)), full((rows, LANES)), full((1, ntp)), full((1, LANES))],
        out_shape=[
            jax.ShapeDtypeStruct((rows, LANES), jnp.int32),
            jax.ShapeDtypeStruct((rows, LANES), jnp.int32),
            jax.ShapeDtypeStruct((1, ntp), jnp.int32),
            jax.ShapeDtypeStruct((1, LANES), jnp.int32),
        ],
        compiler_params=pltpu.CompilerParams(vmem_limit_bytes=VMEM_LIMIT),
        name="positions",
    )(e1, e2)


def _dispatch_kernel(seg_ref, pos1_ref, pos2_ref, u_ref, xs_ref, z_ref, zsem, sem, *, tb):
    i = pl.program_id(0)

    def zero_copy(e):
        end = seg_ref[e]
        start = pl.multiple_of(end - MOE_TILE, MOE_TILE)
        return pltpu.make_async_copy(z_ref, xs_ref.at[pl.ds(start, MOE_TILE)], zsem)

    def nonempty(e):
        return seg_ref[e] > (seg_ref[e - 1] if e > 0 else 0)

    n_tiles = xs_ref.shape[0] // MOE_TILE

    def tail_copy(k):
        start = pl.multiple_of((seg_ref[SEG_NTILES] + k) * MOE_TILE, MOE_TILE)
        return pltpu.make_async_copy(z_ref, xs_ref.at[pl.ds(start, MOE_TILE)], zsem)

    def tail_exists(k):
        return seg_ref[SEG_NTILES] + k < n_tiles

    @pl.when(i == 0)
    def _():
        z_ref[...] = jnp.zeros(z_ref.shape, z_ref.dtype)
        for e in range(N_EXPERTS):
            @pl.when(nonempty(e))
            def _():
                zero_copy(e).start()

            @pl.when(tail_exists(e))
            def _():
                tail_copy(e).start()
        for e in range(N_EXPERTS):
            @pl.when(nonempty(e))
            def _():
                zero_copy(e).wait()

            @pl.when(tail_exists(e))
            def _():
                tail_copy(e).wait()

    def row_copy(t, pos_ref):
        return pltpu.make_async_copy(u_ref.at[pl.ds(t, 1)], xs_ref.at[pl.ds(pos_ref[t], 1)], sem)

    def issue(t, carry):
        row_copy(t, pos1_ref).start()
        row_copy(t, pos2_ref).start()
        return carry

    lax.fori_loop(0, tb, issue, 0, unroll=8)

    for _ in range(2):
        pltpu.make_async_copy(u_ref, xs_ref.at[pl.ds(0, tb)], sem).wait()


def _dispatch(seg, pos1, pos2, u2p, n_tiles, *, tb):
    t_rows, width = u2p.shape
    return pl.pallas_call(
        functools.partial(_dispatch_kernel, tb=tb),
        grid_spec=pltpu.PrefetchScalarGridSpec(
            num_scalar_prefetch=1,
            grid=(t_rows // tb,),
            in_specs=[
                pl.BlockSpec((tb,), lambda i, seg: (i,), memory_space=pltpu.SMEM),
                pl.BlockSpec((tb,), lambda i, seg: (i,), memory_space=pltpu.SMEM),
                pl.BlockSpec((tb, width), lambda i, seg: (i, 0)),
            ],
            out_specs=pl.BlockSpec(memory_space=pl.ANY),
            scratch_shapes=[
                pltpu.VMEM((MOE_TILE, width), u2p.dtype),
                pltpu.SemaphoreType.DMA(()),
                pltpu.SemaphoreType.DMA(()),
            ],
        ),
        out_shape=jax.ShapeDtypeStruct((n_tiles * MOE_TILE, width), u2p.dtype),
        compiler_params=_params(("arbitrary",)),
        name="dispatch",
    )(seg, pos1, pos2, u2p)


def _moe_kernel(te_ref, seg_ref, x_ref, wg_ref, wu_ref, wd_ref, y_ref):
    i = pl.program_id(0)

    @pl.when(i < seg_ref[SEG_NTILES])
    def _():
        xw = x_ref[...]
        half = D_MODEL // 2
        lo = pltpu.bitcast(xw << 16, F32).astype(BF16)
        hi = pltpu.bitcast(xw & jnp.uint32(0xFFFF0000), F32).astype(BF16)
        hg = (jnp.dot(lo, wg_ref[0, :half, :], preferred_element_type=F32)
              + jnp.dot(hi, wg_ref[0, half:, :], preferred_element_type=F32))
        hu = (jnp.dot(lo, wu_ref[0, :half, :], preferred_element_type=F32)
              + jnp.dot(hi, wu_ref[0, half:, :], preferred_element_type=F32))
        hid = hg * _sigmoid(hg) * hu
        y_ref[...] = jnp.dot(hid.astype(BF16), wd_ref[0], preferred_element_type=F32)

    @pl.when(i >= seg_ref[SEG_NTILES])
    def _():
        y_ref[...] = jnp.zeros(y_ref.shape, y_ref.dtype)


def _moe(tile_expert, seg, xs, wg, wu, wd):
    n_tiles = xs.shape[0] // MOE_TILE

    def row_map(i, te, seg):
        return (jnp.minimum(i, seg[SEG_NTILES] - 1), 0)

    def w_map(i, te, seg):
        return (te[jnp.minimum(i, seg[SEG_NTILES] - 1)], 0, 0)

    return pl.pallas_call(
        _moe_kernel,
        grid_spec=pltpu.PrefetchScalarGridSpec(
            num_scalar_prefetch=2,
            grid=(n_tiles,),
            in_specs=[
                pl.BlockSpec((MOE_TILE, D_MODEL // 2), row_map),
                pl.BlockSpec((1, D_MODEL, EXPERT_FF), w_map),
                pl.BlockSpec((1, D_MODEL, EXPERT_FF), w_map),
                pl.BlockSpec((1, EXPERT_FF, D_MODEL), w_map),
            ],
            out_specs=pl.BlockSpec((MOE_TILE, D_MODEL), lambda i, te, seg: (i, 0)),
        ),
        out_shape=jax.ShapeDtypeStruct((n_tiles * MOE_TILE, D_MODEL), F32),
        compiler_params=_params(("arbitrary",)),
        name="moe",
    )(tile_expert, seg, xs, wg, wu, wd)


def _final_kernel(pos1_ref, pos2_ref, nxt1_ref, nxt2_ref, h2_ref, info_ref, g_ref, ys_ref, o_ref, ybuf, sems, *, tm):
    i = pl.program_id(0)
    n = pl.num_programs(0)
    slot = i % 2

    def gather(p1_ref, p2_ref, s):
        def issue(t, carry):
            pltpu.make_async_copy(ys_ref.at[pl.ds(p1_ref[t], 1)], ybuf.at[s, 0, pl.ds(t, 1)], sems.at[s]).start()
            pltpu.make_async_copy(ys_ref.at[pl.ds(p2_ref[t], 1)], ybuf.at[s, 1, pl.ds(t, 1)], sems.at[s]).start()
            return carry

        lax.fori_loop(0, tm, issue, 0, unroll=8)

    @pl.when(i == 0)
    def _():
        gather(pos1_ref, pos2_ref, 0)

    @pl.when(i + 1 < n)
    def _():
        gather(nxt1_ref, nxt2_ref, 1 - slot)

    for k in range(2):
        pltpu.make_async_copy(ys_ref.at[pl.ds(0, tm)], ybuf.at[slot, k], sems.at[slot]).wait()

    info = info_ref[...]
    h = h2_ref[...] + info[:, 2:3] * ybuf[slot, 0] + info[:, 3:4] * ybuf[slot, 1]
    ms = jnp.mean(h * h, axis=-1, keepdims=True)
    o_ref[...] = h * lax.rsqrt(ms + RMS_EPS) * g_ref[...]


def _final(pos1, pos2, h2, info, final_norm, ys, *, tm):
    t_rows = h2.shape[0]
    n_steps = t_rows // tm
    nxt = lambda i: (jnp.minimum(i + 1, n_steps - 1),)
    return pl.pallas_call(
        functools.partial(_final_kernel, tm=tm),
        grid=(n_steps,),
        in_specs=[
            pl.BlockSpec((tm,), lambda i: (i,), memory_space=pltpu.SMEM),
            pl.BlockSpec((tm,), lambda i: (i,), memory_space=pltpu.SMEM),
            pl.BlockSpec((tm,), nxt, memory_space=pltpu.SMEM),
            pl.BlockSpec((tm,), nxt, memory_space=pltpu.SMEM),
            pl.BlockSpec((tm, D_MODEL), lambda i: (i, 0)),
            pl.BlockSpec((tm, ROUTER_COLS), lambda i: (i, 0)),
            pl.BlockSpec((1, D_MODEL), lambda i: (0, 0)),
            pl.BlockSpec(memory_space=pl.ANY),
        ],
        out_specs=pl.BlockSpec((tm, D_MODEL), lambda i: (i, 0)),
        out_shape=jax.ShapeDtypeStruct((t_rows, D_MODEL), F32),
        scratch_shapes=[pltpu.VMEM((2, 2, tm, D_MODEL), F32), pltpu.SemaphoreType.DMA((2,))],
        compiler_params=_params(("arbitrary",)),
        name="final",
    )(pos1, pos2, pos1, pos2, h2, info, final_norm, ys)


def _rope_tables(row, col):
    n_freq = HEAD_DIM // 4
    inv_freq = ROPE_THETA ** (-jnp.arange(n_freq, dtype=F32) / n_freq)
    ra = row.astype(F32)[:, None] * inv_freq
    ca = col.astype(F32)[:, None] * inv_freq
    cos = jnp.concatenate([jnp.cos(ra), jnp.cos(ca), jnp.cos(ra), jnp.cos(ca)], axis=-1)
    sin = jnp.concatenate([-jnp.sin(ra), -jnp.sin(ca), jnp.sin(ra), jnp.sin(ca)], axis=-1)
    return cos, sin


def _gain_rows(g):
    gs = _swap_rotary_sections(g.astype(F32))
    return jnp.stack([gs, jnp.roll(gs, HEAD_DIM // 2)])


def _split_bf16(w):
    hi = w.astype(BF16)
    lo = (w - hi.astype(F32)).astype(BF16)
    return hi, lo


def _tile(n, pref):
    t = min(n, pref)
    assert n % t == 0, (n, pref)
    return t


def _encode_group(x, shared):
    (norm_mix, w_in, col_scale, q_norm, k_norm, meta_kv, tab, tabm, sink_rows, wa, wb, wo, norm_ffn, wr, br,
     wg, wu, wd, final_norm) = shared
    b, s, _ = x.shape
    t_rows = b * s
    x2d = x.reshape(t_rows, D_MODEL)
    tok = jnp.arange(s)
    cos, sin = _rope_tables(tok // GRID_W, tok % GRID_W)
    proj2d = _in_proj(x2d, norm_mix, w_in, col_scale, cos, sin, q_norm, k_norm, tm=_tile(s, 1024))
    proj = proj2d.reshape(b, s, IN_COLS)
    ka_m, va_m, kb_m, vb_m = meta_kv
    oa = _global_attn(proj, ka_m, va_m, tq=_tile(s, 256), ck=_tile(s, 1024))
    nblk = s // Q_BLOCK
    ob = _window_attn(proj, kb_m, vb_m, tab, tabm, sink_rows, nq=4 if nblk % 4 == 0 else 1)
    h2, u2p, logits = _merge_route(oa.reshape(t_rows, A_WIDTH), ob.reshape(t_rows, B_WIDTH), proj2d, x2d,
                                   wa, wb, wo, norm_ffn, wr, br, tm=_tile(t_rows, 512))
    info = _route_call(logits, tm=_tile(t_rows, 2048))
    e1 = info[:, 0].reshape(t_rows // LANES, LANES)
    e2 = info[:, 1].reshape(t_rows // LANES, LANES)
    n_tiles = 2 * t_rows // MOE_TILE + N_EXPERTS
    pos1, pos2, tile_expert, seg = _positions(e1, e2, n_tiles)
    pos1 = pos1.reshape(t_rows)
    pos2 = pos2.reshape(t_rows)
    seg = seg.reshape(LANES)
    xs = _dispatch(seg, pos1, pos2, u2p, n_tiles, tb=_tile(t_rows, 1024))
    ys = _moe(tile_expert.reshape(-1)[:n_tiles], seg, xs, wg, wu, wd)
    out = _final(pos1, pos2, h2, info, final_norm, ys, tm=_tile(t_rows, 256))
    return out.reshape(b, s, D_MODEL)


def kernel(x_prompt, x_sample, meta_tokens, rel_bias, final_norm, norm_mix, w_in, q_norm, k_norm, sink,
           w_branch_a, w_branch_b, w_out, norm_ffn, w_router_g, b_router_g, w_router_e, b_router_e,
           w_gate, w_up, w_down):
    assert norm_mix.shape[0] == 1, "single-layer encoder"
    rot_cols = (A_HEADS + A_KV_HEADS) * HEAD_DIM
    w_in_b = w_in[0].astype(BF16)
    w_in_b = lax.dynamic_update_slice(w_in_b, _swap_rotary_sections(w_in_b[:, :rot_cols]), (0, 0))
    norm_mix2 = norm_mix[0].reshape(1, D_MODEL)
    q_norm2 = _gain_rows(q_norm[0])
    k_norm2 = _gain_rows(k_norm[0])
    col = jnp.arange(IN_COLS)
    is_qb = (col >= COL_QB * HEAD_DIM) & (col < COL_KB * HEAD_DIM)
    col_scale = jnp.where(is_qb, SCORE_SCALE * LOG2E, 1.0).astype(F32).reshape(1, IN_COLS)

    cos_m, sin_m = _rope_tables(jnp.full((N_META,), -1), jnp.arange(N_META))
    proj_m = _in_proj(meta_tokens, norm_mix2, w_in_b, col_scale, cos_m, sin_m, q_norm2, k_norm2, tm=N_META)

    def meta_heads(c0):
        blk = proj_m[:, c0 * HEAD_DIM:(c0 + A_KV_HEADS) * HEAD_DIM]
        return blk.reshape(N_META, A_KV_HEADS, HEAD_DIM).transpose(1, 0, 2)

    meta_kv = tuple(meta_heads(c) for c in (COL_KA, COL_VA, COL_KB, COL_VB))

    tab, tabm = _bias_tables(rel_bias)
    sink_rows = jnp.repeat(sink[0].astype(F32), Q_BLOCK).reshape(B_KV_HEADS, REP * Q_BLOCK, 1)

    wr_full = jnp.zeros((D_MODEL, ROUTER_COLS), F32)
    wr_full = wr_full.at[:, :N_GROUPS].set(w_router_g[0]).at[:, N_GROUPS:N_GROUPS + N_EXPERTS].set(w_router_e[0])
    wr_hi, wr_lo = _split_bf16(wr_full)
    wr = jnp.concatenate([wr_hi, wr_lo], axis=1)
    br = jnp.zeros((1, ROUTER_COLS), F32)
    br = br.at[0, :N_GROUPS].set(b_router_g[0]).at[0, N_GROUPS:N_GROUPS + N_EXPERTS].set(b_router_e[0])

    shared = (norm_mix2, w_in_b, col_scale, q_norm2, k_norm2, meta_kv, tab, tabm, sink_rows,
              w_branch_a[0].astype(BF16), w_branch_b[0].astype(BF16), w_out[0].astype(BF16),
              norm_ffn[0].reshape(1, D_MODEL), wr, br,
              w_gate[0].astype(BF16), w_up[0].astype(BF16), w_down[0].astype(BF16),
              final_norm.reshape(1, D_MODEL))
    return (_encode_group(x_prompt, shared), _encode_group(x_sample, shared))
```

```python
import functools
import math

import numpy as np
import jax
import jax.numpy as jnp
from jax import lax
from jax.experimental import pallas as pl
from jax.experimental.pallas import tpu as pltpu

F32 = jnp.float32
BF16 = jnp.bfloat16

D_MODEL = 2048
HEAD_DIM = 128
A_HEADS = 8
A_KV_HEADS = 2
B_HEADS = 8
B_KV_HEADS = 2
REP = A_HEADS // A_KV_HEADS
A_WIDTH = A_HEADS * HEAD_DIM
B_WIDTH = B_HEADS * HEAD_DIM
KV_WIDTH = A_KV_HEADS * HEAD_DIM
IN_COLS = A_WIDTH + 2 * KV_WIDTH + B_WIDTH + 2 * KV_WIDTH + 2 * D_MODEL
Q_BLOCK = 128
WINDOW = 128
N_META = 16
GRID_W = 64
ROPE_THETA = 10000.0
N_BUCKETS = 32
MAX_DISTANCE = 128
N_GROUPS = 4
EXPERTS_PER_GROUP = 4
N_EXPERTS = N_GROUPS * EXPERTS_PER_GROUP
EXPERT_FF = 1024
RMS_EPS = 1e-6
NEG_INF = -1e30
SCORE_SCALE = HEAD_DIM ** -0.5
LOG2E = math.log2(math.e)

COL_QA = 0
COL_KA = A_WIDTH // HEAD_DIM
COL_VA = COL_KA + A_KV_HEADS
COL_QB = COL_VA + A_KV_HEADS
COL_KB = COL_QB + B_HEADS
COL_VB = COL_KB + B_KV_HEADS
COL_GA = COL_VB + B_KV_HEADS
COL_GB = COL_GA + D_MODEL // HEAD_DIM

LANES = 128
VMEM_LIMIT = 56 * 1024 * 1024
MOE_TILE = 512
ROUTER_COLS = 128

_NT = (((1,), (1,)), ((), ()))


def _params(sem, vmem=VMEM_LIMIT):
    return pltpu.CompilerParams(dimension_semantics=sem, vmem_limit_bytes=vmem)


def _const_spec(shape):
    nd = len(shape)
    return pl.BlockSpec(shape, lambda *_: (0,) * nd, pipeline_mode=pl.Buffered(1))


def _swap_rotary_sections(w):
    lead = w.shape[:-1]
    heads = w.shape[-1] // HEAD_DIM
    nd = len(lead)
    w5 = w.reshape(lead + (heads, 2, 2, HEAD_DIM // 4))
    return jnp.swapaxes(w5, nd + 1, nd + 2).reshape(w.shape)


def _norm_rope(a, cg, sg):
    ones = jnp.ones((HEAD_DIM, HEAD_DIM), BF16)
    ssq = jnp.dot((a * a).astype(BF16), ones, preferred_element_type=F32)
    rinv = lax.rsqrt(ssq * (1.0 / HEAD_DIM) + RMS_EPS)
    return rinv * (a * cg + pltpu.roll(a, HEAD_DIM // 2, 1) * sg)


def _in_proj_kernel(x_ref, g_ref, w_ref, cs_ref, cos_ref, sin_ref, qn_ref, kn_ref, o_ref, u_ref, *, sub):
    j = pl.program_id(1)
    row_tiles = [slice(k * sub, (k + 1) * sub) for k in range(x_ref.shape[0] // sub)]

    def project(r):
        return jnp.dot(u_ref[r, :], w_ref[...], preferred_element_type=F32) * cs_ref[...]

    def rope_factors(gain_ref, scale, r):
        return cos_ref[r, :] * (gain_ref[0:1, :] * scale), sin_ref[r, :] * (gain_ref[1:2, :] * scale)

    @pl.when(j == 0)
    def _():
        for r in row_tiles:
            x = x_ref[r, :]
            ms = jnp.mean(x * x, axis=-1, keepdims=True)
            u_ref[r, :] = (x * lax.rsqrt(ms + RMS_EPS) * g_ref[...]).astype(BF16)
            acc = project(r)
            cg, sg = rope_factors(qn_ref, SCORE_SCALE * LOG2E, r)
            for h in range(A_HEADS):
                sl = slice(h * HEAD_DIM, (h + 1) * HEAD_DIM)
                o_ref[r, sl] = _norm_rope(acc[:, sl], cg, sg).astype(BF16)

    @pl.when(j == 1)
    def _():
        for r in row_tiles:
            acc = project(r)
            cg, sg = rope_factors(kn_ref, 1.0, r)
            for h in range(A_KV_HEADS):
                sl = slice(h * HEAD_DIM, (h + 1) * HEAD_DIM)
                o_ref[r, sl] = _norm_rope(acc[:, sl], cg, sg).astype(BF16)
            o_ref[r, KV_WIDTH:] = acc[:, KV_WIDTH:].astype(BF16)

    @pl.when(j >= 2)
    def _():
        for r in row_tiles:
            o_ref[r, :] = project(r).astype(BF16)


def _in_proj(x2d, norm_g, w_bf16, col_scale, cos, sin, q_norm, k_norm, *, tm):
    t_rows = x2d.shape[0]
    tn = A_WIDTH
    pos_tiles = cos.shape[0] // tm
    return pl.pallas_call(
        functools.partial(_in_proj_kernel, sub=min(tm, 512)),
        grid=(t_rows // tm, IN_COLS // tn),
        in_specs=[
            pl.BlockSpec((tm, D_MODEL), lambda i, j: (i, 0)),
            pl.BlockSpec((1, D_MODEL), lambda i, j: (0, 0)),
            pl.BlockSpec((D_MODEL, tn), lambda i, j: (0, j)),
            pl.BlockSpec((1, tn), lambda i, j: (0, j)),
            pl.BlockSpec((tm, HEAD_DIM), lambda i, j: (i % pos_tiles, 0)),
            pl.BlockSpec((tm, HEAD_DIM), lambda i, j: (i % pos_tiles, 0)),
            pl.BlockSpec((2, HEAD_DIM), lambda i, j: (0, 0)),
            pl.BlockSpec((2, HEAD_DIM), lambda i, j: (0, 0)),
        ],
        out_specs=pl.BlockSpec((tm, tn), lambda i, j: (i, j)),
        out_shape=jax.ShapeDtypeStruct((t_rows, IN_COLS), BF16),
        scratch_shapes=[pltpu.VMEM((tm, D_MODEL), BF16)],
        compiler_params=_params(("parallel", "arbitrary")),
        name="in_proj",
    )(x2d, norm_g, w_bf16, col_scale, cos, sin, q_norm, k_norm)


def _stack_heads(q):
    return jnp.concatenate([q[:, r * HEAD_DIM:(r + 1) * HEAD_DIM] for r in range(REP)], axis=0)


def _global_attn_kernel(q_ref, k_ref, v_ref, km_ref, vm_ref, o_ref, vx_ref, vmx_ref, *, tq, ck):
    @pl.when(pl.program_id(2) == 0)
    def _():
        vx_ref[:, :HEAD_DIM] = v_ref[0]
        vx_ref[:, HEAD_DIM:] = jnp.ones((v_ref.shape[1], HEAD_DIM), BF16)
        vmx_ref[:, :HEAD_DIM] = vm_ref[0]
        vmx_ref[:, HEAD_DIM:] = jnp.ones((N_META, HEAD_DIM), BF16)

    qs = _stack_heads(q_ref[0])
    n_chunks = k_ref.shape[1] // ck

    def scores(c):
        return lax.dot_general(qs, k_ref[0, c * ck:(c + 1) * ck, :], _NT, preferred_element_type=F32)

    s = lax.dot_general(qs, km_ref[0], _NT, preferred_element_type=F32)
    s_next = scores(0)
    m = jnp.max(s, axis=-1, keepdims=True)
    acc = jnp.dot(jnp.exp2(s - m).astype(BF16), vmx_ref[...], preferred_element_type=F32)
    for c in range(n_chunks):
        s = s_next
        if c + 1 < n_chunks:
            s_next = scores(c + 1)
        m_new = jnp.maximum(m, jnp.max(s, axis=-1, keepdims=True))
        p = jnp.exp2(s - m_new).astype(BF16)
        acc = jnp.exp2(m - m_new) * acc + jnp.dot(p, vx_ref[c * ck:(c + 1) * ck, :],
                                                  preferred_element_type=F32)
        m = m_new
    o = acc[:, :HEAD_DIM] / acc[:, HEAD_DIM:]
    for r in range(REP):
        o_ref[0, :, r * HEAD_DIM:(r + 1) * HEAD_DIM] = o[r * tq:(r + 1) * tq].astype(BF16)


def _global_attn(proj, km, vm, *, tq, ck):
    b, s, _ = proj.shape
    gw = REP * HEAD_DIM
    return pl.pallas_call(
        functools.partial(_global_attn_kernel, tq=tq, ck=ck),
        grid=(b, A_KV_HEADS, s // tq),
        in_specs=[
            pl.BlockSpec((1, tq, gw), lambda bi, g, i: (bi, i, g)),
            pl.BlockSpec((1, s, HEAD_DIM), lambda bi, g, i: (bi, 0, COL_KA + g)),
            pl.BlockSpec((1, s, HEAD_DIM), lambda bi, g, i: (bi, 0, COL_VA + g)),
            pl.BlockSpec((1, N_META, HEAD_DIM), lambda bi, g, i: (g, 0, 0)),
            pl.BlockSpec((1, N_META, HEAD_DIM), lambda bi, g, i: (g, 0, 0)),
        ],
        out_specs=pl.BlockSpec((1, tq, gw), lambda bi, g, i: (bi, i, g)),
        out_shape=jax.ShapeDtypeStruct((b, s, A_WIDTH), BF16),
        scratch_shapes=[pltpu.VMEM((s, 2 * HEAD_DIM), BF16), pltpu.VMEM((N_META, 2 * HEAD_DIM), BF16)],
        compiler_params=_params(("parallel", "parallel", "arbitrary")),
        name="global_attn",
    )(proj, proj, proj, km, vm)


def _t5_bucket_np(rel):
    nb = N_BUCKETS // 2
    max_exact = nb // 2
    bucket = np.where(rel > 0, nb, 0)
    n = np.abs(rel)
    nf = np.maximum(n, 1).astype(np.float32)
    large = max_exact + (np.log(nf / np.float32(max_exact)) / np.float32(math.log(MAX_DISTANCE / max_exact))
                         * np.float32(nb - max_exact)).astype(np.int32)
    large = np.minimum(large, nb - 1)
    return (bucket + np.where(n < max_exact, n, large)).astype(np.int32)


def _bucket_maps():
    i = np.arange(Q_BLOCK)[:, None]
    j = np.arange(3 * Q_BLOCK)[None, :]
    real = []
    for off in range(3):
        rel = j - off * Q_BLOCK - i
        real.append(np.where(np.abs(rel) <= WINDOW, _t5_bucket_np(rel), -1))
    m = np.arange(N_META)[None, :]
    first = _t5_bucket_np(m - (N_META + i))
    later = _t5_bucket_np(m - (N_META + i + Q_BLOCK))
    meta = [first, later, later]
    return np.stack(real).astype(np.int32), np.stack(meta).astype(np.int32)


def _bias_table_kernel(rb_ref, bm_ref, bmm_ref, o_ref, om_ref):
    h = pl.program_id(1)
    bm = bm_ref[0]
    bmm = bmm_ref[0]
    acc = jnp.full(bm.shape, NEG_INF, F32)
    accm = jnp.full(bmm.shape, NEG_INF, F32)
    for k in range(N_BUCKETS):
        val = rb_ref[k, h] * LOG2E
        acc = jnp.where(bm == k, val, acc)
        accm = jnp.where(bmm == k, val, accm)
    o_ref[0, 0] = acc
    om_ref[0, 0] = accm


def _bias_tables(rel_bias):
    bm, bmm = _bucket_maps()
    kw = 3 * Q_BLOCK
    tab, tabm = pl.pallas_call(
        _bias_table_kernel,
        grid=(3, B_HEADS),
        in_specs=[
            pl.BlockSpec(memory_space=pltpu.SMEM),
            pl.BlockSpec((1, Q_BLOCK, kw), lambda v, h: (v, 0, 0)),
            pl.BlockSpec((1, Q_BLOCK, N_META), lambda v, h: (v, 0, 0)),
        ],
        out_specs=[
            pl.BlockSpec((1, 1, Q_BLOCK, kw), lambda v, h: (v, h, 0, 0)),
            pl.BlockSpec((1, 1, Q_BLOCK, N_META), lambda v, h: (v, h, 0, 0)),
        ],
        out_shape=[
            jax.ShapeDtypeStruct((3, B_HEADS, Q_BLOCK, kw), F32),
            jax.ShapeDtypeStruct((3, B_HEADS, Q_BLOCK, N_META), F32),
        ],
        compiler_params=_params(("arbitrary", "arbitrary")),
        name="bias_tables",
    )(rel_bias, jnp.asarray(bm), jnp.asarray(bmm))
    rows = REP * Q_BLOCK
    return (tab.reshape(3, B_KV_HEADS, rows, kw), tabm.reshape(3, B_KV_HEADS, rows, N_META))


def _window_attn_kernel(q_ref, k_ref, v_ref, km_ref, vm_ref, tab_ref, tabm_ref, sink_ref, o_ref, *, nq):
    jb = pl.program_id(2)
    nblk = k_ref.shape[1] // Q_BLOCK
    km = km_ref[0]
    vm = vm_ref[0]
    sink = sink_ref[0] * LOG2E
    for t in range(nq):
        n = jb * nq + t
        qs = _stack_heads(q_ref[0, t * Q_BLOCK:(t + 1) * Q_BLOCK, :])
        var = jnp.where(n == 0, 0, jnp.where(n == nblk - 1, 2, 1))
        start = pl.multiple_of(jnp.clip(n - 1, 0, nblk - 3) * Q_BLOCK, Q_BLOCK)
        kb = k_ref[0, pl.ds(start, 3 * Q_BLOCK), :]
        vb = v_ref[0, pl.ds(start, 3 * Q_BLOCK), :]
        s = lax.dot_general(qs, kb, _NT, preferred_element_type=F32) + tab_ref[var, 0]
        sm = lax.dot_general(qs, km, _NT, preferred_element_type=F32) + tabm_ref[var, 0]
        m = jnp.maximum(jnp.maximum(jnp.max(s, axis=-1, keepdims=True),
                                    jnp.max(sm, axis=-1, keepdims=True)), sink)
        p = jnp.exp2(s - m)
        pm = jnp.exp2(sm - m)
        l = jnp.sum(p, axis=-1, keepdims=True) + jnp.sum(pm, axis=-1, keepdims=True) + jnp.exp2(sink - m)
        o = (jnp.dot(p.astype(BF16), vb, preferred_element_type=F32)
             + jnp.dot(pm.astype(BF16), vm, preferred_element_type=F32)) / l
        for r in range(REP):
            o_ref[0, t * Q_BLOCK:(t + 1) * Q_BLOCK, r * HEAD_DIM:(r + 1) * HEAD_DIM] = (
                o[r * Q_BLOCK:(r + 1) * Q_BLOCK].astype(BF16))


def _window_attn(proj, km, vm, tab, tabm, sink_rows, *, nq):
    b, s, _ = proj.shape
    assert s // Q_BLOCK >= 3 and (s // Q_BLOCK) % nq == 0
    gw = REP * HEAD_DIM
    rows = REP * Q_BLOCK
    kw = 3 * Q_BLOCK
    tq = nq * Q_BLOCK
    return pl.pallas_call(
        functools.partial(_window_attn_kernel, nq=nq),
        grid=(b, B_KV_HEADS, s // tq),
        in_specs=[
            pl.BlockSpec((1, tq, gw), lambda bi, g, i: (bi, i, COL_QB // REP + g)),
            pl.BlockSpec((1, s, HEAD_DIM), lambda bi, g, i: (bi, 0, COL_KB + g)),
            pl.BlockSpec((1, s, HEAD_DIM), lambda bi, g, i: (bi, 0, COL_VB + g)),
            pl.BlockSpec((1, N_META, HEAD_DIM), lambda bi, g, i: (g, 0, 0)),
            pl.BlockSpec((1, N_META, HEAD_DIM), lambda bi, g, i: (g, 0, 0)),
            pl.BlockSpec((3, 1, rows, kw), lambda bi, g, i: (0, g, 0, 0)),
            pl.BlockSpec((3, 1, rows, N_META), lambda bi, g, i: (0, g, 0, 0)),
            pl.BlockSpec((1, rows, 1), lambda bi, g, i: (g, 0, 0)),
        ],
        out_specs=pl.BlockSpec((1, tq, gw), lambda bi, g, i: (bi, i, g)),
        out_shape=jax.ShapeDtypeStruct((b, s, B_WIDTH), BF16),
        compiler_params=_params(("parallel", "parallel", "arbitrary")),
        name="window_attn",
    )(proj, proj, proj, km, vm, tab, tabm, sink_rows)


def _sigmoid(x):
    return 1.0 / (1.0 + jnp.exp(-x))


def _route(logits):
    lane = lax.broadcasted_iota(jnp.int32, logits.shape, 1).astype(F32)
    ninf = jnp.float32(-jnp.inf)
    big = jnp.float32(ROUTER_COLS)
    is_g = lane < N_GROUPS
    lg = jnp.where(is_g, logits, ninf)
    mg = jnp.max(lg, axis=-1, keepdims=True)
    gidx = jnp.min(jnp.where(lg == mg, lane, big), axis=-1, keepdims=True)
    p_top = 1.0 / jnp.sum(jnp.where(is_g, jnp.exp(lg - mg), 0.0), axis=-1, keepdims=True)
    lo = N_GROUPS + EXPERTS_PER_GROUP * gidx
    sel = (lane >= lo) & (lane < lo + EXPERTS_PER_GROUP)
    le = jnp.where(sel, logits, ninf)
    v1 = jnp.max(le, axis=-1, keepdims=True)
    i1 = jnp.min(jnp.where(sel & (le == v1), lane, big), axis=-1, keepdims=True)
    rest = sel & (lane != i1)
    le2 = jnp.where(rest, logits, ninf)
    v2 = jnp.max(le2, axis=-1, keepdims=True)
    i2 = jnp.min(jnp.where(rest & (le2 == v2), lane, big), axis=-1, keepdims=True)
    t = jnp.exp(v2 - v1)
    w1 = p_top / (1.0 + t)
    w2 = p_top * t / (1.0 + t)
    return i1 - N_GROUPS, i2 - N_GROUPS, w1, w2


def _merge_kernel(oa_ref, ob_ref, ga0_ref, ga1_ref, gb0_ref, gb1_ref, x_ref, wa_ref, wb_ref, wo_ref, gn_ref,
                  wr_ref, br_ref, h2_ref, u2_ref, logits_ref, *, sub):
    for st in range(oa_ref.shape[0] // sub):
        r = slice(st * sub, (st + 1) * sub)
        ya = jnp.dot(oa_ref[r, :], wa_ref[...], preferred_element_type=F32)
        yb = jnp.dot(ob_ref[r, :], wb_ref[...], preferred_element_type=F32)
        ga = jnp.concatenate([ga0_ref[r, :], ga1_ref[r, :]], axis=1).astype(F32)
        gb = jnp.concatenate([gb0_ref[r, :], gb1_ref[r, :]], axis=1).astype(F32)
        mixed = _sigmoid(ga) * ya + _sigmoid(gb) * yb
        h2 = x_ref[r, :] + jnp.dot(mixed.astype(BF16), wo_ref[...], preferred_element_type=F32)
        h2_ref[r, :] = h2
        ms = jnp.mean(h2 * h2, axis=-1, keepdims=True)
        u = h2 * lax.rsqrt(ms + RMS_EPS) * gn_ref[...]
        u_hi = u.astype(BF16)
        u_hi32 = u_hi.astype(F32)
        u_lo = (u - u_hi32).astype(BF16)
        lg = (jnp.dot(u_hi, wr_ref[...], preferred_element_type=F32)
              + jnp.dot(u_lo, wr_ref[...], preferred_element_type=F32))
        logits_ref[r, :] = lg[:, :ROUTER_COLS] + lg[:, ROUTER_COLS:] + br_ref[...]
        half = D_MODEL // 2
        lo_bits = pltpu.bitcast(u_hi32[:, :half], jnp.uint32) >> 16
        hi_bits = pltpu.bitcast(u_hi32[:, half:], jnp.uint32)
        u2_ref[r, :] = hi_bits | lo_bits


def _merge_route(oa, ob, proj2d, x2d, wa, wb, wo, norm_ffn, wr, br, *, tm):
    t_rows = x2d.shape[0]
    gw = D_MODEL // 2
    ga_blk = COL_GA * HEAD_DIM // gw
    gb_blk = COL_GB * HEAD_DIM // gw
    return pl.pallas_call(
        functools.partial(_merge_kernel, sub=min(tm, 256)),
        grid=(t_rows // tm,),
        in_specs=[
            pl.BlockSpec((tm, A_WIDTH), lambda i: (i, 0)),
            pl.BlockSpec((tm, B_WIDTH), lambda i: (i, 0)),
            pl.BlockSpec((tm, gw), lambda i: (i, ga_blk)),
            pl.BlockSpec((tm, gw), lambda i: (i, ga_blk + 1)),
            pl.BlockSpec((tm, gw), lambda i: (i, gb_blk)),
            pl.BlockSpec((tm, gw), lambda i: (i, gb_blk + 1)),
            pl.BlockSpec((tm, D_MODEL), lambda i: (i, 0)),
            _const_spec((A_WIDTH, D_MODEL)),
            _const_spec((B_WIDTH, D_MODEL)),
            _const_spec((D_MODEL, D_MODEL)),
            _const_spec((1, D_MODEL)),
            _const_spec((D_MODEL, 2 * ROUTER_COLS)),
            _const_spec((1, ROUTER_COLS)),
        ],
        out_specs=[
            pl.BlockSpec((tm, D_MODEL), lambda i: (i, 0)),
            pl.BlockSpec((tm, D_MODEL // 2), lambda i: (i, 0)),
            pl.BlockSpec((tm, ROUTER_COLS), lambda i: (i, 0)),
        ],
        out_shape=[
            jax.ShapeDtypeStruct((t_rows, D_MODEL), F32),
            jax.ShapeDtypeStruct((t_rows, D_MODEL // 2), jnp.uint32),
            jax.ShapeDtypeStruct((t_rows, ROUTER_COLS), F32),
        ],
        compiler_params=_params(("parallel",)),
        name="merge_route",
    )(oa, ob, proj2d, proj2d, proj2d, proj2d, x2d, wa, wb, wo, norm_ffn, wr, br)


def _route_kernel(logits_ref, info_ref):
    logits = logits_ref[...]
    e1, e2, w1, w2 = _route(logits)
    lane = lax.broadcasted_iota(jnp.int32, logits.shape, 1)
    info_ref[...] = jnp.where(lane == 0, e1, jnp.where(lane == 1, e2, jnp.where(lane == 2, w1,
                              jnp.where(lane == 3, w2, 0.0))))


def _route_call(logits, *, tm):
    t_rows = logits.shape[0]
    return pl.pallas_call(
        _route_kernel,
        grid=(t_rows // tm,),
        in_specs=[pl.BlockSpec((tm, ROUTER_COLS), lambda i: (i, 0))],
        out_specs=pl.BlockSpec((tm, ROUTER_COLS), lambda i: (i, 0)),
        out_shape=jax.ShapeDtypeStruct((t_rows, ROUTER_COLS), F32),
        compiler_params=_params(("parallel",)),
        name="route",
    )(logits)


SEG_NTILES = 2 * N_EXPERTS


def _positions_kernel(e1_ref, e2_ref, pos1_ref, pos2_ref, te_ref, seg_ref):
    e1 = e1_ref[...]
    e2 = e2_ref[...]
    rows = e1.shape[0]
    r_i = lax.broadcasted_iota(jnp.int32, (LANES, LANES), 0)
    c_i = lax.broadcasted_iota(jnp.int32, (LANES, LANES), 1)
    upper = (r_i < c_i).astype(BF16)
    rr = lax.broadcasted_iota(jnp.int32, (rows, rows), 0)
    rc = lax.broadcasted_iota(jnp.int32, (rows, rows), 1)
    lower = (rc < rr).astype(BF16)
    seg_lane = lax.broadcasted_iota(jnp.int32, seg_ref.shape, 1)
    tile_row = lax.broadcasted_iota(jnp.int32, te_ref.shape, 1).astype(F32) * MOE_TILE
    base = jnp.zeros((1, 1), F32)
    pos1 = jnp.zeros(e1.shape, F32)
    pos2 = jnp.zeros(e1.shape, F32)
    seg = jnp.zeros(seg_ref.shape, F32)
    tile_expert = jnp.zeros(te_ref.shape, F32)
    for e in range(N_EXPERTS):
        m1 = e1 == e
        m2 = e2 == e
        m = jnp.where(m1 | m2, 1.0, 0.0)
        lane_pre = jnp.dot(m.astype(BF16), upper, preferred_element_type=F32)
        row_tot = jnp.broadcast_to(jnp.sum(m, axis=-1, keepdims=True), m.shape)
        row_pre = jnp.dot(lower, row_tot.astype(BF16), preferred_element_type=F32)
        total = jnp.sum(row_tot[:, 0:1], axis=0, keepdims=True)
        p = base + row_pre + lane_pre
        pos1 = jnp.where(m1, p, pos1)
        pos2 = jnp.where(m2, p, pos2)
        padded = jnp.floor((total + (MOE_TILE - 1)) * (1.0 / MOE_TILE)) * MOE_TILE
        base = base + padded
        seg = seg + jnp.where(seg_lane == e, base, 0.0) + jnp.where(seg_lane == N_EXPERTS + e, total, 0.0)
        tile_expert = tile_expert + jnp.where(tile_row >= base, 1.0, 0.0)
    seg = seg + jnp.where(seg_lane == SEG_NTILES, base * (1.0 / MOE_TILE), 0.0)
    pos1_ref[...] = pos1.astype(jnp.int32)
    pos2_ref[...] = pos2.astype(jnp.int32)
    te_ref[...] = jnp.minimum(tile_expert, N_EXPERTS - 1).astype(jnp.int32)
    seg_ref[...] = seg.astype(jnp.int32)


def _positions(e1, e2, n_tiles):
    rows = e1.shape[0]
    ntp = -(-n_tiles // LANES) * LANES
    full = lambda shape: pl.BlockSpec(shape, lambda: (0,) * len(shape))
    return pl.pallas_call(
        _positions_kernel,
        in_specs=[full((rows, LANES)), full((rows, LANES))],
        out_specs=[full((rows, LANES)), full((rows, LANES)), full((1, ntp)), full((1, LANES))],
        out_shape=[
            jax.ShapeDtypeStruct((rows, LANES), jnp.int32),
            jax.ShapeDtypeStruct((rows, LANES), jnp.int32),
            jax.ShapeDtypeStruct((1, ntp), jnp.int32),
            jax.ShapeDtypeStruct((1, LANES), jnp.int32),
        ],
        compiler_params=pltpu.CompilerParams(vmem_limit_bytes=VMEM_LIMIT),
        name="positions",
    )(e1, e2)


def _dispatch_kernel(seg_ref, pos1_ref, pos2_ref, u_ref, xs_ref, z_ref, zsem, sem, *, tb):
    i = pl.program_id(0)

    def zero_copy(e):
        end = seg_ref[e]
        start = pl.multiple_of(end - MOE_TILE, MOE_TILE)
        return pltpu.make_async_copy(z_ref, xs_ref.at[pl.ds(start, MOE_TILE)], zsem)

    def nonempty(e):
        return seg_ref[e] > (seg_ref[e - 1] if e > 0 else 0)

    n_tiles = xs_ref.shape[0] // MOE_TILE

    def tail_copy(k):
        start = pl.multiple_of((seg_ref[SEG_NTILES] + k) * MOE_TILE, MOE_TILE)
        return pltpu.make_async_copy(z_ref, xs_ref.at[pl.ds(start, MOE_TILE)], zsem)

    def tail_exists(k):
        return seg_ref[SEG_NTILES] + k < n_tiles

    @pl.when(i == 0)
    def _():
        z_ref[...] = jnp.zeros(z_ref.shape, z_ref.dtype)
        for e in range(N_EXPERTS):
            @pl.when(nonempty(e))
            def _():
                zero_copy(e).start()

            @pl.when(tail_exists(e))
            def _():
                tail_copy(e).start()
        for e in range(N_EXPERTS):
            @pl.when(nonempty(e))
            def _():
                zero_copy(e).wait()

            @pl.when(tail_exists(e))
            def _():
                tail_copy(e).wait()

    def row_copy(t, pos_ref):
        return pltpu.make_async_copy(u_ref.at[pl.ds(t, 1)], xs_ref.at[pl.ds(pos_ref[t], 1)], sem)

    def issue(t, carry):
        row_copy(t, pos1_ref).start()
        row_copy(t, pos2_ref).start()
        return carry

    lax.fori_loop(0, tb, issue, 0, unroll=8)

    for _ in range(2):
        pltpu.make_async_copy(u_ref, xs_ref.at[pl.ds(0, tb)], sem).wait()


def _dispatch(seg, pos1, pos2, u2p, n_tiles, *, tb):
    t_rows, width = u2p.shape
    return pl.pallas_call(
        functools.partial(_dispatch_kernel, tb=tb),
        grid_spec=pltpu.PrefetchScalarGridSpec(
            num_scalar_prefetch=1,
            grid=(t_rows // tb,),
            in_specs=[
                pl.BlockSpec((tb,), lambda i, seg: (i,), memory_space=pltpu.SMEM),
                pl.BlockSpec((tb,), lambda i, seg: (i,), memory_space=pltpu.SMEM),
                pl.BlockSpec((tb, width), lambda i, seg: (i, 0)),
            ],
            out_specs=pl.BlockSpec(memory_space=pl.ANY),
            scratch_shapes=[
                pltpu.VMEM((MOE_TILE, width), u2p.dtype),
                pltpu.SemaphoreType.DMA(()),
                pltpu.SemaphoreType.DMA(()),
            ],
        ),
        out_shape=jax.ShapeDtypeStruct((n_tiles * MOE_TILE, width), u2p.dtype),
        compiler_params=_params(("arbitrary",)),
        name="dispatch",
    )(seg, pos1, pos2, u2p)


def _moe_kernel(te_ref, seg_ref, x_ref, wg_ref, wu_ref, wd_ref, y_ref):
    i = pl.program_id(0)

    @pl.when(i < seg_ref[SEG_NTILES])
    def _():
        xw = x_ref[...]
        half = D_MODEL // 2
        lo = pltpu.bitcast(xw << 16, F32).astype(BF16)
        hi = pltpu.bitcast(xw & jnp.uint32(0xFFFF0000), F32).astype(BF16)
        hg = (jnp.dot(lo, wg_ref[0, :half, :], preferred_element_type=F32)
              + jnp.dot(hi, wg_ref[0, half:, :], preferred_element_type=F32))
        hu = (jnp.dot(lo, wu_ref[0, :half, :], preferred_element_type=F32)
              + jnp.dot(hi, wu_ref[0, half:, :], preferred_element_type=F32))
        hid = hg * _sigmoid(hg) * hu
        y_ref[...] = jnp.dot(hid.astype(BF16), wd_ref[0], preferred_element_type=F32)

    @pl.when(i >= seg_ref[SEG_NTILES])
    def _():
        y_ref[...] = jnp.zeros(y_ref.shape, y_ref.dtype)


def _moe(tile_expert, seg, xs, wg, wu, wd):
    n_tiles = xs.shape[0] // MOE_TILE

    def row_map(i, te, seg):
        return (jnp.minimum(i, seg[SEG_NTILES] - 1), 0)

    def w_map(i, te, seg):
        return (te[jnp.minimum(i, seg[SEG_NTILES] - 1)], 0, 0)

    return pl.pallas_call(
        _moe_kernel,
        grid_spec=pltpu.PrefetchScalarGridSpec(
            num_scalar_prefetch=2,
            grid=(n_tiles,),
            in_specs=[
                pl.BlockSpec((MOE_TILE, D_MODEL // 2), row_map),
                pl.BlockSpec((1, D_MODEL, EXPERT_FF), w_map),
                pl.BlockSpec((1, D_MODEL, EXPERT_FF), w_map),
                pl.BlockSpec((1, EXPERT_FF, D_MODEL), w_map),
            ],
            out_specs=pl.BlockSpec((MOE_TILE, D_MODEL), lambda i, te, seg: (i, 0)),
        ),
        out_shape=jax.ShapeDtypeStruct((n_tiles * MOE_TILE, D_MODEL), F32),
        compiler_params=_params(("arbitrary",)),
        name="moe",
    )(tile_expert, seg, xs, wg, wu, wd)


def _final_kernel(pos1_ref, pos2_ref, nxt1_ref, nxt2_ref, h2_ref, info_ref, g_ref, ys_ref, o_ref, ybuf, sems, *, tm):
    i = pl.program_id(0)
    n = pl.num_programs(0)
    slot = i % 2

    def gather(p1_ref, p2_ref, s):
        def issue(t, carry):
            pltpu.make_async_copy(ys_ref.at[pl.ds(p1_ref[t], 1)], ybuf.at[s, 0, pl.ds(t, 1)], sems.at[s]).start()
            pltpu.make_async_copy(ys_ref.at[pl.ds(p2_ref[t], 1)], ybuf.at[s, 1, pl.ds(t, 1)], sems.at[s]).start()
            return carry

        lax.fori_loop(0, tm, issue, 0, unroll=8)

    @pl.when(i == 0)
    def _():
        gather(pos1_ref, pos2_ref, 0)

    @pl.when(i + 1 < n)
    def _():
        gather(nxt1_ref, nxt2_ref, 1 - slot)

    for k in range(2):
        pltpu.make_async_copy(ys_ref.at[pl.ds(0, tm)], ybuf.at[slot, k], sems.at[slot]).wait()

    info = info_ref[...]
    h = h2_ref[...] + info[:, 2:3] * ybuf[slot, 0] + info[:, 3:4] * ybuf[slot, 1]
    ms = jnp.mean(h * h, axis=-1, keepdims=True)
    o_ref[...] = h * lax.rsqrt(ms + RMS_EPS) * g_ref[...]


def _final(pos1, pos2, h2, info, final_norm, ys, *, tm):
    t_rows = h2.shape[0]
    n_steps = t_rows // tm
    nxt = lambda i: (jnp.minimum(i + 1, n_steps - 1),)
    return pl.pallas_call(
        functools.partial(_final_kernel, tm=tm),
        grid=(n_steps,),
        in_specs=[
            pl.BlockSpec((tm,), lambda i: (i,), memory_space=pltpu.SMEM),
            pl.BlockSpec((tm,), lambda i: (i,), memory_space=pltpu.SMEM),
            pl.BlockSpec((tm,), nxt, memory_space=pltpu.SMEM),
            pl.BlockSpec((tm,), nxt, memory_space=pltpu.SMEM),
            pl.BlockSpec((tm, D_MODEL), lambda i: (i, 0)),
            pl.BlockSpec((tm, ROUTER_COLS), lambda i: (i, 0)),
            pl.BlockSpec((1, D_MODEL), lambda i: (0, 0)),
            pl.BlockSpec(memory_space=pl.ANY),
        ],
        out_specs=pl.BlockSpec((tm, D_MODEL), lambda i: (i, 0)),
        out_shape=jax.ShapeDtypeStruct((t_rows, D_MODEL), F32),
        scratch_shapes=[pltpu.VMEM((2, 2, tm, D_MODEL), F32), pltpu.SemaphoreType.DMA((2,))],
        compiler_params=_params(("arbitrary",)),
        name="final",
    )(pos1, pos2, pos1, pos2, h2, info, final_norm, ys)


def _rope_tables(row, col):
    n_freq = HEAD_DIM // 4
    inv_freq = ROPE_THETA ** (-jnp.arange(n_freq, dtype=F32) / n_freq)
    ra = row.astype(F32)[:, None] * inv_freq
    ca = col.astype(F32)[:, None] * inv_freq
    cos = jnp.concatenate([jnp.cos(ra), jnp.cos(ca), jnp.cos(ra), jnp.cos(ca)], axis=-1)
    sin = jnp.concatenate([-jnp.sin(ra), -jnp.sin(ca), jnp.sin(ra), jnp.sin(ca)], axis=-1)
    return cos, sin


def _gain_rows(g):
    gs = _swap_rotary_sections(g.astype(F32))
    return jnp.stack([gs, jnp.roll(gs, HEAD_DIM // 2)])


def _split_bf16(w):
    hi = w.astype(BF16)
    lo = (w - hi.astype(F32)).astype(BF16)
    return hi, lo


def _tile(n, pref):
    t = min(n, pref)
    assert n % t == 0, (n, pref)
    return t


def _encode_group(x, shared):
    (norm_mix, w_in, col_scale, q_norm, k_norm, meta_kv, tab, tabm, sink_rows, wa, wb, wo, norm_ffn, wr, br,
     wg, wu, wd, final_norm) = shared
    b, s, _ = x.shape
    t_rows = b * s
    x2d = x.reshape(t_rows, D_MODEL)
    tok = jnp.arange(s)
    cos, sin = _rope_tables(tok // GRID_W, tok % GRID_W)
    proj2d = _in_proj(x2d, norm_mix, w_in, col_scale, cos, sin, q_norm, k_norm, tm=_tile(s, 1024))
    proj = proj2d.reshape(b, s, IN_COLS)
    ka_m, va_m, kb_m, vb_m = meta_kv
    oa = _global_attn(proj, ka_m, va_m, tq=_tile(s, 256), ck=_tile(s, 1024))
    nblk = s // Q_BLOCK
    ob = _window_attn(proj, kb_m, vb_m, tab, tabm, sink_rows, nq=math.gcd(nblk, 8))
    h2, u2p, logits = _merge_route(oa.reshape(t_rows, A_WIDTH), ob.reshape(t_rows, B_WIDTH), proj2d, x2d,
                                   wa, wb, wo, norm_ffn, wr, br, tm=_tile(t_rows, 512))
    info = _route_call(logits, tm=_tile(t_rows, 2048))
    e1 = info[:, 0].reshape(t_rows // LANES, LANES)
    e2 = info[:, 1].reshape(t_rows // LANES, LANES)
    n_tiles = 2 * t_rows // MOE_TILE + N_EXPERTS
    pos1, pos2, tile_expert, seg = _positions(e1, e2, n_tiles)
    pos1 = pos1.reshape(t_rows)
    pos2 = pos2.reshape(t_rows)
    seg = seg.reshape(LANES)
    xs = _dispatch(seg, pos1, pos2, u2p, n_tiles, tb=_tile(t_rows, 2048))
    ys = _moe(tile_expert.reshape(-1)[:n_tiles], seg, xs, wg, wu, wd)
    out = _final(pos1, pos2, h2, info, final_norm, ys, tm=_tile(t_rows, 512))
    return out.reshape(b, s, D_MODEL)


def kernel(x_prompt, x_sample, meta_tokens, rel_bias, final_norm, norm_mix, w_in, q_norm, k_norm, sink,
           w_branch_a, w_branch_b, w_out, norm_ffn, w_router_g, b_router_g, w_router_e, b_router_e,
           w_gate, w_up, w_down):
    assert norm_mix.shape[0] == 1, "single-layer encoder"
    rot_cols = (A_HEADS + A_KV_HEADS) * HEAD_DIM
    w_in_b = w_in[0].astype(BF16)
    w_in_b = lax.dynamic_update_slice(w_in_b, _swap_rotary_sections(w_in_b[:, :rot_cols]), (0, 0))
    norm_mix2 = norm_mix[0].reshape(1, D_MODEL)
    q_norm2 = _gain_rows(q_norm[0])
    k_norm2 = _gain_rows(k_norm[0])
    col = jnp.arange(IN_COLS)
    is_qb = (col >= COL_QB * HEAD_DIM) & (col < COL_KB * HEAD_DIM)
    col_scale = jnp.where(is_qb, SCORE_SCALE * LOG2E, 1.0).astype(F32).reshape(1, IN_COLS)

    cos_m, sin_m = _rope_tables(jnp.full((N_META,), -1), jnp.arange(N_META))
    proj_m = _in_proj(meta_tokens, norm_mix2, w_in_b, col_scale, cos_m, sin_m, q_norm2, k_norm2, tm=N_META)

    def meta_heads(c0):
        blk = proj_m[:, c0 * HEAD_DIM:(c0 + A_KV_HEADS) * HEAD_DIM]
        return blk.reshape(N_META, A_KV_HEADS, HEAD_DIM).transpose(1, 0, 2)

    meta_kv = tuple(meta_heads(c) for c in (COL_KA, COL_VA, COL_KB, COL_VB))

    tab, tabm = _bias_tables(rel_bias)
    sink_rows = jnp.repeat(sink[0].astype(F32), Q_BLOCK).reshape(B_KV_HEADS, REP * Q_BLOCK, 1)

    wr_full = jnp.zeros((D_MODEL, ROUTER_COLS), F32)
    wr_full = wr_full.at[:, :N_GROUPS].set(w_router_g[0]).at[:, N_GROUPS:N_GROUPS + N_EXPERTS].set(w_router_e[0])
    wr_hi, wr_lo = _split_bf16(wr_full)
    wr = jnp.concatenate([wr_hi, wr_lo], axis=1)
    br = jnp.zeros((1, ROUTER_COLS), F32)
    br = br.at[0, :N_GROUPS].set(b_router_g[0]).at[0, N_GROUPS:N_GROUPS + N_EXPERTS].set(b_router_e[0])

    shared = (norm_mix2, w_in_b, col_scale, q_norm2, k_norm2, meta_kv, tab, tabm, sink_rows,
              w_branch_a[0].astype(BF16), w_branch_b[0].astype(BF16), w_out[0].astype(BF16),
              norm_ffn[0].reshape(1, D_MODEL), wr, br,
              w_gate[0].astype(BF16), w_up[0].astype(BF16), w_down[0].astype(BF16),
              final_norm.reshape(1, D_MODEL))
    return (_encode_group(x_prompt, shared), _encode_group(x_sample, shared))
```

```python
import functools
import math

import numpy as np
import jax
import jax.numpy as jnp
from jax import lax
from jax.experimental import pallas as pl
from jax.experimental.pallas import tpu as pltpu

F32 = jnp.float32
BF16 = jnp.bfloat16

D_MODEL = 2048
HEAD_DIM = 128
A_HEADS = 8
A_KV_HEADS = 2
B_HEADS = 8
B_KV_HEADS = 2
REP = A_HEADS // A_KV_HEADS
A_WIDTH = A_HEADS * HEAD_DIM
B_WIDTH = B_HEADS * HEAD_DIM
KV_WIDTH = A_KV_HEADS * HEAD_DIM
IN_COLS = A_WIDTH + 2 * KV_WIDTH + B_WIDTH + 2 * KV_WIDTH + 2 * D_MODEL
Q_BLOCK = 128
WINDOW = 128
N_META = 16
GRID_W = 64
ROPE_THETA = 10000.0
N_BUCKETS = 32
MAX_DISTANCE = 128
N_GROUPS = 4
EXPERTS_PER_GROUP = 4
N_EXPERTS = N_GROUPS * EXPERTS_PER_GROUP
EXPERT_FF = 1024
RMS_EPS = 1e-6
NEG_INF = -1e30
SCORE_SCALE = HEAD_DIM ** -0.5
LOG2E = math.log2(math.e)

COL_QA = 0
COL_KA = A_WIDTH // HEAD_DIM
COL_VA = COL_KA + A_KV_HEADS
COL_QB = COL_VA + A_KV_HEADS
COL_KB = COL_QB + B_HEADS
COL_VB = COL_KB + B_KV_HEADS
COL_GA = COL_VB + B_KV_HEADS
COL_GB = COL_GA + D_MODEL // HEAD_DIM

LANES = 128
VMEM_LIMIT = 56 * 1024 * 1024
MOE_TILE = 512
ROUTER_COLS = 128

_NT = (((1,), (1,)), ((), ()))


def _params(sem, vmem=VMEM_LIMIT):
    return pltpu.CompilerParams(dimension_semantics=sem, vmem_limit_bytes=vmem)


def _const_spec(shape):
    nd = len(shape)
    return pl.BlockSpec(shape, lambda *_: (0,) * nd, pipeline_mode=pl.Buffered(1))


def _swap_rotary_sections(w):
    lead = w.shape[:-1]
    heads = w.shape[-1] // HEAD_DIM
    nd = len(lead)
    w5 = w.reshape(lead + (heads, 2, 2, HEAD_DIM // 4))
    return jnp.swapaxes(w5, nd + 1, nd + 2).reshape(w.shape)


def _norm_rope(a, cg, sg):
    ones = jnp.ones((HEAD_DIM, HEAD_DIM), BF16)
    ssq = jnp.dot((a * a).astype(BF16), ones, preferred_element_type=F32)
    rinv = lax.rsqrt(ssq * (1.0 / HEAD_DIM) + RMS_EPS)
    return rinv * (a * cg + pltpu.roll(a, HEAD_DIM // 2, 1) * sg)


def _in_proj_kernel(x_ref, g_ref, w_ref, cs_ref, cos_ref, sin_ref, qn_ref, kn_ref, o_ref, u_ref, *, sub):
    j = pl.program_id(1)
    row_tiles = [slice(k * sub, (k + 1) * sub) for k in range(x_ref.shape[0] // sub)]

    def project(r):
        return jnp.dot(u_ref[r, :], w_ref[...], preferred_element_type=F32) * cs_ref[...]

    def rope_factors(gain_ref, scale, r):
        return cos_ref[r, :] * (gain_ref[0:1, :] * scale), sin_ref[r, :] * (gain_ref[1:2, :] * scale)

    @pl.when(j == 0)
    def _():
        rot = (A_HEADS + A_KV_HEADS) * HEAD_DIM
        for r in row_tiles:
            x = x_ref[r, :]
            ms = jnp.mean(x * x, axis=-1, keepdims=True)
            u_ref[r, :] = (x * lax.rsqrt(ms + RMS_EPS) * g_ref[...]).astype(BF16)
            acc = project(r)
            q_factors = rope_factors(qn_ref, SCORE_SCALE * LOG2E, r)
            k_factors = rope_factors(kn_ref, 1.0, r)
            for h in range(A_HEADS + A_KV_HEADS):
                sl = slice(h * HEAD_DIM, (h + 1) * HEAD_DIM)
                cg, sg = q_factors if h < A_HEADS else k_factors
                o_ref[r, sl] = _norm_rope(acc[:, sl], cg, sg).astype(BF16)
            o_ref[r, rot:] = acc[:, rot:].astype(BF16)

    @pl.when(j >= 1)
    def _():
        for r in row_tiles:
            o_ref[r, :] = project(r).astype(BF16)


IN_PROJ_TN = IN_COLS // 4


def _in_proj(x2d, norm_g, w_bf16, col_scale, cos, sin, q_norm, k_norm, *, tm):
    t_rows = x2d.shape[0]
    tn = IN_PROJ_TN
    assert tn % HEAD_DIM == 0 and tn >= (A_HEADS + A_KV_HEADS) * HEAD_DIM
    pos_tiles = cos.shape[0] // tm
    return pl.pallas_call(
        functools.partial(_in_proj_kernel, sub=min(tm, 512)),
        grid=(t_rows // tm, IN_COLS // tn),
        in_specs=[
            pl.BlockSpec((tm, D_MODEL), lambda i, j: (i, 0)),
            pl.BlockSpec((1, D_MODEL), lambda i, j: (0, 0)),
            pl.BlockSpec((D_MODEL, tn), lambda i, j: (0, j)),
            pl.BlockSpec((1, tn), lambda i, j: (0, j)),
            pl.BlockSpec((tm, HEAD_DIM), lambda i, j: (i % pos_tiles, 0)),
            pl.BlockSpec((tm, HEAD_DIM), lambda i, j: (i % pos_tiles, 0)),
            pl.BlockSpec((2, HEAD_DIM), lambda i, j: (0, 0)),
            pl.BlockSpec((2, HEAD_DIM), lambda i, j: (0, 0)),
        ],
        out_specs=pl.BlockSpec((tm, tn), lambda i, j: (i, j)),
        out_shape=jax.ShapeDtypeStruct((t_rows, IN_COLS), BF16),
        scratch_shapes=[pltpu.VMEM((tm, D_MODEL), BF16)],
        compiler_params=_params(("parallel", "arbitrary")),
        name="in_proj",
    )(x2d, norm_g, w_bf16, col_scale, cos, sin, q_norm, k_norm)


def _stack_heads(q):
    return jnp.concatenate([q[:, r * HEAD_DIM:(r + 1) * HEAD_DIM] for r in range(REP)], axis=0)


def _global_attn_kernel(q_ref, k_ref, v_ref, km_ref, vm_ref, o_ref, vx_ref, vmx_ref, *, tq, ck):
    @pl.when(pl.program_id(2) == 0)
    def _():
        vx_ref[:, :HEAD_DIM] = v_ref[0]
        vx_ref[:, HEAD_DIM:] = jnp.ones((v_ref.shape[1], HEAD_DIM), BF16)
        vmx_ref[:, :HEAD_DIM] = vm_ref[0]
        vmx_ref[:, HEAD_DIM:] = jnp.ones((N_META, HEAD_DIM), BF16)

    qs = _stack_heads(q_ref[0])
    n_chunks = k_ref.shape[1] // ck

    def scores(c):
        return lax.dot_general(qs, k_ref[0, c * ck:(c + 1) * ck, :], _NT, preferred_element_type=F32)

    s = lax.dot_general(qs, km_ref[0], _NT, preferred_element_type=F32)
    s_next = scores(0)
    m = jnp.max(s, axis=-1, keepdims=True)
    acc = jnp.dot(jnp.exp2(s - m).astype(BF16), vmx_ref[...], preferred_element_type=F32)
    for c in range(n_chunks):
        s = s_next
        if c + 1 < n_chunks:
            s_next = scores(c + 1)
        m_new = jnp.maximum(m, jnp.max(s, axis=-1, keepdims=True))
        p = jnp.exp2(s - m_new).astype(BF16)
        acc = jnp.exp2(m - m_new) * acc + jnp.dot(p, vx_ref[c * ck:(c + 1) * ck, :],
                                                  preferred_element_type=F32)
        m = m_new
    o = acc[:, :HEAD_DIM] / acc[:, HEAD_DIM:]
    for r in range(REP):
        o_ref[0, :, r * HEAD_DIM:(r + 1) * HEAD_DIM] = o[r * tq:(r + 1) * tq].astype(BF16)


def _global_attn(proj, km, vm, *, tq, ck):
    b, s, _ = proj.shape
    gw = REP * HEAD_DIM
    return pl.pallas_call(
        functools.partial(_global_attn_kernel, tq=tq, ck=ck),
        grid=(b, A_KV_HEADS, s // tq),
        in_specs=[
            pl.BlockSpec((1, tq, gw), lambda bi, g, i: (bi, i, g)),
            pl.BlockSpec((1, s, HEAD_DIM), lambda bi, g, i: (bi, 0, COL_KA + g)),
            pl.BlockSpec((1, s, HEAD_DIM), lambda bi, g, i: (bi, 0, COL_VA + g)),
            pl.BlockSpec((1, N_META, HEAD_DIM), lambda bi, g, i: (g, 0, 0)),
            pl.BlockSpec((1, N_META, HEAD_DIM), lambda bi, g, i: (g, 0, 0)),
        ],
        out_specs=pl.BlockSpec((1, tq, gw), lambda bi, g, i: (bi, i, g)),
        out_shape=jax.ShapeDtypeStruct((b, s, A_WIDTH), BF16),
        scratch_shapes=[pltpu.VMEM((s, 2 * HEAD_DIM), BF16), pltpu.VMEM((N_META, 2 * HEAD_DIM), BF16)],
        compiler_params=_params(("parallel", "parallel", "arbitrary")),
        name="global_attn",
    )(proj, proj, proj, km, vm)


def _t5_bucket_np(rel):
    nb = N_BUCKETS // 2
    max_exact = nb // 2
    bucket = np.where(rel > 0, nb, 0)
    n = np.abs(rel)
    nf = np.maximum(n, 1).astype(np.float32)
    large = max_exact + (np.log(nf / np.float32(max_exact)) / np.float32(math.log(MAX_DISTANCE / max_exact))
                         * np.float32(nb - max_exact)).astype(np.int32)
    large = np.minimum(large, nb - 1)
    return (bucket + np.where(n < max_exact, n, large)).astype(np.int32)


def _bucket_maps():
    i = np.arange(Q_BLOCK)[:, None]
    j = np.arange(3 * Q_BLOCK)[None, :]
    real = []
    for off in range(3):
        rel = j - off * Q_BLOCK - i
        real.append(np.where(np.abs(rel) <= WINDOW, _t5_bucket_np(rel), -1))
    m = np.arange(N_META)[None, :]
    first = _t5_bucket_np(m - (N_META + i))
    later = _t5_bucket_np(m - (N_META + i + Q_BLOCK))
    meta = [first, later, later]
    return np.stack(real).astype(np.int32), np.stack(meta).astype(np.int32)


def _bias_table_kernel(rb_ref, bm_ref, bmm_ref, o_ref, om_ref):
    h = pl.program_id(1)
    bm = bm_ref[0]
    bmm = bmm_ref[0]
    acc = jnp.full(bm.shape, NEG_INF, F32)
    accm = jnp.full(bmm.shape, NEG_INF, F32)
    for k in range(N_BUCKETS):
        val = rb_ref[k, h] * LOG2E
        acc = jnp.where(bm == k, val, acc)
        accm = jnp.where(bmm == k, val, accm)
    o_ref[0, 0] = acc
    om_ref[0, 0] = accm


def _bias_tables(rel_bias):
    bm, bmm = _bucket_maps()
    kw = 3 * Q_BLOCK
    tab, tabm = pl.pallas_call(
        _bias_table_kernel,
        grid=(3, B_HEADS),
        in_specs=[
            pl.BlockSpec(memory_space=pltpu.SMEM),
            pl.BlockSpec((1, Q_BLOCK, kw), lambda v, h: (v, 0, 0)),
            pl.BlockSpec((1, Q_BLOCK, N_META), lambda v, h: (v, 0, 0)),
        ],
        out_specs=[
            pl.BlockSpec((1, 1, Q_BLOCK, kw), lambda v, h: (v, h, 0, 0)),
            pl.BlockSpec((1, 1, Q_BLOCK, N_META), lambda v, h: (v, h, 0, 0)),
        ],
        out_shape=[
            jax.ShapeDtypeStruct((3, B_HEADS, Q_BLOCK, kw), F32),
            jax.ShapeDtypeStruct((3, B_HEADS, Q_BLOCK, N_META), F32),
        ],
        compiler_params=_params(("arbitrary", "arbitrary")),
        name="bias_tables",
    )(rel_bias, jnp.asarray(bm), jnp.asarray(bmm))
    rows = REP * Q_BLOCK
    return (tab.reshape(3, B_KV_HEADS, rows, kw), tabm.reshape(3, B_KV_HEADS, rows, N_META))


def _window_attn_kernel(q_ref, k_ref, v_ref, km_ref, vm_ref, tab_ref, tabm_ref, sink_ref, o_ref, *, nq):
    jb = pl.program_id(2)
    nblk = k_ref.shape[1] // Q_BLOCK
    km = km_ref[0]
    vm = vm_ref[0]
    sink = sink_ref[0] * LOG2E
    for t in range(nq):
        n = jb * nq + t
        qs = _stack_heads(q_ref[0, t * Q_BLOCK:(t + 1) * Q_BLOCK, :])
        var = jnp.where(n == 0, 0, jnp.where(n == nblk - 1, 2, 1))
        start = pl.multiple_of(jnp.clip(n - 1, 0, nblk - 3) * Q_BLOCK, Q_BLOCK)
        kb = k_ref[0, pl.ds(start, 3 * Q_BLOCK), :]
        vb = v_ref[0, pl.ds(start, 3 * Q_BLOCK), :]
        s = lax.dot_general(qs, kb, _NT, preferred_element_type=F32) + tab_ref[var, 0]
        sm = lax.dot_general(qs, km, _NT, preferred_element_type=F32) + tabm_ref[var, 0]
        m = jnp.maximum(jnp.maximum(jnp.max(s, axis=-1, keepdims=True),
                                    jnp.max(sm, axis=-1, keepdims=True)), sink)
        p = jnp.exp2(s - m)
        pm = jnp.exp2(sm - m)
        l = jnp.sum(p, axis=-1, keepdims=True) + jnp.sum(pm, axis=-1, keepdims=True) + jnp.exp2(sink - m)
        o = (jnp.dot(p.astype(BF16), vb, preferred_element_type=F32)
             + jnp.dot(pm.astype(BF16), vm, preferred_element_type=F32)) / l
        for r in range(REP):
            o_ref[0, t * Q_BLOCK:(t + 1) * Q_BLOCK, r * HEAD_DIM:(r + 1) * HEAD_DIM] = (
                o[r * Q_BLOCK:(r + 1) * Q_BLOCK].astype(BF16))


def _window_attn(proj, km, vm, tab, tabm, sink_rows, *, nq):
    b, s, _ = proj.shape
    assert s // Q_BLOCK >= 3 and (s // Q_BLOCK) % nq == 0
    gw = REP * HEAD_DIM
    rows = REP * Q_BLOCK
    kw = 3 * Q_BLOCK
    tq = nq * Q_BLOCK
    return pl.pallas_call(
        functools.partial(_window_attn_kernel, nq=nq),
        grid=(b, B_KV_HEADS, s // tq),
        in_specs=[
            pl.BlockSpec((1, tq, gw), lambda bi, g, i: (bi, i, COL_QB // REP + g)),
            pl.BlockSpec((1, s, HEAD_DIM), lambda bi, g, i: (bi, 0, COL_KB + g)),
            pl.BlockSpec((1, s, HEAD_DIM), lambda bi, g, i: (bi, 0, COL_VB + g)),
            pl.BlockSpec((1, N_META, HEAD_DIM), lambda bi, g, i: (g, 0, 0)),
            pl.BlockSpec((1, N_META, HEAD_DIM), lambda bi, g, i: (g, 0, 0)),
            pl.BlockSpec((3, 1, rows, kw), lambda bi, g, i: (0, g, 0, 0)),
            pl.BlockSpec((3, 1, rows, N_META), lambda bi, g, i: (0, g, 0, 0)),
            pl.BlockSpec((1, rows, 1), lambda bi, g, i: (g, 0, 0)),
        ],
        out_specs=pl.BlockSpec((1, tq, gw), lambda bi, g, i: (bi, i, g)),
        out_shape=jax.ShapeDtypeStruct((b, s, B_WIDTH), BF16),
        compiler_params=_params(("parallel", "parallel", "arbitrary")),
        name="window_attn",
    )(proj, proj, proj, km, vm, tab, tabm, sink_rows)


def _sigmoid(x):
    return 1.0 / (1.0 + jnp.exp(-x))


def _route(logits):
    lane = lax.broadcasted_iota(jnp.int32, logits.shape, 1).astype(F32)
    ninf = jnp.float32(-jnp.inf)
    big = jnp.float32(ROUTER_COLS)
    is_g = lane < N_GROUPS
    lg = jnp.where(is_g, logits, ninf)
    mg = jnp.max(lg, axis=-1, keepdims=True)
    gidx = jnp.min(jnp.where(lg == mg, lane, big), axis=-1, keepdims=True)
    p_top = 1.0 / jnp.sum(jnp.where(is_g, jnp.exp(lg - mg), 0.0), axis=-1, keepdims=True)
    lo = N_GROUPS + EXPERTS_PER_GROUP * gidx
    sel = (lane >= lo) & (lane < lo + EXPERTS_PER_GROUP)
    le = jnp.where(sel, logits, ninf)
    v1 = jnp.max(le, axis=-1, keepdims=True)
    i1 = jnp.min(jnp.where(sel & (le == v1), lane, big), axis=-1, keepdims=True)
    rest = sel & (lane != i1)
    le2 = jnp.where(rest, logits, ninf)
    v2 = jnp.max(le2, axis=-1, keepdims=True)
    i2 = jnp.min(jnp.where(rest & (le2 == v2), lane, big), axis=-1, keepdims=True)
    t = jnp.exp(v2 - v1)
    w1 = p_top / (1.0 + t)
    w2 = p_top * t / (1.0 + t)
    return i1 - N_GROUPS, i2 - N_GROUPS, w1, w2


def _merge_kernel(oa_ref, ob_ref, ga0_ref, ga1_ref, gb0_ref, gb1_ref, x_ref, wa_ref, wb_ref, wo_ref, gn_ref,
                  wr_ref, br_ref, h2_ref, u2_ref, logits_ref, *, sub):
    for st in range(oa_ref.shape[0] // sub):
        r = slice(st * sub, (st + 1) * sub)
        ya = jnp.dot(oa_ref[r, :], wa_ref[...], preferred_element_type=F32)
        yb = jnp.dot(ob_ref[r, :], wb_ref[...], preferred_element_type=F32)
        ga = jnp.concatenate([ga0_ref[r, :], ga1_ref[r, :]], axis=1).astype(F32)
        gb = jnp.concatenate([gb0_ref[r, :], gb1_ref[r, :]], axis=1).astype(F32)
        mixed = _sigmoid(ga) * ya + _sigmoid(gb) * yb
        h2 = x_ref[r, :] + jnp.dot(mixed.astype(BF16), wo_ref[...], preferred_element_type=F32)
        h2_ref[r, :] = h2
        ms = jnp.mean(h2 * h2, axis=-1, keepdims=True)
        u = h2 * lax.rsqrt(ms + RMS_EPS) * gn_ref[...]
        u_hi = u.astype(BF16)
        u_hi32 = u_hi.astype(F32)
        u_lo = (u - u_hi32).astype(BF16)
        lg = (jnp.dot(u_hi, wr_ref[...], preferred_element_type=F32)
              + jnp.dot(u_lo, wr_ref[...], preferred_element_type=F32))
        logits_ref[r, :] = lg[:, :ROUTER_COLS] + lg[:, ROUTER_COLS:] + br_ref[...]
        half = D_MODEL // 2
        lo_bits = pltpu.bitcast(u_hi32[:, :half], jnp.uint32) >> 16
        hi_bits = pltpu.bitcast(u_hi32[:, half:], jnp.uint32)
        u2_ref[r, :] = hi_bits | lo_bits


def _merge_route(oa, ob, proj2d, x2d, wa, wb, wo, norm_ffn, wr, br, *, tm):
    t_rows = x2d.shape[0]
    gw = D_MODEL // 2
    ga_blk = COL_GA * HEAD_DIM // gw
    gb_blk = COL_GB * HEAD_DIM // gw
    return pl.pallas_call(
        functools.partial(_merge_kernel, sub=min(tm, 256)),
        grid=(t_rows // tm,),
        in_specs=[
            pl.BlockSpec((tm, A_WIDTH), lambda i: (i, 0)),
            pl.BlockSpec((tm, B_WIDTH), lambda i: (i, 0)),
            pl.BlockSpec((tm, gw), lambda i: (i, ga_blk)),
            pl.BlockSpec((tm, gw), lambda i: (i, ga_blk + 1)),
            pl.BlockSpec((tm, gw), lambda i: (i, gb_blk)),
            pl.BlockSpec((tm, gw), lambda i: (i, gb_blk + 1)),
            pl.BlockSpec((tm, D_MODEL), lambda i: (i, 0)),
            _const_spec((A_WIDTH, D_MODEL)),
            _const_spec((B_WIDTH, D_MODEL)),
            _const_spec((D_MODEL, D_MODEL)),
            _const_spec((1, D_MODEL)),
            _const_spec((D_MODEL, 2 * ROUTER_COLS)),
            _const_spec((1, ROUTER_COLS)),
        ],
        out_specs=[
            pl.BlockSpec((tm, D_MODEL), lambda i: (i, 0)),
            pl.BlockSpec((tm, D_MODEL // 2), lambda i: (i, 0)),
            pl.BlockSpec((tm, ROUTER_COLS), lambda i: (i, 0)),
        ],
        out_shape=[
            jax.ShapeDtypeStruct((t_rows, D_MODEL), F32),
            jax.ShapeDtypeStruct((t_rows, D_MODEL // 2), jnp.uint32),
            jax.ShapeDtypeStruct((t_rows, ROUTER_COLS), F32),
        ],
        compiler_params=_params(("parallel",)),
        name="merge_route",
    )(oa, ob, proj2d, proj2d, proj2d, proj2d, x2d, wa, wb, wo, norm_ffn, wr, br)


def _route_kernel(logits_ref, info_ref):
    logits = logits_ref[...]
    e1, e2, w1, w2 = _route(logits)
    lane = lax.broadcasted_iota(jnp.int32, logits.shape, 1)
    info_ref[...] = jnp.where(lane == 0, e1, jnp.where(lane == 1, e2, jnp.where(lane == 2, w1,
                              jnp.where(lane == 3, w2, 0.0))))


def _route_call(logits, *, tm):
    t_rows = logits.shape[0]
    return pl.pallas_call(
        _route_kernel,
        grid=(t_rows // tm,),
        in_specs=[pl.BlockSpec((tm, ROUTER_COLS), lambda i: (i, 0))],
        out_specs=pl.BlockSpec((tm, ROUTER_COLS), lambda i: (i, 0)),
        out_shape=jax.ShapeDtypeStruct((t_rows, ROUTER_COLS), F32),
        compiler_params=_params(("parallel",)),
        name="route",
    )(logits)


SEG_NTILES = 2 * N_EXPERTS


def _positions_kernel(e1_ref, e2_ref, pos1_ref, pos2_ref, te_ref, seg_ref):
    e1 = e1_ref[...]
    e2 = e2_ref[...]
    rows = e1.shape[0]
    r_i = lax.broadcasted_iota(jnp.int32, (LANES, LANES), 0)
    c_i = lax.broadcasted_iota(jnp.int32, (LANES, LANES), 1)
    upper = (r_i < c_i).astype(BF16)
    rr = lax.broadcasted_iota(jnp.int32, (rows, rows), 0)
    rc = lax.broadcasted_iota(jnp.int32, (rows, rows), 1)
    lower = (rc < rr).astype(BF16)
    seg_lane = lax.broadcasted_iota(jnp.int32, seg_ref.shape, 1)
    tile_row = lax.broadcasted_iota(jnp.int32, te_ref.shape, 1).astype(F32) * MOE_TILE
    base = jnp.zeros((1, 1), F32)
    pos1 = jnp.zeros(e1.shape, F32)
    pos2 = jnp.zeros(e1.shape, F32)
    seg = jnp.zeros(seg_ref.shape, F32)
    tile_expert = jnp.zeros(te_ref.shape, F32)
    for e in range(N_EXPERTS):
        m1 = e1 == e
        m2 = e2 == e
        m = jnp.where(m1 | m2, 1.0, 0.0)
        lane_pre = jnp.dot(m.astype(BF16), upper, preferred_element_type=F32)
        row_tot = jnp.broadcast_to(jnp.sum(m, axis=-1, keepdims=True), m.shape)
        row_pre = jnp.dot(lower, row_tot.astype(BF16), preferred_element_type=F32)
        total = jnp.sum(row_tot[:, 0:1], axis=0, keepdims=True)
        p = base + row_pre + lane_pre
        pos1 = jnp.where(m1, p, pos1)
        pos2 = jnp.where(m2, p, pos2)
        padded = jnp.floor((total + (MOE_TILE - 1)) * (1.0 / MOE_TILE)) * MOE_TILE
        base = base + padded
        seg = seg + jnp.where(seg_lane == e, base, 0.0) + jnp.where(seg_lane == N_EXPERTS + e, total, 0.0)
        tile_expert = tile_expert + jnp.where(tile_row >= base, 1.0, 0.0)
    seg = seg + jnp.where(seg_lane == SEG_NTILES, base * (1.0 / MOE_TILE), 0.0)
    pos1_ref[...] = pos1.astype(jnp.int32)
    pos2_ref[...] = pos2.astype(jnp.int32)
    te_ref[...] = jnp.minimum(tile_expert, N_EXPERTS - 1).astype(jnp.int32)
    seg_ref[...] = seg.astype(jnp.int32)


def _positions(e1, e2, n_tiles):
    rows = e1.shape[0]
    ntp = -(-n_tiles // LANES) * LANES
    full = lambda shape: pl.BlockSpec(shape, lambda: (0,) * len(shape))
    return pl.pallas_call(
        _positions_kernel,
        in_specs=[full((rows, LANES)), full((rows, LANES))],
        out_specs=[full((rows, LANES)), full((rows, LANES)), full((1, ntp)), full((1, LANES))],
        out_shape=[
            jax.ShapeDtypeStruct((rows, LANES), jnp.int32),
            jax.ShapeDtypeStruct((rows, LANES), jnp.int32),
            jax.ShapeDtypeStruct((1, ntp), jnp.int32),
            jax.ShapeDtypeStruct((1, LANES), jnp.int32),
        ],
        compiler_params=pltpu.CompilerParams(vmem_limit_bytes=VMEM_LIMIT),
        name="positions",
    )(e1, e2)


def _dispatch_kernel(seg_ref, pos1_ref, pos2_ref, u_ref, xs_ref, z_ref, zsem, sem, *, tb):
    i = pl.program_id(0)

    def zero_copy(e):
        end = seg_ref[e]
        start = pl.multiple_of(end - MOE_TILE, MOE_TILE)
        return pltpu.make_async_copy(z_ref, xs_ref.at[pl.ds(start, MOE_TILE)], zsem)

    def nonempty(e):
        return seg_ref[e] > (seg_ref[e - 1] if e > 0 else 0)

    n_tiles = xs_ref.shape[0] // MOE_TILE

    def tail_copy(k):
        start = pl.multiple_of((seg_ref[SEG_NTILES] + k) * MOE_TILE, MOE_TILE)
        return pltpu.make_async_copy(z_ref, xs_ref.at[pl.ds(start, MOE_TILE)], zsem)

    def tail_exists(k):
        return seg_ref[SEG_NTILES] + k < n_tiles

    @pl.when(i == 0)
    def _():
        z_ref[...] = jnp.zeros(z_ref.shape, z_ref.dtype)
        for e in range(N_EXPERTS):
            @pl.when(nonempty(e))
            def _():
                zero_copy(e).start()

            @pl.when(tail_exists(e))
            def _():
                tail_copy(e).start()
        for e in range(N_EXPERTS):
            @pl.when(nonempty(e))
            def _():
                zero_copy(e).wait()

            @pl.when(tail_exists(e))
            def _():
                tail_copy(e).wait()

    def row_copy(t, pos_ref):
        return pltpu.make_async_copy(u_ref.at[pl.ds(t, 1)], xs_ref.at[pl.ds(pos_ref[t], 1)], sem)

    def issue(t, carry):
        row_copy(t, pos1_ref).start()
        row_copy(t, pos2_ref).start()
        return carry

    lax.fori_loop(0, tb, issue, 0, unroll=8)

    for _ in range(2):
        pltpu.make_async_copy(u_ref, xs_ref.at[pl.ds(0, tb)], sem).wait()


def _dispatch(seg, pos1, pos2, u2p, n_tiles, *, tb):
    t_rows, width = u2p.shape
    return pl.pallas_call(
        functools.partial(_dispatch_kernel, tb=tb),
        grid_spec=pltpu.PrefetchScalarGridSpec(
            num_scalar_prefetch=1,
            grid=(t_rows // tb,),
            in_specs=[
                pl.BlockSpec((tb,), lambda i, seg: (i,), memory_space=pltpu.SMEM),
                pl.BlockSpec((tb,), lambda i, seg: (i,), memory_space=pltpu.SMEM),
                pl.BlockSpec((tb, width), lambda i, seg: (i, 0)),
            ],
            out_specs=pl.BlockSpec(memory_space=pl.ANY),
            scratch_shapes=[
                pltpu.VMEM((MOE_TILE, width), u2p.dtype),
                pltpu.SemaphoreType.DMA(()),
                pltpu.SemaphoreType.DMA(()),
            ],
        ),
        out_shape=jax.ShapeDtypeStruct((n_tiles * MOE_TILE, width), u2p.dtype),
        compiler_params=_params(("arbitrary",)),
        name="dispatch",
    )(seg, pos1, pos2, u2p)


def _moe_kernel(te_ref, seg_ref, x_ref, wg_ref, wu_ref, wd_ref, y_ref):
    i = pl.program_id(0)

    @pl.when(i < seg_ref[SEG_NTILES])
    def _():
        xw = x_ref[...]
        half = D_MODEL // 2
        lo = pltpu.bitcast(xw << 16, F32).astype(BF16)
        hi = pltpu.bitcast(xw & jnp.uint32(0xFFFF0000), F32).astype(BF16)
        hg = (jnp.dot(lo, wg_ref[0, :half, :], preferred_element_type=F32)
              + jnp.dot(hi, wg_ref[0, half:, :], preferred_element_type=F32))
        hu = (jnp.dot(lo, wu_ref[0, :half, :], preferred_element_type=F32)
              + jnp.dot(hi, wu_ref[0, half:, :], preferred_element_type=F32))
        hid = hg * _sigmoid(hg) * hu
        y_ref[...] = jnp.dot(hid.astype(BF16), wd_ref[0], preferred_element_type=F32)

    @pl.when(i >= seg_ref[SEG_NTILES])
    def _():
        y_ref[...] = jnp.zeros(y_ref.shape, y_ref.dtype)


def _moe(tile_expert, seg, xs, wg, wu, wd):
    n_tiles = xs.shape[0] // MOE_TILE

    def row_map(i, te, seg):
        return (jnp.minimum(i, seg[SEG_NTILES] - 1), 0)

    def w_map(i, te, seg):
        return (te[jnp.minimum(i, seg[SEG_NTILES] - 1)], 0, 0)

    return pl.pallas_call(
        _moe_kernel,
        grid_spec=pltpu.PrefetchScalarGridSpec(
            num_scalar_prefetch=2,
            grid=(n_tiles,),
            in_specs=[
                pl.BlockSpec((MOE_TILE, D_MODEL // 2), row_map),
                pl.BlockSpec((1, D_MODEL, EXPERT_FF), w_map),
                pl.BlockSpec((1, D_MODEL, EXPERT_FF), w_map),
                pl.BlockSpec((1, EXPERT_FF, D_MODEL), w_map),
            ],
            out_specs=pl.BlockSpec((MOE_TILE, D_MODEL), lambda i, te, seg: (i, 0)),
        ),
        out_shape=jax.ShapeDtypeStruct((n_tiles * MOE_TILE, D_MODEL), F32),
        compiler_params=_params(("arbitrary",)),
        name="moe",
    )(tile_expert, seg, xs, wg, wu, wd)


def _final_kernel(pos1_ref, pos2_ref, nxt1_ref, nxt2_ref, h2_ref, info_ref, g_ref, ys_ref, o_ref, ybuf, sems, *, tm):
    i = pl.program_id(0)
    n = pl.num_programs(0)
    slot = i % 2

    def gather(p1_ref, p2_ref, s):
        def issue(t, carry):
            pltpu.make_async_copy(ys_ref.at[pl.ds(p1_ref[t], 1)], ybuf.at[s, 0, pl.ds(t, 1)], sems.at[s]).start()
            pltpu.make_async_copy(ys_ref.at[pl.ds(p2_ref[t], 1)], ybuf.at[s, 1, pl.ds(t, 1)], sems.at[s]).start()
            return carry

        lax.fori_loop(0, tm, issue, 0, unroll=8)

    @pl.when(i == 0)
    def _():
        gather(pos1_ref, pos2_ref, 0)

    @pl.when(i + 1 < n)
    def _():
        gather(nxt1_ref, nxt2_ref, 1 - slot)

    for k in range(2):
        pltpu.make_async_copy(ys_ref.at[pl.ds(0, tm)], ybuf.at[slot, k], sems.at[slot]).wait()

    info = info_ref[...]
    h = h2_ref[...] + info[:, 2:3] * ybuf[slot, 0] + info[:, 3:4] * ybuf[slot, 1]
    ms = jnp.mean(h * h, axis=-1, keepdims=True)
    o_ref[...] = h * lax.rsqrt(ms + RMS_EPS) * g_ref[...]


def _final(pos1, pos2, h2, info, final_norm, ys, *, tm):
    t_rows = h2.shape[0]
    n_steps = t_rows // tm
    nxt = lambda i: (jnp.minimum(i + 1, n_steps - 1),)
    return pl.pallas_call(
        functools.partial(_final_kernel, tm=tm),
        grid=(n_steps,),
        in_specs=[
            pl.BlockSpec((tm,), lambda i: (i,), memory_space=pltpu.SMEM),
            pl.BlockSpec((tm,), lambda i: (i,), memory_space=pltpu.SMEM),
            pl.BlockSpec((tm,), nxt, memory_space=pltpu.SMEM),
            pl.BlockSpec((tm,), nxt, memory_space=pltpu.SMEM),
            pl.BlockSpec((tm, D_MODEL), lambda i: (i, 0)),
            pl.BlockSpec((tm, ROUTER_COLS), lambda i: (i, 0)),
            pl.BlockSpec((1, D_MODEL), lambda i: (0, 0)),
            pl.BlockSpec(memory_space=pl.ANY),
        ],
        out_specs=pl.BlockSpec((tm, D_MODEL), lambda i: (i, 0)),
        out_shape=jax.ShapeDtypeStruct((t_rows, D_MODEL), F32),
        scratch_shapes=[pltpu.VMEM((2, 2, tm, D_MODEL), F32), pltpu.SemaphoreType.DMA((2,))],
        compiler_params=_params(("arbitrary",)),
        name="final",
    )(pos1, pos2, pos1, pos2, h2, info, final_norm, ys)


def _rope_tables(row, col):
    n_freq = HEAD_DIM // 4
    inv_freq = ROPE_THETA ** (-jnp.arange(n_freq, dtype=F32) / n_freq)
    ra = row.astype(F32)[:, None] * inv_freq
    ca = col.astype(F32)[:, None] * inv_freq
    cos = jnp.concatenate([jnp.cos(ra), jnp.cos(ca), jnp.cos(ra), jnp.cos(ca)], axis=-1)
    sin = jnp.concatenate([-jnp.sin(ra), -jnp.sin(ca), jnp.sin(ra), jnp.sin(ca)], axis=-1)
    return cos, sin


def _gain_rows(g):
    gs = _swap_rotary_sections(g.astype(F32))
    return jnp.stack([gs, jnp.roll(gs, HEAD_DIM // 2)])


def _split_bf16(w):
    hi = w.astype(BF16)
    lo = (w - hi.astype(F32)).astype(BF16)
    return hi, lo


def _tile(n, pref):
    t = min(n, pref)
    assert n % t == 0, (n, pref)
    return t


def _encode_group(x, shared):
    (norm_mix, w_in, col_scale, q_norm, k_norm, meta_kv, tab, tabm, sink_rows, wa, wb, wo, norm_ffn, wr, br,
     wg, wu, wd, final_norm) = shared
    b, s, _ = x.shape
    t_rows = b * s
    x2d = x.reshape(t_rows, D_MODEL)
    tok = jnp.arange(s)
    cos, sin = _rope_tables(tok // GRID_W, tok % GRID_W)
    proj2d = _in_proj(x2d, norm_mix, w_in, col_scale, cos, sin, q_norm, k_norm, tm=_tile(s, 1024))
    proj = proj2d.reshape(b, s, IN_COLS)
    ka_m, va_m, kb_m, vb_m = meta_kv
    oa = _global_attn(proj, ka_m, va_m, tq=_tile(s, 256), ck=_tile(s, 1024))
    nblk = s // Q_BLOCK
    ob = _window_attn(proj, kb_m, vb_m, tab, tabm, sink_rows, nq=math.gcd(nblk, 8))
    h2, u2p, logits = _merge_route(oa.reshape(t_rows, A_WIDTH), ob.reshape(t_rows, B_WIDTH), proj2d, x2d,
                                   wa, wb, wo, norm_ffn, wr, br, tm=_tile(t_rows, 512))
    info = _route_call(logits, tm=_tile(t_rows, 2048))
    e1 = info[:, 0].reshape(t_rows // LANES, LANES)
    e2 = info[:, 1].reshape(t_rows // LANES, LANES)
    n_tiles = 2 * t_rows // MOE_TILE + N_EXPERTS
    pos1, pos2, tile_expert, seg = _positions(e1, e2, n_tiles)
    pos1 = pos1.reshape(t_rows)
    pos2 = pos2.reshape(t_rows)
    seg = seg.reshape(LANES)
    xs = _dispatch(seg, pos1, pos2, u2p, n_tiles, tb=_tile(t_rows, 2048))
    ys = _moe(tile_expert.reshape(-1)[:n_tiles], seg, xs, wg, wu, wd)
    out = _final(pos1, pos2, h2, info, final_norm, ys, tm=_tile(t_rows, 256))
    return out.reshape(b, s, D_MODEL)


def kernel(x_prompt, x_sample, meta_tokens, rel_bias, final_norm, norm_mix, w_in, q_norm, k_norm, sink,
           w_branch_a, w_branch_b, w_out, norm_ffn, w_router_g, b_router_g, w_router_e, b_router_e,
           w_gate, w_up, w_down):
    assert norm_mix.shape[0] == 1, "single-layer encoder"
    rot_cols = (A_HEADS + A_KV_HEADS) * HEAD_DIM
    w_in_b = w_in[0].astype(BF16)
    w_in_b = lax.dynamic_update_slice(w_in_b, _swap_rotary_sections(w_in_b[:, :rot_cols]), (0, 0))
    norm_mix2 = norm_mix[0].reshape(1, D_MODEL)
    q_norm2 = _gain_rows(q_norm[0])
    k_norm2 = _gain_rows(k_norm[0])
    col = jnp.arange(IN_COLS)
    is_qb = (col >= COL_QB * HEAD_DIM) & (col < COL_KB * HEAD_DIM)
    col_scale = jnp.where(is_qb, SCORE_SCALE * LOG2E, 1.0).astype(F32).reshape(1, IN_COLS)

    cos_m, sin_m = _rope_tables(jnp.full((N_META,), -1), jnp.arange(N_META))
    proj_m = _in_proj(meta_tokens, norm_mix2, w_in_b, col_scale, cos_m, sin_m, q_norm2, k_norm2, tm=N_META)

    def meta_heads(c0):
        blk = proj_m[:, c0 * HEAD_DIM:(c0 + A_KV_HEADS) * HEAD_DIM]
        return blk.reshape(N_META, A_KV_HEADS, HEAD_DIM).transpose(1, 0, 2)

    meta_kv = tuple(meta_heads(c) for c in (COL_KA, COL_VA, COL_KB, COL_VB))

    tab, tabm = _bias_tables(rel_bias)
    sink_rows = jnp.repeat(sink[0].astype(F32), Q_BLOCK).reshape(B_KV_HEADS, REP * Q_BLOCK, 1)

    wr_full = jnp.zeros((D_MODEL, ROUTER_COLS), F32)
    wr_full = wr_full.at[:, :N_GROUPS].set(w_router_g[0]).at[:, N_GROUPS:N_GROUPS + N_EXPERTS].set(w_router_e[0])
    wr_hi, wr_lo = _split_bf16(wr_full)
    wr = jnp.concatenate([wr_hi, wr_lo], axis=1)
    br = jnp.zeros((1, ROUTER_COLS), F32)
    br = br.at[0, :N_GROUPS].set(b_router_g[0]).at[0, N_GROUPS:N_GROUPS + N_EXPERTS].set(b_router_e[0])

    shared = (norm_mix2, w_in_b, col_scale, q_norm2, k_norm2, meta_kv, tab, tabm, sink_rows,
              w_branch_a[0].astype(BF16), w_branch_b[0].astype(BF16), w_out[0].astype(BF16),
              norm_ffn[0].reshape(1, D_MODEL), wr, br,
              w_gate[0].astype(BF16), w_up[0].astype(BF16), w_down[0].astype(BF16),
              final_norm.reshape(1, D_MODEL))
    return (_encode_group(x_prompt, shared), _encode_group(x_sample, shared))
```

```python
import functools
import math

import numpy as np
import jax
import jax.numpy as jnp
from jax import lax
from jax.experimental import pallas as pl
from jax.experimental.pallas import tpu as pltpu

F32 = jnp.float32
BF16 = jnp.bfloat16

D_MODEL = 2048
HEAD_DIM = 128
A_HEADS = 8
A_KV_HEADS = 2
B_HEADS = 8
B_KV_HEADS = 2
REP = A_HEADS // A_KV_HEADS
A_WIDTH = A_HEADS * HEAD_DIM
B_WIDTH = B_HEADS * HEAD_DIM
KV_WIDTH = A_KV_HEADS * HEAD_DIM
IN_COLS = A_WIDTH + 2 * KV_WIDTH + B_WIDTH + 2 * KV_WIDTH + 2 * D_MODEL
Q_BLOCK = 128
WINDOW = 128
N_META = 16
GRID_W = 64
ROPE_THETA = 10000.0
N_BUCKETS = 32
MAX_DISTANCE = 128
N_GROUPS = 4
EXPERTS_PER_GROUP = 4
N_EXPERTS = N_GROUPS * EXPERTS_PER_GROUP
EXPERT_FF = 1024
RMS_EPS = 1e-6
NEG_INF = -1e30
SCORE_SCALE = HEAD_DIM ** -0.5
LOG2E = math.log2(math.e)

COL_QA = 0
COL_KA = A_WIDTH // HEAD_DIM
COL_VA = COL_KA + A_KV_HEADS
COL_QB = COL_VA + A_KV_HEADS
COL_KB = COL_QB + B_HEADS
COL_VB = COL_KB + B_KV_HEADS
COL_GA = COL_VB + B_KV_HEADS
COL_GB = COL_GA + D_MODEL // HEAD_DIM

LANES = 128
VMEM_LIMIT = 56 * 1024 * 1024
MOE_TILE = 512
ROUTER_COLS = 128

_NT = (((1,), (1,)), ((), ()))


def _params(sem, vmem=VMEM_LIMIT):
    return pltpu.CompilerParams(dimension_semantics=sem, vmem_limit_bytes=vmem)


def _const_spec(shape):
    nd = len(shape)
    return pl.BlockSpec(shape, lambda *_: (0,) * nd, pipeline_mode=pl.Buffered(1))


def _swap_rotary_sections(w):
    lead = w.shape[:-1]
    heads = w.shape[-1] // HEAD_DIM
    nd = len(lead)
    w5 = w.reshape(lead + (heads, 2, 2, HEAD_DIM // 4))
    return jnp.swapaxes(w5, nd + 1, nd + 2).reshape(w.shape)


def _norm_rope(a, cg, sg):
    ones = jnp.ones((HEAD_DIM, HEAD_DIM), BF16)
    ssq = jnp.dot((a * a).astype(BF16), ones, preferred_element_type=F32)
    rinv = lax.rsqrt(ssq * (1.0 / HEAD_DIM) + RMS_EPS)
    return rinv * (a * cg + pltpu.roll(a, HEAD_DIM // 2, 1) * sg)


def _in_proj_kernel(x_ref, g_ref, w_ref, cs_ref, cos_ref, sin_ref, qn_ref, kn_ref, o_ref, u_ref, *, sub):
    j = pl.program_id(1)
    row_tiles = [slice(k * sub, (k + 1) * sub) for k in range(x_ref.shape[0] // sub)]

    def project(r):
        return jnp.dot(u_ref[r, :], w_ref[...], preferred_element_type=F32) * cs_ref[...]

    def rope_factors(gain_ref, scale, r):
        return cos_ref[r, :] * (gain_ref[0:1, :] * scale), sin_ref[r, :] * (gain_ref[1:2, :] * scale)

    @pl.when(j == 0)
    def _():
        rot = (A_HEADS + A_KV_HEADS) * HEAD_DIM
        for r in row_tiles:
            x = x_ref[r, :]
            ms = jnp.mean(x * x, axis=-1, keepdims=True)
            u_ref[r, :] = (x * lax.rsqrt(ms + RMS_EPS) * g_ref[...]).astype(BF16)
            acc = project(r)
            q_factors = rope_factors(qn_ref, SCORE_SCALE * LOG2E, r)
            k_factors = rope_factors(kn_ref, 1.0, r)
            for h in range(A_HEADS + A_KV_HEADS):
                sl = slice(h * HEAD_DIM, (h + 1) * HEAD_DIM)
                cg, sg = q_factors if h < A_HEADS else k_factors
                o_ref[r, sl] = _norm_rope(acc[:, sl], cg, sg).astype(BF16)
            o_ref[r, rot:] = acc[:, rot:].astype(BF16)

    @pl.when(j >= 1)
    def _():
        for r in row_tiles:
            o_ref[r, :] = project(r).astype(BF16)


IN_PROJ_TN = IN_COLS // 4


def _in_proj(x2d, norm_g, w_bf16, col_scale, cos, sin, q_norm, k_norm, *, tm):
    t_rows = x2d.shape[0]
    tn = IN_PROJ_TN
    assert tn % HEAD_DIM == 0 and tn >= (A_HEADS + A_KV_HEADS) * HEAD_DIM
    pos_tiles = cos.shape[0] // tm
    return pl.pallas_call(
        functools.partial(_in_proj_kernel, sub=min(tm, 512)),
        grid=(t_rows // tm, IN_COLS // tn),
        in_specs=[
            pl.BlockSpec((tm, D_MODEL), lambda i, j: (i, 0)),
            pl.BlockSpec((1, D_MODEL), lambda i, j: (0, 0)),
            pl.BlockSpec((D_MODEL, tn), lambda i, j: (0, j)),
            pl.BlockSpec((1, tn), lambda i, j: (0, j)),
            pl.BlockSpec((tm, HEAD_DIM), lambda i, j: (i % pos_tiles, 0)),
            pl.BlockSpec((tm, HEAD_DIM), lambda i, j: (i % pos_tiles, 0)),
            pl.BlockSpec((2, HEAD_DIM), lambda i, j: (0, 0)),
            pl.BlockSpec((2, HEAD_DIM), lambda i, j: (0, 0)),
        ],
        out_specs=pl.BlockSpec((tm, tn), lambda i, j: (i, j)),
        out_shape=jax.ShapeDtypeStruct((t_rows, IN_COLS), BF16),
        scratch_shapes=[pltpu.VMEM((tm, D_MODEL), BF16)],
        compiler_params=_params(("parallel", "arbitrary")),
        name="in_proj",
    )(x2d, norm_g, w_bf16, col_scale, cos, sin, q_norm, k_norm)


def _stack_heads(q):
    return jnp.concatenate([q[:, r * HEAD_DIM:(r + 1) * HEAD_DIM] for r in range(REP)], axis=0)


def _global_attn_kernel(q_ref, k_ref, v_ref, km_ref, vm_ref, o_ref, vx_ref, vmx_ref, *, tq, ck):
    @pl.when(pl.program_id(2) == 0)
    def _():
        vx_ref[:, :HEAD_DIM] = v_ref[0]
        vx_ref[:, HEAD_DIM:] = jnp.ones((v_ref.shape[1], HEAD_DIM), BF16)
        vmx_ref[:, :HEAD_DIM] = vm_ref[0]
        vmx_ref[:, HEAD_DIM:] = jnp.ones((N_META, HEAD_DIM), BF16)

    qs = _stack_heads(q_ref[0])
    n_chunks = k_ref.shape[1] // ck

    def scores(c):
        return lax.dot_general(qs, k_ref[0, c * ck:(c + 1) * ck, :], _NT, preferred_element_type=F32)

    s = lax.dot_general(qs, km_ref[0], _NT, preferred_element_type=F32)
    s_next = scores(0)
    m = jnp.max(s, axis=-1, keepdims=True)
    acc = jnp.dot(jnp.exp2(s - m).astype(BF16), vmx_ref[...], preferred_element_type=F32)
    for c in range(n_chunks):
        s = s_next
        if c + 1 < n_chunks:
            s_next = scores(c + 1)
        m_new = jnp.maximum(m, jnp.max(s, axis=-1, keepdims=True))
        p = jnp.exp2(s - m_new).astype(BF16)
        acc = jnp.exp2(m - m_new) * acc + jnp.dot(p, vx_ref[c * ck:(c + 1) * ck, :],
                                                  preferred_element_type=F32)
        m = m_new
    o = acc[:, :HEAD_DIM] / acc[:, HEAD_DIM:]
    for r in range(REP):
        o_ref[0, :, r * HEAD_DIM:(r + 1) * HEAD_DIM] = o[r * tq:(r + 1) * tq].astype(BF16)


def _global_attn(proj, km, vm, *, tq, ck):
    b, s, _ = proj.shape
    gw = REP * HEAD_DIM
    return pl.pallas_call(
        functools.partial(_global_attn_kernel, tq=tq, ck=ck),
        grid=(b, A_KV_HEADS, s // tq),
        in_specs=[
            pl.BlockSpec((1, tq, gw), lambda bi, g, i: (bi, i, g)),
            pl.BlockSpec((1, s, HEAD_DIM), lambda bi, g, i: (bi, 0, COL_KA + g)),
            pl.BlockSpec((1, s, HEAD_DIM), lambda bi, g, i: (bi, 0, COL_VA + g)),
            pl.BlockSpec((1, N_META, HEAD_DIM), lambda bi, g, i: (g, 0, 0)),
            pl.BlockSpec((1, N_META, HEAD_DIM), lambda bi, g, i: (g, 0, 0)),
        ],
        out_specs=pl.BlockSpec((1, tq, gw), lambda bi, g, i: (bi, i, g)),
        out_shape=jax.ShapeDtypeStruct((b, s, A_WIDTH), BF16),
        scratch_shapes=[pltpu.VMEM((s, 2 * HEAD_DIM), BF16), pltpu.VMEM((N_META, 2 * HEAD_DIM), BF16)],
        compiler_params=_params(("parallel", "parallel", "arbitrary")),
        name="global_attn",
    )(proj, proj, proj, km, vm)


def _t5_bucket_np(rel):
    nb = N_BUCKETS // 2
    max_exact = nb // 2
    bucket = np.where(rel > 0, nb, 0)
    n = np.abs(rel)
    nf = np.maximum(n, 1).astype(np.float32)
    large = max_exact + (np.log(nf / np.float32(max_exact)) / np.float32(math.log(MAX_DISTANCE / max_exact))
                         * np.float32(nb - max_exact)).astype(np.int32)
    large = np.minimum(large, nb - 1)
    return (bucket + np.where(n < max_exact, n, large)).astype(np.int32)


def _bucket_maps():
    i = np.arange(Q_BLOCK)[:, None]
    j = np.arange(3 * Q_BLOCK)[None, :]
    real = []
    for off in range(3):
        rel = j - off * Q_BLOCK - i
        real.append(np.where(np.abs(rel) <= WINDOW, _t5_bucket_np(rel), -1))
    m = np.arange(N_META)[None, :]
    first = _t5_bucket_np(m - (N_META + i))
    later = _t5_bucket_np(m - (N_META + i + Q_BLOCK))
    meta = [first, later, later]
    return np.stack(real).astype(np.int32), np.stack(meta).astype(np.int32)


def _bias_table_kernel(rb_ref, bm_ref, bmm_ref, o_ref, om_ref):
    h = pl.program_id(1)
    bm = bm_ref[0]
    bmm = bmm_ref[0]
    acc = jnp.full(bm.shape, NEG_INF, F32)
    accm = jnp.full(bmm.shape, NEG_INF, F32)
    for k in range(N_BUCKETS):
        val = rb_ref[k, h] * LOG2E
        acc = jnp.where(bm == k, val, acc)
        accm = jnp.where(bmm == k, val, accm)
    o_ref[0, 0] = acc
    om_ref[0, 0] = accm


def _bias_tables(rel_bias):
    bm, bmm = _bucket_maps()
    kw = 3 * Q_BLOCK
    tab, tabm = pl.pallas_call(
        _bias_table_kernel,
        grid=(3, B_HEADS),
        in_specs=[
            pl.BlockSpec(memory_space=pltpu.SMEM),
            pl.BlockSpec((1, Q_BLOCK, kw), lambda v, h: (v, 0, 0)),
            pl.BlockSpec((1, Q_BLOCK, N_META), lambda v, h: (v, 0, 0)),
        ],
        out_specs=[
            pl.BlockSpec((1, 1, Q_BLOCK, kw), lambda v, h: (v, h, 0, 0)),
            pl.BlockSpec((1, 1, Q_BLOCK, N_META), lambda v, h: (v, h, 0, 0)),
        ],
        out_shape=[
            jax.ShapeDtypeStruct((3, B_HEADS, Q_BLOCK, kw), F32),
            jax.ShapeDtypeStruct((3, B_HEADS, Q_BLOCK, N_META), F32),
        ],
        compiler_params=_params(("arbitrary", "arbitrary")),
        name="bias_tables",
    )(rel_bias, jnp.asarray(bm), jnp.asarray(bmm))
    rows = REP * Q_BLOCK
    return (tab.reshape(3, B_KV_HEADS, rows, kw), tabm.reshape(3, B_KV_HEADS, rows, N_META))


def _window_attn_kernel(q_ref, k_ref, v_ref, km_ref, vm_ref, tab_ref, tabm_ref, sink_ref, o_ref, *, nq):
    jb = pl.program_id(2)
    nblk = k_ref.shape[1] // Q_BLOCK
    km = km_ref[0]
    vm = vm_ref[0]
    sink = sink_ref[0] * LOG2E
    for t in range(nq):
        n = jb * nq + t
        qs = _stack_heads(q_ref[0, t * Q_BLOCK:(t + 1) * Q_BLOCK, :])
        var = jnp.where(n == 0, 0, jnp.where(n == nblk - 1, 2, 1))
        start = pl.multiple_of(jnp.clip(n - 1, 0, nblk - 3) * Q_BLOCK, Q_BLOCK)
        kb = k_ref[0, pl.ds(start, 3 * Q_BLOCK), :]
        vb = v_ref[0, pl.ds(start, 3 * Q_BLOCK), :]
        s = lax.dot_general(qs, kb, _NT, preferred_element_type=F32) + tab_ref[var, 0]
        sm = lax.dot_general(qs, km, _NT, preferred_element_type=F32) + tabm_ref[var, 0]
        m = jnp.maximum(jnp.maximum(jnp.max(s, axis=-1, keepdims=True),
                                    jnp.max(sm, axis=-1, keepdims=True)), sink)
        p = jnp.exp2(s - m)
        pm = jnp.exp2(sm - m)
        l = jnp.sum(p, axis=-1, keepdims=True) + jnp.sum(pm, axis=-1, keepdims=True) + jnp.exp2(sink - m)
        o = (jnp.dot(p.astype(BF16), vb, preferred_element_type=F32)
             + jnp.dot(pm.astype(BF16), vm, preferred_element_type=F32)) / l
        for r in range(REP):
            o_ref[0, t * Q_BLOCK:(t + 1) * Q_BLOCK, r * HEAD_DIM:(r + 1) * HEAD_DIM] = (
                o[r * Q_BLOCK:(r + 1) * Q_BLOCK].astype(BF16))


def _window_attn(proj, km, vm, tab, tabm, sink_rows, *, nq):
    b, s, _ = proj.shape
    assert s // Q_BLOCK >= 3 and (s // Q_BLOCK) % nq == 0
    gw = REP * HEAD_DIM
    rows = REP * Q_BLOCK
    kw = 3 * Q_BLOCK
    tq = nq * Q_BLOCK
    return pl.pallas_call(
        functools.partial(_window_attn_kernel, nq=nq),
        grid=(b, B_KV_HEADS, s // tq),
        in_specs=[
            pl.BlockSpec((1, tq, gw), lambda bi, g, i: (bi, i, COL_QB // REP + g)),
            pl.BlockSpec((1, s, HEAD_DIM), lambda bi, g, i: (bi, 0, COL_KB + g)),
            pl.BlockSpec((1, s, HEAD_DIM), lambda bi, g, i: (bi, 0, COL_VB + g)),
            pl.BlockSpec((1, N_META, HEAD_DIM), lambda bi, g, i: (g, 0, 0)),
            pl.BlockSpec((1, N_META, HEAD_DIM), lambda bi, g, i: (g, 0, 0)),
            pl.BlockSpec((3, 1, rows, kw), lambda bi, g, i: (0, g, 0, 0)),
            pl.BlockSpec((3, 1, rows, N_META), lambda bi, g, i: (0, g, 0, 0)),
            pl.BlockSpec((1, rows, 1), lambda bi, g, i: (g, 0, 0)),
        ],
        out_specs=pl.BlockSpec((1, tq, gw), lambda bi, g, i: (bi, i, g)),
        out_shape=jax.ShapeDtypeStruct((b, s, B_WIDTH), BF16),
        compiler_params=_params(("parallel", "parallel", "arbitrary")),
        name="window_attn",
    )(proj, proj, proj, km, vm, tab, tabm, sink_rows)


def _sigmoid(x):
    return 1.0 / (1.0 + jnp.exp(-x))


def _route(logits):
    lane = lax.broadcasted_iota(jnp.int32, logits.shape, 1).astype(F32)
    ninf = jnp.float32(-jnp.inf)
    big = jnp.float32(ROUTER_COLS)
    is_g = lane < N_GROUPS
    lg = jnp.where(is_g, logits, ninf)
    mg = jnp.max(lg, axis=-1, keepdims=True)
    gidx = jnp.min(jnp.where(lg == mg, lane, big), axis=-1, keepdims=True)
    p_top = 1.0 / jnp.sum(jnp.where(is_g, jnp.exp(lg - mg), 0.0), axis=-1, keepdims=True)
    lo = N_GROUPS + EXPERTS_PER_GROUP * gidx
    sel = (lane >= lo) & (lane < lo + EXPERTS_PER_GROUP)
    le = jnp.where(sel, logits, ninf)
    v1 = jnp.max(le, axis=-1, keepdims=True)
    i1 = jnp.min(jnp.where(sel & (le == v1), lane, big), axis=-1, keepdims=True)
    rest = sel & (lane != i1)
    le2 = jnp.where(rest, logits, ninf)
    v2 = jnp.max(le2, axis=-1, keepdims=True)
    i2 = jnp.min(jnp.where(rest & (le2 == v2), lane, big), axis=-1, keepdims=True)
    t = jnp.exp(v2 - v1)
    w1 = p_top / (1.0 + t)
    w2 = p_top * t / (1.0 + t)
    return i1 - N_GROUPS, i2 - N_GROUPS, w1, w2


def _merge_kernel(oa_ref, ob_ref, ga0_ref, ga1_ref, gb0_ref, gb1_ref, x_ref, wa_ref, wb_ref, wo_ref, gn_ref,
                  wr_ref, br_ref, h2_ref, u2_ref, logits_ref, *, sub):
    for st in range(oa_ref.shape[0] // sub):
        r = slice(st * sub, (st + 1) * sub)
        ya = jnp.dot(oa_ref[r, :], wa_ref[...], preferred_element_type=F32)
        yb = jnp.dot(ob_ref[r, :], wb_ref[...], preferred_element_type=F32)
        ga = jnp.concatenate([ga0_ref[r, :], ga1_ref[r, :]], axis=1).astype(F32)
        gb = jnp.concatenate([gb0_ref[r, :], gb1_ref[r, :]], axis=1).astype(F32)
        mixed = _sigmoid(ga) * ya + _sigmoid(gb) * yb
        h2 = x_ref[r, :] + jnp.dot(mixed.astype(BF16), wo_ref[...], preferred_element_type=F32)
        h2_ref[r, :] = h2
        ms = jnp.mean(h2 * h2, axis=-1, keepdims=True)
        u = h2 * lax.rsqrt(ms + RMS_EPS) * gn_ref[...]
        u_hi = u.astype(BF16)
        u_hi32 = u_hi.astype(F32)
        u_lo = (u - u_hi32).astype(BF16)
        lg = (jnp.dot(u_hi, wr_ref[...], preferred_element_type=F32)
              + jnp.dot(u_lo, wr_ref[...], preferred_element_type=F32))
        logits_ref[r, :] = lg[:, :ROUTER_COLS] + lg[:, ROUTER_COLS:] + br_ref[...]
        half = D_MODEL // 2
        lo_bits = pltpu.bitcast(u_hi32[:, :half], jnp.uint32) >> 16
        hi_bits = pltpu.bitcast(u_hi32[:, half:], jnp.uint32)
        u2_ref[r, :] = hi_bits | lo_bits


def _merge_route(oa, ob, proj2d, x2d, wa, wb, wo, norm_ffn, wr, br, *, tm):
    t_rows = x2d.shape[0]
    gw = D_MODEL // 2
    ga_blk = COL_GA * HEAD_DIM // gw
    gb_blk = COL_GB * HEAD_DIM // gw
    return pl.pallas_call(
        functools.partial(_merge_kernel, sub=min(tm, 256)),
        grid=(t_rows // tm,),
        in_specs=[
            pl.BlockSpec((tm, A_WIDTH), lambda i: (i, 0)),
            pl.BlockSpec((tm, B_WIDTH), lambda i: (i, 0)),
            pl.BlockSpec((tm, gw), lambda i: (i, ga_blk)),
            pl.BlockSpec((tm, gw), lambda i: (i, ga_blk + 1)),
            pl.BlockSpec((tm, gw), lambda i: (i, gb_blk)),
            pl.BlockSpec((tm, gw), lambda i: (i, gb_blk + 1)),
            pl.BlockSpec((tm, D_MODEL), lambda i: (i, 0)),
            _const_spec((A_WIDTH, D_MODEL)),
            _const_spec((B_WIDTH, D_MODEL)),
            _const_spec((D_MODEL, D_MODEL)),
            _const_spec((1, D_MODEL)),
            _const_spec((D_MODEL, 2 * ROUTER_COLS)),
            _const_spec((1, ROUTER_COLS)),
        ],
        out_specs=[
            pl.BlockSpec((tm, D_MODEL), lambda i: (i, 0)),
            pl.BlockSpec((tm, D_MODEL // 2), lambda i: (i, 0)),
            pl.BlockSpec((tm, ROUTER_COLS), lambda i: (i, 0)),
        ],
        out_shape=[
            jax.ShapeDtypeStruct((t_rows, D_MODEL), F32),
            jax.ShapeDtypeStruct((t_rows, D_MODEL // 2), jnp.uint32),
            jax.ShapeDtypeStruct((t_rows, ROUTER_COLS), F32),
        ],
        compiler_params=_params(("parallel",)),
        name="merge_route",
    )(oa, ob, proj2d, proj2d, proj2d, proj2d, x2d, wa, wb, wo, norm_ffn, wr, br)


def _route_kernel(logits_ref, info_ref):
    logits = logits_ref[...]
    e1, e2, w1, w2 = _route(logits)
    lane = lax.broadcasted_iota(jnp.int32, logits.shape, 1)
    info_ref[...] = jnp.where(lane == 0, e1, jnp.where(lane == 1, e2, jnp.where(lane == 2, w1,
                              jnp.where(lane == 3, w2, 0.0))))


def _route_call(logits, *, tm):
    t_rows = logits.shape[0]
    return pl.pallas_call(
        _route_kernel,
        grid=(t_rows // tm,),
        in_specs=[pl.BlockSpec((tm, ROUTER_COLS), lambda i: (i, 0))],
        out_specs=pl.BlockSpec((tm, ROUTER_COLS), lambda i: (i, 0)),
        out_shape=jax.ShapeDtypeStruct((t_rows, ROUTER_COLS), F32),
        compiler_params=_params(("parallel",)),
        name="route",
    )(logits)


SEG_NTILES = 2 * N_EXPERTS


def _positions_kernel(e1_ref, e2_ref, pos1_ref, pos2_ref, te_ref, seg_ref):
    e1 = e1_ref[...]
    e2 = e2_ref[...]
    rows = e1.shape[0]
    r_i = lax.broadcasted_iota(jnp.int32, (LANES, LANES), 0)
    c_i = lax.broadcasted_iota(jnp.int32, (LANES, LANES), 1)
    upper = (r_i < c_i).astype(BF16)
    rr = lax.broadcasted_iota(jnp.int32, (rows, rows), 0)
    rc = lax.broadcasted_iota(jnp.int32, (rows, rows), 1)
    lower = (rc < rr).astype(BF16)
    seg_lane = lax.broadcasted_iota(jnp.int32, seg_ref.shape, 1)
    tile_row = lax.broadcasted_iota(jnp.int32, (1, te_ref.shape[1]), 1).astype(F32) * MOE_TILE
    base = jnp.zeros((1, 1), F32)
    pos1 = jnp.zeros(e1.shape, F32)
    pos2 = jnp.zeros(e1.shape, F32)
    seg = jnp.zeros(seg_ref.shape, F32)
    tile_expert = jnp.zeros(tile_row.shape, F32)
    tile_used = jnp.zeros(tile_row.shape, F32)
    for e in range(N_EXPERTS):
        m1 = e1 == e
        m2 = e2 == e
        m = jnp.where(m1 | m2, 1.0, 0.0)
        lane_pre = jnp.dot(m.astype(BF16), upper, preferred_element_type=F32)
        row_tot = jnp.broadcast_to(jnp.sum(m, axis=-1, keepdims=True), m.shape)
        row_pre = jnp.dot(lower, row_tot.astype(BF16), preferred_element_type=F32)
        total = jnp.sum(row_tot[:, 0:1], axis=0, keepdims=True)
        p = base + row_pre + lane_pre
        pos1 = jnp.where(m1, p, pos1)
        pos2 = jnp.where(m2, p, pos2)
        padded = jnp.floor((total + (MOE_TILE - 1)) * (1.0 / MOE_TILE)) * MOE_TILE
        in_segment = (tile_row >= base) & (tile_row < base + padded)
        tile_used = tile_used + jnp.where(in_segment, jnp.clip(base + total - tile_row, 0.0, MOE_TILE), 0.0)
        base = base + padded
        seg = seg + jnp.where(seg_lane == e, base, 0.0) + jnp.where(seg_lane == N_EXPERTS + e, total, 0.0)
        tile_expert = tile_expert + jnp.where(tile_row >= base, 1.0, 0.0)
    seg = seg + jnp.where(seg_lane == SEG_NTILES, base * (1.0 / MOE_TILE), 0.0)
    pos1_ref[...] = pos1.astype(jnp.int32)
    pos2_ref[...] = pos2.astype(jnp.int32)
    te_ref[0:1, :] = jnp.minimum(tile_expert, N_EXPERTS - 1).astype(jnp.int32)
    te_ref[1:2, :] = tile_used.astype(jnp.int32)
    seg_ref[...] = seg.astype(jnp.int32)


def _positions(e1, e2, n_tiles):
    rows = e1.shape[0]
    ntp = -(-n_tiles // LANES) * LANES
    full = lambda shape: pl.BlockSpec(shape, lambda: (0,) * len(shape))
    return pl.pallas_call(
        _positions_kernel,
        in_specs=[full((rows, LANES)), full((rows, LANES))],
        out_specs=[full((rows, LANES)), full((rows, LANES)), full((2, ntp)), full((1, LANES))],
        out_shape=[
            jax.ShapeDtypeStruct((rows, LANES), jnp.int32),
            jax.ShapeDtypeStruct((rows, LANES), jnp.int32),
            jax.ShapeDtypeStruct((2, ntp), jnp.int32),
            jax.ShapeDtypeStruct((1, LANES), jnp.int32),
        ],
        compiler_params=pltpu.CompilerParams(vmem_limit_bytes=VMEM_LIMIT),
        name="positions",
    )(e1, e2)


def _dispatch_kernel(seg_ref, pos1_ref, pos2_ref, u_ref, xs_ref, z_ref, zsem, sem, *, tb):
    i = pl.program_id(0)

    def zero_copy(e):
        end = seg_ref[e]
        start = pl.multiple_of(end - MOE_TILE, MOE_TILE)
        return pltpu.make_async_copy(z_ref, xs_ref.at[pl.ds(start, MOE_TILE)], zsem)

    def nonempty(e):
        return seg_ref[e] > (seg_ref[e - 1] if e > 0 else 0)

    n_tiles = xs_ref.shape[0] // MOE_TILE

    def tail_copy(k):
        start = pl.multiple_of((seg_ref[SEG_NTILES] + k) * MOE_TILE, MOE_TILE)
        return pltpu.make_async_copy(z_ref, xs_ref.at[pl.ds(start, MOE_TILE)], zsem)

    def tail_exists(k):
        return seg_ref[SEG_NTILES] + k < n_tiles

    @pl.when(i == 0)
    def _():
        z_ref[...] = jnp.zeros(z_ref.shape, z_ref.dtype)
        for e in range(N_EXPERTS):
            @pl.when(nonempty(e))
            def _():
                zero_copy(e).start()

            @pl.when(tail_exists(e))
            def _():
                tail_copy(e).start()
        for e in range(N_EXPERTS):
            @pl.when(nonempty(e))
            def _():
                zero_copy(e).wait()

            @pl.when(tail_exists(e))
            def _():
                tail_copy(e).wait()

    def row_copy(t, pos_ref):
        return pltpu.make_async_copy(u_ref.at[pl.ds(t, 1)], xs_ref.at[pl.ds(pos_ref[t], 1)], sem)

    def issue(t, carry):
        row_copy(t, pos1_ref).start()
        row_copy(t, pos2_ref).start()
        return carry

    lax.fori_loop(0, tb, issue, 0, unroll=8)

    for _ in range(2):
        pltpu.make_async_copy(u_ref, xs_ref.at[pl.ds(0, tb)], sem).wait()


def _dispatch(seg, pos1, pos2, u2p, n_tiles, *, tb):
    t_rows, width = u2p.shape
    return pl.pallas_call(
        functools.partial(_dispatch_kernel, tb=tb),
        grid_spec=pltpu.PrefetchScalarGridSpec(
            num_scalar_prefetch=1,
            grid=(t_rows // tb,),
            in_specs=[
                pl.BlockSpec((tb,), lambda i, seg: (i,), memory_space=pltpu.SMEM),
                pl.BlockSpec((tb,), lambda i, seg: (i,), memory_space=pltpu.SMEM),
                pl.BlockSpec((tb, width), lambda i, seg: (i, 0)),
            ],
            out_specs=pl.BlockSpec(memory_space=pl.ANY),
            scratch_shapes=[
                pltpu.VMEM((MOE_TILE, width), u2p.dtype),
                pltpu.SemaphoreType.DMA(()),
                pltpu.SemaphoreType.DMA(()),
            ],
        ),
        out_shape=jax.ShapeDtypeStruct((n_tiles * MOE_TILE, width), u2p.dtype),
        compiler_params=_params(("arbitrary",)),
        name="dispatch",
    )(seg, pos1, pos2, u2p)


def _moe_kernel(te_ref, used_ref, seg_ref, x_ref, wg_ref, wu_ref, wd_ref, y_ref):
    i = pl.program_id(0)
    valid = i < seg_ref[SEG_NTILES]
    used = used_ref[i]
    half_tile = MOE_TILE // 2

    def expert(rows):
        xw = x_ref[rows, :]
        half = D_MODEL // 2
        lo = pltpu.bitcast(xw << 16, F32).astype(BF16)
        hi = pltpu.bitcast(xw & jnp.uint32(0xFFFF0000), F32).astype(BF16)
        hg = (jnp.dot(lo, wg_ref[0, :half, :], preferred_element_type=F32)
              + jnp.dot(hi, wg_ref[0, half:, :], preferred_element_type=F32))
        hu = (jnp.dot(lo, wu_ref[0, :half, :], preferred_element_type=F32)
              + jnp.dot(hi, wu_ref[0, half:, :], preferred_element_type=F32))
        hid = hg * _sigmoid(hg) * hu
        y_ref[rows, :] = jnp.dot(hid.astype(BF16), wd_ref[0], preferred_element_type=F32)

    @pl.when(valid & (used > half_tile))
    def _():
        expert(slice(0, MOE_TILE))

    @pl.when(valid & (used <= half_tile))
    def _():
        expert(slice(0, half_tile))
        y_ref[half_tile:, :] = jnp.zeros((MOE_TILE - half_tile, D_MODEL), y_ref.dtype)

    @pl.when(jnp.logical_not(valid))
    def _():
        y_ref[...] = jnp.zeros(y_ref.shape, y_ref.dtype)


def _moe(tile_expert, tile_used, seg, xs, wg, wu, wd):
    n_tiles = xs.shape[0] // MOE_TILE

    def row_map(i, te, used, seg):
        return (jnp.minimum(i, seg[SEG_NTILES] - 1), 0)

    def w_map(i, te, used, seg):
        return (te[jnp.minimum(i, seg[SEG_NTILES] - 1)], 0, 0)

    return pl.pallas_call(
        _moe_kernel,
        grid_spec=pltpu.PrefetchScalarGridSpec(
            num_scalar_prefetch=3,
            grid=(n_tiles,),
            in_specs=[
                pl.BlockSpec((MOE_TILE, D_MODEL // 2), row_map),
                pl.BlockSpec((1, D_MODEL, EXPERT_FF), w_map),
                pl.BlockSpec((1, D_MODEL, EXPERT_FF), w_map),
                pl.BlockSpec((1, EXPERT_FF, D_MODEL), w_map),
            ],
            out_specs=pl.BlockSpec((MOE_TILE, D_MODEL), lambda i, te, used, seg: (i, 0)),
        ),
        out_shape=jax.ShapeDtypeStruct((n_tiles * MOE_TILE, D_MODEL), F32),
        compiler_params=_params(("arbitrary",)),
        name="moe",
    )(tile_expert, tile_used, seg, xs, wg, wu, wd)


def _final_kernel(pos1_ref, pos2_ref, nxt1_ref, nxt2_ref, h2_ref, info_ref, g_ref, ys_ref, o_ref, ybuf, sems, *, tm):
    i = pl.program_id(0)
    n = pl.num_programs(0)
    slot = i % 2

    def gather(p1_ref, p2_ref, s):
        def issue(t, carry):
            pltpu.make_async_copy(ys_ref.at[pl.ds(p1_ref[t], 1)], ybuf.at[s, 0, pl.ds(t, 1)], sems.at[s]).start()
            pltpu.make_async_copy(ys_ref.at[pl.ds(p2_ref[t], 1)], ybuf.at[s, 1, pl.ds(t, 1)], sems.at[s]).start()
            return carry

        lax.fori_loop(0, tm, issue, 0, unroll=8)

    @pl.when(i == 0)
    def _():
        gather(pos1_ref, pos2_ref, 0)

    @pl.when(i + 1 < n)
    def _():
        gather(nxt1_ref, nxt2_ref, 1 - slot)

    for k in range(2):
        pltpu.make_async_copy(ys_ref.at[pl.ds(0, tm)], ybuf.at[slot, k], sems.at[slot]).wait()

    info = info_ref[...]
    h = h2_ref[...] + info[:, 2:3] * ybuf[slot, 0] + info[:, 3:4] * ybuf[slot, 1]
    ms = jnp.mean(h * h, axis=-1, keepdims=True)
    o_ref[...] = h * lax.rsqrt(ms + RMS_EPS) * g_ref[...]


def _final(pos1, pos2, h2, info, final_norm, ys, *, tm):
    t_rows = h2.shape[0]
    n_steps = t_rows // tm
    nxt = lambda i: (jnp.minimum(i + 1, n_steps - 1),)
    return pl.pallas_call(
        functools.partial(_final_kernel, tm=tm),
        grid=(n_steps,),
        in_specs=[
            pl.BlockSpec((tm,), lambda i: (i,), memory_space=pltpu.SMEM),
            pl.BlockSpec((tm,), lambda i: (i,), memory_space=pltpu.SMEM),
            pl.BlockSpec((tm,), nxt, memory_space=pltpu.SMEM),
            pl.BlockSpec((tm,), nxt, memory_space=pltpu.SMEM),
            pl.BlockSpec((tm, D_MODEL), lambda i: (i, 0)),
            pl.BlockSpec((tm, ROUTER_COLS), lambda i: (i, 0)),
            pl.BlockSpec((1, D_MODEL), lambda i: (0, 0)),
            pl.BlockSpec(memory_space=pl.ANY),
        ],
        out_specs=pl.BlockSpec((tm, D_MODEL), lambda i: (i, 0)),
        out_shape=jax.ShapeDtypeStruct((t_rows, D_MODEL), F32),
        scratch_shapes=[pltpu.VMEM((2, 2, tm, D_MODEL), F32), pltpu.SemaphoreType.DMA((2,))],
        compiler_params=_params(("arbitrary",)),
        name="final",
    )(pos1, pos2, pos1, pos2, h2, info, final_norm, ys)


def _rope_tables(row, col):
    n_freq = HEAD_DIM // 4
    inv_freq = ROPE_THETA ** (-jnp.arange(n_freq, dtype=F32) / n_freq)
    ra = row.astype(F32)[:, None] * inv_freq
    ca = col.astype(F32)[:, None] * inv_freq
    cos = jnp.concatenate([jnp.cos(ra), jnp.cos(ca), jnp.cos(ra), jnp.cos(ca)], axis=-1)
    sin = jnp.concatenate([-jnp.sin(ra), -jnp.sin(ca), jnp.sin(ra), jnp.sin(ca)], axis=-1)
    return cos, sin


def _gain_rows(g):
    gs = _swap_rotary_sections(g.astype(F32))
    return jnp.stack([gs, jnp.roll(gs, HEAD_DIM // 2)])


def _split_bf16(w):
    hi = w.astype(BF16)
    lo = (w - hi.astype(F32)).astype(BF16)
    return hi, lo


def _tile(n, pref):
    t = min(n, pref)
    assert n % t == 0, (n, pref)
    return t


def _encode_group(x, shared):
    (norm_mix, w_in, col_scale, q_norm, k_norm, meta_kv, tab, tabm, sink_rows, wa, wb, wo, norm_ffn, wr, br,
     wg, wu, wd, final_norm) = shared
    b, s, _ = x.shape
    t_rows = b * s
    x2d = x.reshape(t_rows, D_MODEL)
    tok = jnp.arange(s)
    cos, sin = _rope_tables(tok // GRID_W, tok % GRID_W)
    proj2d = _in_proj(x2d, norm_mix, w_in, col_scale, cos, sin, q_norm, k_norm, tm=_tile(s, 1024))
    proj = proj2d.reshape(b, s, IN_COLS)
    ka_m, va_m, kb_m, vb_m = meta_kv
    oa = _global_attn(proj, ka_m, va_m, tq=_tile(s, 256), ck=_tile(s, 1024))
    nblk = s // Q_BLOCK
    ob = _window_attn(proj, kb_m, vb_m, tab, tabm, sink_rows, nq=math.gcd(nblk, 8))
    h2, u2p, logits = _merge_route(oa.reshape(t_rows, A_WIDTH), ob.reshape(t_rows, B_WIDTH), proj2d, x2d,
                                   wa, wb, wo, norm_ffn, wr, br, tm=_tile(t_rows, 512))
    info = _route_call(logits, tm=_tile(t_rows, 2048))
    e1 = info[:, 0].reshape(t_rows // LANES, LANES)
    e2 = info[:, 1].reshape(t_rows // LANES, LANES)
    n_tiles = 2 * t_rows // MOE_TILE + N_EXPERTS
    pos1, pos2, tile_expert, seg = _positions(e1, e2, n_tiles)
    pos1 = pos1.reshape(t_rows)
    pos2 = pos2.reshape(t_rows)
    seg = seg.reshape(LANES)
    xs = _dispatch(seg, pos1, pos2, u2p, n_tiles, tb=_tile(t_rows, 2048))
    ys = _moe(tile_expert[0, :n_tiles], tile_expert[1, :n_tiles], seg, xs, wg, wu, wd)
    out = _final(pos1, pos2, h2, info, final_norm, ys, tm=_tile(t_rows, 256))
    return out.reshape(b, s, D_MODEL)


def kernel(x_prompt, x_sample, meta_tokens, rel_bias, final_norm, norm_mix, w_in, q_norm, k_norm, sink,
           w_branch_a, w_branch_b, w_out, norm_ffn, w_router_g, b_router_g, w_router_e, b_router_e,
           w_gate, w_up, w_down):
    assert norm_mix.shape[0] == 1, "single-layer encoder"
    rot_cols = (A_HEADS + A_KV_HEADS) * HEAD_DIM
    w_in_b = w_in[0].astype(BF16)
    w_in_b = lax.dynamic_update_slice(w_in_b, _swap_rotary_sections(w_in_b[:, :rot_cols]), (0, 0))
    norm_mix2 = norm_mix[0].reshape(1, D_MODEL)
    q_norm2 = _gain_rows(q_norm[0])
    k_norm2 = _gain_rows(k_norm[0])
    col = jnp.arange(IN_COLS)
    is_qb = (col >= COL_QB * HEAD_DIM) & (col < COL_KB * HEAD_DIM)
    col_scale = jnp.where(is_qb, SCORE_SCALE * LOG2E, 1.0).astype(F32).reshape(1, IN_COLS)

    cos_m, sin_m = _rope_tables(jnp.full((N_META,), -1), jnp.arange(N_META))
    proj_m = _in_proj(meta_tokens, norm_mix2, w_in_b, col_scale, cos_m, sin_m, q_norm2, k_norm2, tm=N_META)

    def meta_heads(c0):
        blk = proj_m[:, c0 * HEAD_DIM:(c0 + A_KV_HEADS) * HEAD_DIM]
        return blk.reshape(N_META, A_KV_HEADS, HEAD_DIM).transpose(1, 0, 2)

    meta_kv = tuple(meta_heads(c) for c in (COL_KA, COL_VA, COL_KB, COL_VB))

    tab, tabm = _bias_tables(rel_bias)
    sink_rows = jnp.repeat(sink[0].astype(F32), Q_BLOCK).reshape(B_KV_HEADS, REP * Q_BLOCK, 1)

    wr_full = jnp.zeros((D_MODEL, ROUTER_COLS), F32)
    wr_full = wr_full.at[:, :N_GROUPS].set(w_router_g[0]).at[:, N_GROUPS:N_GROUPS + N_EXPERTS].set(w_router_e[0])
    wr_hi, wr_lo = _split_bf16(wr_full)
    wr = jnp.concatenate([wr_hi, wr_lo], axis=1)
    br = jnp.zeros((1, ROUTER_COLS), F32)
    br = br.at[0, :N_GROUPS].set(b_router_g[0]).at[0, N_GROUPS:N_GROUPS + N_EXPERTS].set(b_router_e[0])

    shared = (norm_mix2, w_in_b, col_scale, q_norm2, k_norm2, meta_kv, tab, tabm, sink_rows,
              w_branch_a[0].astype(BF16), w_branch_b[0].astype(BF16), w_out[0].astype(BF16),
              norm_ffn[0].reshape(1, D_MODEL), wr, br,
              w_gate[0].astype(BF16), w_up[0].astype(BF16), w_down[0].astype(BF16),
              final_norm.reshape(1, D_MODEL))
    return (_encode_group(x_prompt, shared), _encode_group(x_sample, shared))
```

```python
import functools
import math

import numpy as np
import jax
import jax.numpy as jnp
from jax import lax
from jax.experimental import pallas as pl
from jax.experimental.pallas import tpu as pltpu

F32 = jnp.float32
BF16 = jnp.bfloat16

D_MODEL = 2048
HEAD_DIM = 128
A_HEADS = 8
A_KV_HEADS = 2
B_HEADS = 8
B_KV_HEADS = 2
REP = A_HEADS // A_KV_HEADS
A_WIDTH = A_HEADS * HEAD_DIM
B_WIDTH = B_HEADS * HEAD_DIM
KV_WIDTH = A_KV_HEADS * HEAD_DIM
IN_COLS = A_WIDTH + 2 * KV_WIDTH + B_WIDTH + 2 * KV_WIDTH + 2 * D_MODEL
Q_BLOCK = 128
WINDOW = 128
N_META = 16
GRID_W = 64
ROPE_THETA = 10000.0
N_BUCKETS = 32
MAX_DISTANCE = 128
N_GROUPS = 4
EXPERTS_PER_GROUP = 4
N_EXPERTS = N_GROUPS * EXPERTS_PER_GROUP
EXPERT_FF = 1024
RMS_EPS = 1e-6
NEG_INF = -1e30
SCORE_SCALE = HEAD_DIM ** -0.5
LOG2E = math.log2(math.e)

COL_QA = 0
COL_KA = A_WIDTH // HEAD_DIM
COL_VA = COL_KA + A_KV_HEADS
COL_QB = COL_VA + A_KV_HEADS
COL_KB = COL_QB + B_HEADS
COL_VB = COL_KB + B_KV_HEADS
COL_GA = COL_VB + B_KV_HEADS
COL_GB = COL_GA + D_MODEL // HEAD_DIM

LANES = 128
VMEM_LIMIT = 56 * 1024 * 1024
MOE_TILE = 512
ROUTER_COLS = 128

_NT = (((1,), (1,)), ((), ()))


def _params(sem, vmem=VMEM_LIMIT):
    return pltpu.CompilerParams(dimension_semantics=sem, vmem_limit_bytes=vmem)


def _const_spec(shape):
    nd = len(shape)
    return pl.BlockSpec(shape, lambda *_: (0,) * nd, pipeline_mode=pl.Buffered(1))


def _swap_rotary_sections(w):
    lead = w.shape[:-1]
    heads = w.shape[-1] // HEAD_DIM
    nd = len(lead)
    w5 = w.reshape(lead + (heads, 2, 2, HEAD_DIM // 4))
    return jnp.swapaxes(w5, nd + 1, nd + 2).reshape(w.shape)


def _norm_rope(a, cg, sg):
    ones = jnp.ones((HEAD_DIM, HEAD_DIM), BF16)
    ssq = jnp.dot((a * a).astype(BF16), ones, preferred_element_type=F32)
    rinv = lax.rsqrt(ssq * (1.0 / HEAD_DIM) + RMS_EPS)
    return rinv * (a * cg + pltpu.roll(a, HEAD_DIM // 2, 1) * sg)


def _in_proj_kernel(x_ref, g_ref, w_ref, cs_ref, cos_ref, sin_ref, qn_ref, kn_ref, o_ref, u_ref, *, sub):
    j = pl.program_id(1)
    row_tiles = [slice(k * sub, (k + 1) * sub) for k in range(x_ref.shape[0] // sub)]

    def project(r):
        return jnp.dot(u_ref[r, :], w_ref[...], preferred_element_type=F32) * cs_ref[...]

    def rope_factors(gain_ref, scale, r):
        return cos_ref[r, :] * (gain_ref[0:1, :] * scale), sin_ref[r, :] * (gain_ref[1:2, :] * scale)

    @pl.when(j == 0)
    def _():
        rot = (A_HEADS + A_KV_HEADS) * HEAD_DIM
        for r in row_tiles:
            x = x_ref[r, :]
            ms = jnp.mean(x * x, axis=-1, keepdims=True)
            u_ref[r, :] = (x * lax.rsqrt(ms + RMS_EPS) * g_ref[...]).astype(BF16)
            acc = project(r)
            q_factors = rope_factors(qn_ref, SCORE_SCALE * LOG2E, r)
            k_factors = rope_factors(kn_ref, 1.0, r)
            for h in range(A_HEADS + A_KV_HEADS):
                sl = slice(h * HEAD_DIM, (h + 1) * HEAD_DIM)
                cg, sg = q_factors if h < A_HEADS else k_factors
                o_ref[r, sl] = _norm_rope(acc[:, sl], cg, sg).astype(BF16)
            o_ref[r, rot:] = acc[:, rot:].astype(BF16)

    @pl.when(j >= 1)
    def _():
        for r in row_tiles:
            o_ref[r, :] = project(r).astype(BF16)


IN_PROJ_TN = IN_COLS // 4


def _in_proj(x2d, norm_g, w_bf16, col_scale, cos, sin, q_norm, k_norm, *, tm):
    t_rows = x2d.shape[0]
    tn = IN_PROJ_TN
    assert tn % HEAD_DIM == 0 and tn >= (A_HEADS + A_KV_HEADS) * HEAD_DIM
    pos_tiles = cos.shape[0] // tm
    return pl.pallas_call(
        functools.partial(_in_proj_kernel, sub=min(tm, 512)),
        grid=(t_rows // tm, IN_COLS // tn),
        in_specs=[
            pl.BlockSpec((tm, D_MODEL), lambda i, j: (i, 0)),
            pl.BlockSpec((1, D_MODEL), lambda i, j: (0, 0)),
            pl.BlockSpec((D_MODEL, tn), lambda i, j: (0, j)),
            pl.BlockSpec((1, tn), lambda i, j: (0, j)),
            pl.BlockSpec((tm, HEAD_DIM), lambda i, j: (i % pos_tiles, 0)),
            pl.BlockSpec((tm, HEAD_DIM), lambda i, j: (i % pos_tiles, 0)),
            pl.BlockSpec((2, HEAD_DIM), lambda i, j: (0, 0)),
            pl.BlockSpec((2, HEAD_DIM), lambda i, j: (0, 0)),
        ],
        out_specs=pl.BlockSpec((tm, tn), lambda i, j: (i, j)),
        out_shape=jax.ShapeDtypeStruct((t_rows, IN_COLS), BF16),
        scratch_shapes=[pltpu.VMEM((tm, D_MODEL), BF16)],
        compiler_params=_params(("parallel", "arbitrary")),
        name="in_proj",
    )(x2d, norm_g, w_bf16, col_scale, cos, sin, q_norm, k_norm)


def _stack_heads(q):
    return jnp.concatenate([q[:, r * HEAD_DIM:(r + 1) * HEAD_DIM] for r in range(REP)], axis=0)


def _global_attn_kernel(q_ref, k_ref, v_ref, km_ref, vm_ref, o_ref, vx_ref, vmx_ref, *, tq, ck):
    @pl.when(pl.program_id(2) == 0)
    def _():
        vx_ref[:, :HEAD_DIM] = v_ref[0]
        vx_ref[:, HEAD_DIM:] = jnp.ones((v_ref.shape[1], HEAD_DIM), BF16)
        vmx_ref[:, :HEAD_DIM] = vm_ref[0]
        vmx_ref[:, HEAD_DIM:] = jnp.ones((N_META, HEAD_DIM), BF16)

    qs = _stack_heads(q_ref[0])
    n_chunks = k_ref.shape[1] // ck

    def scores(c):
        return lax.dot_general(qs, k_ref[0, c * ck:(c + 1) * ck, :], _NT, preferred_element_type=F32)

    s = lax.dot_general(qs, km_ref[0], _NT, preferred_element_type=F32)
    s_next = scores(0)
    m = jnp.max(s, axis=-1, keepdims=True)
    acc = jnp.dot(jnp.exp2(s - m).astype(BF16), vmx_ref[...], preferred_element_type=F32)
    for c in range(n_chunks):
        s = s_next
        if c + 1 < n_chunks:
            s_next = scores(c + 1)
        m_new = jnp.maximum(m, jnp.max(s, axis=-1, keepdims=True))
        p = jnp.exp2(s - m_new).astype(BF16)
        acc = jnp.exp2(m - m_new) * acc + jnp.dot(p, vx_ref[c * ck:(c + 1) * ck, :],
                                                  preferred_element_type=F32)
        m = m_new
    o = acc[:, :HEAD_DIM] / acc[:, HEAD_DIM:]
    for r in range(REP):
        o_ref[0, :, r * HEAD_DIM:(r + 1) * HEAD_DIM] = o[r * tq:(r + 1) * tq].astype(BF16)


def _global_attn(proj, km, vm, *, tq, ck):
    b, s, _ = proj.shape
    gw = REP * HEAD_DIM
    return pl.pallas_call(
        functools.partial(_global_attn_kernel, tq=tq, ck=ck),
        grid=(b, A_KV_HEADS, s // tq),
        in_specs=[
            pl.BlockSpec((1, tq, gw), lambda bi, g, i: (bi, i, g)),
            pl.BlockSpec((1, s, HEAD_DIM), lambda bi, g, i: (bi, 0, COL_KA + g)),
            pl.BlockSpec((1, s, HEAD_DIM), lambda bi, g, i: (bi, 0, COL_VA + g)),
            pl.BlockSpec((1, N_META, HEAD_DIM), lambda bi, g, i: (g, 0, 0)),
            pl.BlockSpec((1, N_META, HEAD_DIM), lambda bi, g, i: (g, 0, 0)),
        ],
        out_specs=pl.BlockSpec((1, tq, gw), lambda bi, g, i: (bi, i, g)),
        out_shape=jax.ShapeDtypeStruct((b, s, A_WIDTH), BF16),
        scratch_shapes=[pltpu.VMEM((s, 2 * HEAD_DIM), BF16), pltpu.VMEM((N_META, 2 * HEAD_DIM), BF16)],
        compiler_params=_params(("parallel", "parallel", "arbitrary")),
        name="global_attn",
    )(proj, proj, proj, km, vm)


def _t5_bucket_np(rel):
    nb = N_BUCKETS // 2
    max_exact = nb // 2
    bucket = np.where(rel > 0, nb, 0)
    n = np.abs(rel)
    nf = np.maximum(n, 1).astype(np.float32)
    large = max_exact + (np.log(nf / np.float32(max_exact)) / np.float32(math.log(MAX_DISTANCE / max_exact))
                         * np.float32(nb - max_exact)).astype(np.int32)
    large = np.minimum(large, nb - 1)
    return (bucket + np.where(n < max_exact, n, large)).astype(np.int32)


def _bucket_maps():
    i = np.arange(Q_BLOCK)[:, None]
    j = np.arange(3 * Q_BLOCK)[None, :]
    real = []
    for off in range(3):
        rel = j - off * Q_BLOCK - i
        real.append(np.where(np.abs(rel) <= WINDOW, _t5_bucket_np(rel), -1))
    m = np.arange(N_META)[None, :]
    first = _t5_bucket_np(m - (N_META + i))
    later = _t5_bucket_np(m - (N_META + i + Q_BLOCK))
    meta = [first, later, later]
    return np.stack(real).astype(np.int32), np.stack(meta).astype(np.int32)


def _bias_table_kernel(rb_ref, bm_ref, bmm_ref, o_ref, om_ref):
    h = pl.program_id(1)
    bm = bm_ref[0]
    bmm = bmm_ref[0]
    acc = jnp.full(bm.shape, NEG_INF, F32)
    accm = jnp.full(bmm.shape, NEG_INF, F32)
    for k in range(N_BUCKETS):
        val = rb_ref[k, h] * LOG2E
        acc = jnp.where(bm == k, val, acc)
        accm = jnp.where(bmm == k, val, accm)
    o_ref[0, 0] = acc
    om_ref[0, 0] = accm


def _bias_tables(rel_bias):
    bm, bmm = _bucket_maps()
    kw = 3 * Q_BLOCK
    tab, tabm = pl.pallas_call(
        _bias_table_kernel,
        grid=(3, B_HEADS),
        in_specs=[
            pl.BlockSpec(memory_space=pltpu.SMEM),
            pl.BlockSpec((1, Q_BLOCK, kw), lambda v, h: (v, 0, 0)),
            pl.BlockSpec((1, Q_BLOCK, N_META), lambda v, h: (v, 0, 0)),
        ],
        out_specs=[
            pl.BlockSpec((1, 1, Q_BLOCK, kw), lambda v, h: (v, h, 0, 0)),
            pl.BlockSpec((1, 1, Q_BLOCK, N_META), lambda v, h: (v, h, 0, 0)),
        ],
        out_shape=[
            jax.ShapeDtypeStruct((3, B_HEADS, Q_BLOCK, kw), F32),
            jax.ShapeDtypeStruct((3, B_HEADS, Q_BLOCK, N_META), F32),
        ],
        compiler_params=_params(("arbitrary", "arbitrary")),
        name="bias_tables",
    )(rel_bias, jnp.asarray(bm), jnp.asarray(bmm))
    rows = REP * Q_BLOCK
    return (tab.reshape(3, B_KV_HEADS, rows, kw), tabm.reshape(3, B_KV_HEADS, rows, N_META))


def _window_attn_kernel(q_ref, k_ref, v_ref, km_ref, vm_ref, tab_ref, tabm_ref, sink_ref, o_ref, *, nq):
    jb = pl.program_id(2)
    nblk = k_ref.shape[1] // Q_BLOCK
    km = km_ref[0]
    vm = vm_ref[0]
    sink = sink_ref[0] * LOG2E
    for t in range(nq):
        n = jb * nq + t
        qs = _stack_heads(q_ref[0, t * Q_BLOCK:(t + 1) * Q_BLOCK, :])
        var = jnp.where(n == 0, 0, jnp.where(n == nblk - 1, 2, 1))
        start = pl.multiple_of(jnp.clip(n - 1, 0, nblk - 3) * Q_BLOCK, Q_BLOCK)
        kb = k_ref[0, pl.ds(start, 3 * Q_BLOCK), :]
        vb = v_ref[0, pl.ds(start, 3 * Q_BLOCK), :]
        s = lax.dot_general(qs, kb, _NT, preferred_element_type=F32) + tab_ref[var, 0]
        sm = lax.dot_general(qs, km, _NT, preferred_element_type=F32) + tabm_ref[var, 0]
        m = jnp.maximum(jnp.maximum(jnp.max(s, axis=-1, keepdims=True),
                                    jnp.max(sm, axis=-1, keepdims=True)), sink)
        p = jnp.exp2(s - m)
        pm = jnp.exp2(sm - m)
        l = jnp.sum(p, axis=-1, keepdims=True) + jnp.sum(pm, axis=-1, keepdims=True) + jnp.exp2(sink - m)
        o = (jnp.dot(p.astype(BF16), vb, preferred_element_type=F32)
             + jnp.dot(pm.astype(BF16), vm, preferred_element_type=F32)) / l
        for r in range(REP):
            o_ref[0, t * Q_BLOCK:(t + 1) * Q_BLOCK, r * HEAD_DIM:(r + 1) * HEAD_DIM] = (
                o[r * Q_BLOCK:(r + 1) * Q_BLOCK].astype(BF16))


def _window_attn(proj, km, vm, tab, tabm, sink_rows, *, nq):
    b, s, _ = proj.shape
    assert s // Q_BLOCK >= 3 and (s // Q_BLOCK) % nq == 0
    gw = REP * HEAD_DIM
    rows = REP * Q_BLOCK
    kw = 3 * Q_BLOCK
    tq = nq * Q_BLOCK
    return pl.pallas_call(
        functools.partial(_window_attn_kernel, nq=nq),
        grid=(b, B_KV_HEADS, s // tq),
        in_specs=[
            pl.BlockSpec((1, tq, gw), lambda bi, g, i: (bi, i, COL_QB // REP + g)),
            pl.BlockSpec((1, s, HEAD_DIM), lambda bi, g, i: (bi, 0, COL_KB + g)),
            pl.BlockSpec((1, s, HEAD_DIM), lambda bi, g, i: (bi, 0, COL_VB + g)),
            pl.BlockSpec((1, N_META, HEAD_DIM), lambda bi, g, i: (g, 0, 0)),
            pl.BlockSpec((1, N_META, HEAD_DIM), lambda bi, g, i: (g, 0, 0)),
            pl.BlockSpec((3, 1, rows, kw), lambda bi, g, i: (0, g, 0, 0)),
            pl.BlockSpec((3, 1, rows, N_META), lambda bi, g, i: (0, g, 0, 0)),
            pl.BlockSpec((1, rows, 1), lambda bi, g, i: (g, 0, 0)),
        ],
        out_specs=pl.BlockSpec((1, tq, gw), lambda bi, g, i: (bi, i, g)),
        out_shape=jax.ShapeDtypeStruct((b, s, B_WIDTH), BF16),
        compiler_params=_params(("parallel", "parallel", "arbitrary")),
        name="window_attn",
    )(proj, proj, proj, km, vm, tab, tabm, sink_rows)


def _sigmoid(x):
    return 1.0 / (1.0 + jnp.exp(-x))


def _route(logits):
    lane = lax.broadcasted_iota(jnp.int32, logits.shape, 1).astype(F32)
    ninf = jnp.float32(-jnp.inf)
    big = jnp.float32(ROUTER_COLS)
    is_g = lane < N_GROUPS
    lg = jnp.where(is_g, logits, ninf)
    mg = jnp.max(lg, axis=-1, keepdims=True)
    gidx = jnp.min(jnp.where(lg == mg, lane, big), axis=-1, keepdims=True)
    p_top = 1.0 / jnp.sum(jnp.where(is_g, jnp.exp(lg - mg), 0.0), axis=-1, keepdims=True)
    lo = N_GROUPS + EXPERTS_PER_GROUP * gidx
    sel = (lane >= lo) & (lane < lo + EXPERTS_PER_GROUP)
    le = jnp.where(sel, logits, ninf)
    v1 = jnp.max(le, axis=-1, keepdims=True)
    i1 = jnp.min(jnp.where(sel & (le == v1), lane, big), axis=-1, keepdims=True)
    rest = sel & (lane != i1)
    le2 = jnp.where(rest, logits, ninf)
    v2 = jnp.max(le2, axis=-1, keepdims=True)
    i2 = jnp.min(jnp.where(rest & (le2 == v2), lane, big), axis=-1, keepdims=True)
    t = jnp.exp(v2 - v1)
    w1 = p_top / (1.0 + t)
    w2 = p_top * t / (1.0 + t)
    return i1 - N_GROUPS, i2 - N_GROUPS, w1, w2


def _merge_kernel(oa_ref, ob_ref, ga0_ref, ga1_ref, gb0_ref, gb1_ref, x_ref, wa_ref, wb_ref, wo_ref, gn_ref,
                  wr_ref, br_ref, h2_ref, u2_ref, logits_ref, *, sub):
    for st in range(oa_ref.shape[0] // sub):
        r = slice(st * sub, (st + 1) * sub)
        ya = jnp.dot(oa_ref[r, :], wa_ref[...], preferred_element_type=F32)
        yb = jnp.dot(ob_ref[r, :], wb_ref[...], preferred_element_type=F32)
        ga = jnp.concatenate([ga0_ref[r, :], ga1_ref[r, :]], axis=1).astype(F32)
        gb = jnp.concatenate([gb0_ref[r, :], gb1_ref[r, :]], axis=1).astype(F32)
        mixed = _sigmoid(ga) * ya + _sigmoid(gb) * yb
        h2 = x_ref[r, :] + jnp.dot(mixed.astype(BF16), wo_ref[...], preferred_element_type=F32)
        h2_ref[r, :] = h2
        ms = jnp.mean(h2 * h2, axis=-1, keepdims=True)
        u = h2 * lax.rsqrt(ms + RMS_EPS) * gn_ref[...]
        u_hi = u.astype(BF16)
        u_hi32 = u_hi.astype(F32)
        u_lo = (u - u_hi32).astype(BF16)
        lg = (jnp.dot(u_hi, wr_ref[...], preferred_element_type=F32)
              + jnp.dot(u_lo, wr_ref[...], preferred_element_type=F32))
        logits_ref[r, :] = lg[:, :ROUTER_COLS] + lg[:, ROUTER_COLS:] + br_ref[...]
        half = D_MODEL // 2
        lo_bits = pltpu.bitcast(u_hi32[:, :half], jnp.uint32) >> 16
        hi_bits = pltpu.bitcast(u_hi32[:, half:], jnp.uint32)
        u2_ref[r, :] = hi_bits | lo_bits


def _merge_route(oa, ob, proj2d, x2d, wa, wb, wo, norm_ffn, wr, br, *, tm):
    t_rows = x2d.shape[0]
    gw = D_MODEL // 2
    ga_blk = COL_GA * HEAD_DIM // gw
    gb_blk = COL_GB * HEAD_DIM // gw
    return pl.pallas_call(
        functools.partial(_merge_kernel, sub=min(tm, 256)),
        grid=(t_rows // tm,),
        in_specs=[
            pl.BlockSpec((tm, A_WIDTH), lambda i: (i, 0)),
            pl.BlockSpec((tm, B_WIDTH), lambda i: (i, 0)),
            pl.BlockSpec((tm, gw), lambda i: (i, ga_blk)),
            pl.BlockSpec((tm, gw), lambda i: (i, ga_blk + 1)),
            pl.BlockSpec((tm, gw), lambda i: (i, gb_blk)),
            pl.BlockSpec((tm, gw), lambda i: (i, gb_blk + 1)),
            pl.BlockSpec((tm, D_MODEL), lambda i: (i, 0)),
            _const_spec((A_WIDTH, D_MODEL)),
            _const_spec((B_WIDTH, D_MODEL)),
            _const_spec((D_MODEL, D_MODEL)),
            _const_spec((1, D_MODEL)),
            _const_spec((D_MODEL, 2 * ROUTER_COLS)),
            _const_spec((1, ROUTER_COLS)),
        ],
        out_specs=[
            pl.BlockSpec((tm, D_MODEL), lambda i: (i, 0)),
            pl.BlockSpec((tm, D_MODEL // 2), lambda i: (i, 0)),
            pl.BlockSpec((tm, ROUTER_COLS), lambda i: (i, 0)),
        ],
        out_shape=[
            jax.ShapeDtypeStruct((t_rows, D_MODEL), F32),
            jax.ShapeDtypeStruct((t_rows, D_MODEL // 2), jnp.uint32),
            jax.ShapeDtypeStruct((t_rows, ROUTER_COLS), F32),
        ],
        compiler_params=_params(("parallel",)),
        name="merge_route",
    )(oa, ob, proj2d, proj2d, proj2d, proj2d, x2d, wa, wb, wo, norm_ffn, wr, br)


def _route_kernel(logits_ref, info_ref, e1_ref, e2_ref):
    logits = logits_ref[...]
    e1, e2, w1, w2 = _route(logits)
    lane = lax.broadcasted_iota(jnp.int32, logits.shape, 1)
    info_ref[...] = jnp.where(lane == 0, e1, jnp.where(lane == 1, e2, jnp.where(lane == 2, w1,
                              jnp.where(lane == 3, w2, 0.0))))
    row = lax.broadcasted_iota(jnp.int32, logits.shape, 0)
    on_diag = lane == (row & (LANES - 1))
    groups = logits.shape[0] // LANES
    for e, ref in ((e1, e1_ref), (e2, e2_ref)):
        ref[...] = jnp.sum(jnp.where(on_diag, e, 0.0).reshape(groups, LANES, LANES), axis=1)


def _route_call(logits, *, tm):
    t_rows = logits.shape[0]
    dense = jax.ShapeDtypeStruct((t_rows // LANES, LANES), F32)
    return pl.pallas_call(
        _route_kernel,
        grid=(t_rows // tm,),
        in_specs=[pl.BlockSpec((tm, ROUTER_COLS), lambda i: (i, 0))],
        out_specs=[pl.BlockSpec((tm, ROUTER_COLS), lambda i: (i, 0)),
                   pl.BlockSpec((tm // LANES, LANES), lambda i: (i, 0)),
                   pl.BlockSpec((tm // LANES, LANES), lambda i: (i, 0))],
        out_shape=[jax.ShapeDtypeStruct((t_rows, ROUTER_COLS), F32), dense, dense],
        compiler_params=_params(("parallel",)),
        name="route",
    )(logits)


SEG_NTILES = 2 * N_EXPERTS


def _positions_kernel(e1_ref, e2_ref, pos1_ref, pos2_ref, te_ref, seg_ref):
    e1 = e1_ref[...]
    e2 = e2_ref[...]
    rows = e1.shape[0]
    r_i = lax.broadcasted_iota(jnp.int32, (LANES, LANES), 0)
    c_i = lax.broadcasted_iota(jnp.int32, (LANES, LANES), 1)
    upper = (r_i < c_i).astype(BF16)
    rr = lax.broadcasted_iota(jnp.int32, (rows, rows), 0)
    rc = lax.broadcasted_iota(jnp.int32, (rows, rows), 1)
    lower = (rc < rr).astype(BF16)
    seg_lane = lax.broadcasted_iota(jnp.int32, seg_ref.shape, 1)
    tile_row = lax.broadcasted_iota(jnp.int32, te_ref.shape, 1).astype(F32) * MOE_TILE
    base = jnp.zeros((1, 1), F32)
    pos1 = jnp.zeros(e1.shape, F32)
    pos2 = jnp.zeros(e1.shape, F32)
    seg = jnp.zeros(seg_ref.shape, F32)
    tile_expert = jnp.zeros(te_ref.shape, F32)
    for e in range(N_EXPERTS):
        m1 = e1 == e
        m2 = e2 == e
        m = jnp.where(m1 | m2, 1.0, 0.0)
        lane_pre = jnp.dot(m.astype(BF16), upper, preferred_element_type=F32)
        row_tot = jnp.broadcast_to(jnp.sum(m, axis=-1, keepdims=True), m.shape)
        row_pre = jnp.dot(lower, row_tot.astype(BF16), preferred_element_type=F32)
        total = jnp.sum(row_tot[:, 0:1], axis=0, keepdims=True)
        p = base + row_pre + lane_pre
        pos1 = jnp.where(m1, p, pos1)
        pos2 = jnp.where(m2, p, pos2)
        padded = jnp.floor((total + (MOE_TILE - 1)) * (1.0 / MOE_TILE)) * MOE_TILE
        base = base + padded
        seg = seg + jnp.where(seg_lane == e, base, 0.0) + jnp.where(seg_lane == N_EXPERTS + e, total, 0.0)
        tile_expert = tile_expert + jnp.where(tile_row >= base, 1.0, 0.0)
    seg = seg + jnp.where(seg_lane == SEG_NTILES, base * (1.0 / MOE_TILE), 0.0)
    pos1_ref[...] = pos1.astype(jnp.int32)
    pos2_ref[...] = pos2.astype(jnp.int32)
    te_ref[...] = jnp.minimum(tile_expert, N_EXPERTS - 1).astype(jnp.int32)
    seg_ref[...] = seg.astype(jnp.int32)


def _positions(e1, e2, n_tiles):
    rows = e1.shape[0]
    ntp = -(-n_tiles // LANES) * LANES
    full = lambda shape: pl.BlockSpec(shape, lambda: (0,) * len(shape))
    return pl.pallas_call(
        _positions_kernel,
        in_specs=[full((rows, LANES)), full((rows, LANES))],
        out_specs=[full((rows, LANES)), full((rows, LANES)), full((1, ntp)), full((1, LANES))],
        out_shape=[
            jax.ShapeDtypeStruct((rows, LANES), jnp.int32),
            jax.ShapeDtypeStruct((rows, LANES), jnp.int32),
            jax.ShapeDtypeStruct((1, ntp), jnp.int32),
            jax.ShapeDtypeStruct((1, LANES), jnp.int32),
        ],
        compiler_params=pltpu.CompilerParams(vmem_limit_bytes=VMEM_LIMIT),
        name="positions",
    )(e1, e2)


def _dispatch_kernel(seg_ref, pos1_ref, pos2_ref, u_ref, xs_ref, z_ref, zsem, sem, *, tb):
    i = pl.program_id(0)

    def zero_copy(e):
        end = seg_ref[e]
        start = pl.multiple_of(end - MOE_TILE, MOE_TILE)
        return pltpu.make_async_copy(z_ref, xs_ref.at[pl.ds(start, MOE_TILE)], zsem)

    def nonempty(e):
        return seg_ref[e] > (seg_ref[e - 1] if e > 0 else 0)

    n_tiles = xs_ref.shape[0] // MOE_TILE

    def tail_copy(k):
        start = pl.multiple_of((seg_ref[SEG_NTILES] + k) * MOE_TILE, MOE_TILE)
        return pltpu.make_async_copy(z_ref, xs_ref.at[pl.ds(start, MOE_TILE)], zsem)

    def tail_exists(k):
        return seg_ref[SEG_NTILES] + k < n_tiles

    @pl.when(i == 0)
    def _():
        z_ref[...] = jnp.zeros(z_ref.shape, z_ref.dtype)
        for e in range(N_EXPERTS):
            @pl.when(nonempty(e))
            def _():
                zero_copy(e).start()

            @pl.when(tail_exists(e))
            def _():
                tail_copy(e).start()
        for e in range(N_EXPERTS):
            @pl.when(nonempty(e))
            def _():
                zero_copy(e).wait()

            @pl.when(tail_exists(e))
            def _():
                tail_copy(e).wait()

    def row_copy(t, pos_ref):
        return pltpu.make_async_copy(u_ref.at[pl.ds(t, 1)], xs_ref.at[pl.ds(pos_ref[t], 1)], sem)

    def issue(t, carry):
        row_copy(t, pos1_ref).start()
        row_copy(t, pos2_ref).start()
        return carry

    lax.fori_loop(0, tb, issue, 0, unroll=8)

    for _ in range(2):
        pltpu.make_async_copy(u_ref, xs_ref.at[pl.ds(0, tb)], sem).wait()


def _dispatch(seg, pos1, pos2, u2p, n_tiles, *, tb):
    t_rows, width = u2p.shape
    return pl.pallas_call(
        functools.partial(_dispatch_kernel, tb=tb),
        grid_spec=pltpu.PrefetchScalarGridSpec(
            num_scalar_prefetch=1,
            grid=(t_rows // tb,),
            in_specs=[
                pl.BlockSpec((tb,), lambda i, seg: (i,), memory_space=pltpu.SMEM),
                pl.BlockSpec((tb,), lambda i, seg: (i,), memory_space=pltpu.SMEM),
                pl.BlockSpec((tb, width), lambda i, seg: (i, 0)),
            ],
            out_specs=pl.BlockSpec(memory_space=pl.ANY),
            scratch_shapes=[
                pltpu.VMEM((MOE_TILE, width), u2p.dtype),
                pltpu.SemaphoreType.DMA(()),
                pltpu.SemaphoreType.DMA(()),
            ],
        ),
        out_shape=jax.ShapeDtypeStruct((n_tiles * MOE_TILE, width), u2p.dtype),
        compiler_params=_params(("arbitrary",)),
        name="dispatch",
    )(seg, pos1, pos2, u2p)


def _moe_kernel(te_ref, seg_ref, x_ref, wg_ref, wu_ref, wd_ref, y_ref):
    i = pl.program_id(0)

    @pl.when(i < seg_ref[SEG_NTILES])
    def _():
        xw = x_ref[...]
        half = D_MODEL // 2
        lo = pltpu.bitcast(xw << 16, F32).astype(BF16)
        hi = pltpu.bitcast(xw & jnp.uint32(0xFFFF0000), F32).astype(BF16)
        hg = (jnp.dot(lo, wg_ref[0, :half, :], preferred_element_type=F32)
              + jnp.dot(hi, wg_ref[0, half:, :], preferred_element_type=F32))
        hu = (jnp.dot(lo, wu_ref[0, :half, :], preferred_element_type=F32)
              + jnp.dot(hi, wu_ref[0, half:, :], preferred_element_type=F32))
        hid = hg * _sigmoid(hg) * hu
        y_ref[...] = jnp.dot(hid.astype(BF16), wd_ref[0], preferred_element_type=F32)

    @pl.when(i >= seg_ref[SEG_NTILES])
    def _():
        y_ref[...] = jnp.zeros(y_ref.shape, y_ref.dtype)


def _moe(tile_expert, seg, xs, wg, wu, wd):
    n_tiles = xs.shape[0] // MOE_TILE

    def row_map(i, te, seg):
        return (jnp.minimum(i, seg[SEG_NTILES] - 1), 0)

    def w_map(i, te, seg):
        return (te[jnp.minimum(i, seg[SEG_NTILES] - 1)], 0, 0)

    return pl.pallas_call(
        _moe_kernel,
        grid_spec=pltpu.PrefetchScalarGridSpec(
            num_scalar_prefetch=2,
            grid=(n_tiles,),
            in_specs=[
                pl.BlockSpec((MOE_TILE, D_MODEL // 2), row_map),
                pl.BlockSpec((1, D_MODEL, EXPERT_FF), w_map),
                pl.BlockSpec((1, D_MODEL, EXPERT_FF), w_map),
                pl.BlockSpec((1, EXPERT_FF, D_MODEL), w_map),
            ],
            out_specs=pl.BlockSpec((MOE_TILE, D_MODEL), lambda i, te, seg: (i, 0)),
        ),
        out_shape=jax.ShapeDtypeStruct((n_tiles * MOE_TILE, D_MODEL), F32),
        compiler_params=_params(("arbitrary",)),
        name="moe",
    )(tile_expert, seg, xs, wg, wu, wd)


def _final_kernel(pos1_ref, pos2_ref, nxt1_ref, nxt2_ref, h2_ref, info_ref, g_ref, ys_ref, o_ref, ybuf, sems, *, tm):
    i = pl.program_id(0)
    n = pl.num_programs(0)
    slot = i % 2

    def gather(p1_ref, p2_ref, s):
        def issue(t, carry):
            pltpu.make_async_copy(ys_ref.at[pl.ds(p1_ref[t], 1)], ybuf.at[s, 0, pl.ds(t, 1)], sems.at[s]).start()
            pltpu.make_async_copy(ys_ref.at[pl.ds(p2_ref[t], 1)], ybuf.at[s, 1, pl.ds(t, 1)], sems.at[s]).start()
            return carry

        lax.fori_loop(0, tm, issue, 0, unroll=8)

    @pl.when(i == 0)
    def _():
        gather(pos1_ref, pos2_ref, 0)

    @pl.when(i + 1 < n)
    def _():
        gather(nxt1_ref, nxt2_ref, 1 - slot)

    for k in range(2):
        pltpu.make_async_copy(ys_ref.at[pl.ds(0, tm)], ybuf.at[slot, k], sems.at[slot]).wait()

    info = info_ref[...]
    h = h2_ref[...] + info[:, 2:3] * ybuf[slot, 0] + info[:, 3:4] * ybuf[slot, 1]
    ms = jnp.mean(h * h, axis=-1, keepdims=True)
    o_ref[...] = h * lax.rsqrt(ms + RMS_EPS) * g_ref[...]


def _final(pos1, pos2, h2, info, final_norm, ys, *, tm):
    t_rows = h2.shape[0]
    n_steps = t_rows // tm
    nxt = lambda i: (jnp.minimum(i + 1, n_steps - 1),)
    return pl.pallas_call(
        functools.partial(_final_kernel, tm=tm),
        grid=(n_steps,),
        in_specs=[
            pl.BlockSpec((tm,), lambda i: (i,), memory_space=pltpu.SMEM),
            pl.BlockSpec((tm,), lambda i: (i,), memory_space=pltpu.SMEM),
            pl.BlockSpec((tm,), nxt, memory_space=pltpu.SMEM),
            pl.BlockSpec((tm,), nxt, memory_space=pltpu.SMEM),
            pl.BlockSpec((tm, D_MODEL), lambda i: (i, 0)),
            pl.BlockSpec((tm, ROUTER_COLS), lambda i: (i, 0)),
            pl.BlockSpec((1, D_MODEL), lambda i: (0, 0)),
            pl.BlockSpec(memory_space=pl.ANY),
        ],
        out_specs=pl.BlockSpec((tm, D_MODEL), lambda i: (i, 0)),
        out_shape=jax.ShapeDtypeStruct((t_rows, D_MODEL), F32),
        scratch_shapes=[pltpu.VMEM((2, 2, tm, D_MODEL), F32), pltpu.SemaphoreType.DMA((2,))],
        compiler_params=_params(("arbitrary",)),
        name="final",
    )(pos1, pos2, pos1, pos2, h2, info, final_norm, ys)


def _rope_tables(row, col):
    n_freq = HEAD_DIM // 4
    inv_freq = ROPE_THETA ** (-jnp.arange(n_freq, dtype=F32) / n_freq)
    ra = row.astype(F32)[:, None] * inv_freq
    ca = col.astype(F32)[:, None] * inv_freq
    cos = jnp.concatenate([jnp.cos(ra), jnp.cos(ca), jnp.cos(ra), jnp.cos(ca)], axis=-1)
    sin = jnp.concatenate([-jnp.sin(ra), -jnp.sin(ca), jnp.sin(ra), jnp.sin(ca)], axis=-1)
    return cos, sin


def _gain_rows(g):
    gs = _swap_rotary_sections(g.astype(F32))
    return jnp.stack([gs, jnp.roll(gs, HEAD_DIM // 2)])


def _split_bf16(w):
    hi = w.astype(BF16)
    lo = (w - hi.astype(F32)).astype(BF16)
    return hi, lo


def _tile(n, pref):
    t = min(n, pref)
    assert n % t == 0, (n, pref)
    return t


def _encode_group(x, shared):
    (norm_mix, w_in, col_scale, q_norm, k_norm, meta_kv, tab, tabm, sink_rows, wa, wb, wo, norm_ffn, wr, br,
     wg, wu, wd, final_norm) = shared
    b, s, _ = x.shape
    t_rows = b * s
    x2d = x.reshape(t_rows, D_MODEL)
    tok = jnp.arange(s)
    cos, sin = _rope_tables(tok // GRID_W, tok % GRID_W)
    proj2d = _in_proj(x2d, norm_mix, w_in, col_scale, cos, sin, q_norm, k_norm, tm=_tile(s, 1024))
    proj = proj2d.reshape(b, s, IN_COLS)
    ka_m, va_m, kb_m, vb_m = meta_kv
    oa = _global_attn(proj, ka_m, va_m, tq=_tile(s, 256), ck=_tile(s, 1024))
    nblk = s // Q_BLOCK
    ob = _window_attn(proj, kb_m, vb_m, tab, tabm, sink_rows, nq=math.gcd(nblk, 8))
    h2, u2p, logits = _merge_route(oa.reshape(t_rows, A_WIDTH), ob.reshape(t_rows, B_WIDTH), proj2d, x2d,
                                   wa, wb, wo, norm_ffn, wr, br, tm=_tile(t_rows, 512))
    info, e1, e2 = _route_call(logits, tm=_tile(t_rows, 2048))
    n_tiles = 2 * t_rows // MOE_TILE + N_EXPERTS
    pos1, pos2, tile_expert, seg = _positions(e1, e2, n_tiles)
    pos1 = pos1.reshape(t_rows)
    pos2 = pos2.reshape(t_rows)
    seg = seg.reshape(LANES)
    xs = _dispatch(seg, pos1, pos2, u2p, n_tiles, tb=_tile(t_rows, 4096))
    ys = _moe(tile_expert.reshape(-1)[:n_tiles], seg, xs, wg, wu, wd)
    out = _final(pos1, pos2, h2, info, final_norm, ys, tm=_tile(t_rows, 256))
    return out.reshape(b, s, D_MODEL)


def kernel(x_prompt, x_sample, meta_tokens, rel_bias, final_norm, norm_mix, w_in, q_norm, k_norm, sink,
           w_branch_a, w_branch_b, w_out, norm_ffn, w_router_g, b_router_g, w_router_e, b_router_e,
           w_gate, w_up, w_down):
    assert norm_mix.shape[0] == 1, "single-layer encoder"
    rot_cols = (A_HEADS + A_KV_HEADS) * HEAD_DIM
    w_in_b = w_in[0].astype(BF16)
    w_in_b = lax.dynamic_update_slice(w_in_b, _swap_rotary_sections(w_in_b[:, :rot_cols]), (0, 0))
    norm_mix2 = norm_mix[0].reshape(1, D_MODEL)
    q_norm2 = _gain_rows(q_norm[0])
    k_norm2 = _gain_rows(k_norm[0])
    col = jnp.arange(IN_COLS)
    is_qb = (col >= COL_QB * HEAD_DIM) & (col < COL_KB * HEAD_DIM)
    col_scale = jnp.where(is_qb, SCORE_SCALE * LOG2E, 1.0).astype(F32).reshape(1, IN_COLS)

    cos_m, sin_m = _rope_tables(jnp.full((N_META,), -1), jnp.arange(N_META))
    proj_m = _in_proj(meta_tokens, norm_mix2, w_in_b, col_scale, cos_m, sin_m, q_norm2, k_norm2, tm=N_META)

    def meta_heads(c0):
        blk = proj_m[:, c0 * HEAD_DIM:(c0 + A_KV_HEADS) * HEAD_DIM]
        return blk.reshape(N_META, A_KV_HEADS, HEAD_DIM).transpose(1, 0, 2)

    meta_kv = tuple(meta_heads(c) for c in (COL_KA, COL_VA, COL_KB, COL_VB))

    tab, tabm = _bias_tables(rel_bias)
    sink_rows = jnp.repeat(sink[0].astype(F32), Q_BLOCK).reshape(B_KV_HEADS, REP * Q_BLOCK, 1)

    wr_full = jnp.zeros((D_MODEL, ROUTER_COLS), F32)
    wr_full = wr_full.at[:, :N_GROUPS].set(w_router_g[0]).at[:, N_GROUPS:N_GROUPS + N_EXPERTS].set(w_router_e[0])
    wr_hi, wr_lo = _split_bf16(wr_full)
    wr = jnp.concatenate([wr_hi, wr_lo], axis=1)
    br = jnp.zeros((1, ROUTER_COLS), F32)
    br = br.at[0, :N_GROUPS].set(b_router_g[0]).at[0, N_GROUPS:N_GROUPS + N_EXPERTS].set(b_router_e[0])

    shared = (norm_mix2, w_in_b, col_scale, q_norm2, k_norm2, meta_kv, tab, tabm, sink_rows,
              w_branch_a[0].astype(BF16), w_branch_b[0].astype(BF16), w_out[0].astype(BF16),
              norm_ffn[0].reshape(1, D_MODEL), wr, br,
              w_gate[0].astype(BF16), w_up[0].astype(BF16), w_down[0].astype(BF16),
              final_norm.reshape(1, D_MODEL))
    return (_encode_group(x_prompt, shared), _encode_group(x_sample, shared))
```

```python
import functools
import math

import numpy as np
import jax
import jax.numpy as jnp
from jax import lax
from jax.experimental import pallas as pl
from jax.experimental.pallas import tpu as pltpu

F32 = jnp.float32
BF16 = jnp.bfloat16

D_MODEL = 2048
HEAD_DIM = 128
A_HEADS = 8
A_KV_HEADS = 2
B_HEADS = 8
B_KV_HEADS = 2
REP = A_HEADS // A_KV_HEADS
A_WIDTH = A_HEADS * HEAD_DIM
B_WIDTH = B_HEADS * HEAD_DIM
KV_WIDTH = A_KV_HEADS * HEAD_DIM
IN_COLS = A_WIDTH + 2 * KV_WIDTH + B_WIDTH + 2 * KV_WIDTH + 2 * D_MODEL
Q_BLOCK = 128
WINDOW = 128
N_META = 16
GRID_W = 64
ROPE_THETA = 10000.0
N_BUCKETS = 32
MAX_DISTANCE = 128
N_GROUPS = 4
EXPERTS_PER_GROUP = 4
N_EXPERTS = N_GROUPS * EXPERTS_PER_GROUP
EXPERT_FF = 1024
RMS_EPS = 1e-6
NEG_INF = -1e30
SCORE_SCALE = HEAD_DIM ** -0.5
LOG2E = math.log2(math.e)

COL_QA = 0
COL_KA = A_WIDTH // HEAD_DIM
COL_VA = COL_KA + A_KV_HEADS
COL_QB = COL_VA + A_KV_HEADS
COL_KB = COL_QB + B_HEADS
COL_VB = COL_KB + B_KV_HEADS
COL_GA = COL_VB + B_KV_HEADS
COL_GB = COL_GA + D_MODEL // HEAD_DIM

LANES = 128
VMEM_LIMIT = 56 * 1024 * 1024
MOE_TILE = 512
ROUTER_COLS = 128

_NT = (((1,), (1,)), ((), ()))


def _params(sem, vmem=VMEM_LIMIT):
    return pltpu.CompilerParams(dimension_semantics=sem, vmem_limit_bytes=vmem)


def _const_spec(shape):
    nd = len(shape)
    return pl.BlockSpec(shape, lambda *_: (0,) * nd, pipeline_mode=pl.Buffered(1))


def _swap_rotary_sections(w):
    lead = w.shape[:-1]
    heads = w.shape[-1] // HEAD_DIM
    nd = len(lead)
    w5 = w.reshape(lead + (heads, 2, 2, HEAD_DIM // 4))
    return jnp.swapaxes(w5, nd + 1, nd + 2).reshape(w.shape)


def _norm_rope(a, cg, sg):
    ones = jnp.ones((HEAD_DIM, HEAD_DIM), BF16)
    ssq = jnp.dot((a * a).astype(BF16), ones, preferred_element_type=F32)
    rinv = lax.rsqrt(ssq * (1.0 / HEAD_DIM) + RMS_EPS)
    return rinv * (a * cg + pltpu.roll(a, HEAD_DIM // 2, 1) * sg)


def _in_proj_kernel(x_ref, g_ref, w_ref, cs_ref, cos_ref, sin_ref, qn_ref, kn_ref, o_ref, u_ref, *, sub):
    j = pl.program_id(1)
    row_tiles = [slice(k * sub, (k + 1) * sub) for k in range(x_ref.shape[0] // sub)]

    def project(r):
        return jnp.dot(u_ref[r, :], w_ref[...], preferred_element_type=F32) * cs_ref[...]

    def rope_factors(gain_ref, scale, r):
        return cos_ref[r, :] * (gain_ref[0:1, :] * scale), sin_ref[r, :] * (gain_ref[1:2, :] * scale)

    @pl.when(j == 0)
    def _():
        rot = (A_HEADS + A_KV_HEADS) * HEAD_DIM
        for r in row_tiles:
            x = x_ref[r, :]
            ms = jnp.mean(x * x, axis=-1, keepdims=True)
            u_ref[r, :] = (x * lax.rsqrt(ms + RMS_EPS) * g_ref[...]).astype(BF16)
            acc = project(r)
            q_factors = rope_factors(qn_ref, SCORE_SCALE * LOG2E, r)
            k_factors = rope_factors(kn_ref, 1.0, r)
            for h in range(A_HEADS + A_KV_HEADS):
                sl = slice(h * HEAD_DIM, (h + 1) * HEAD_DIM)
                cg, sg = q_factors if h < A_HEADS else k_factors
                o_ref[r, sl] = _norm_rope(acc[:, sl], cg, sg).astype(BF16)
            o_ref[r, rot:] = acc[:, rot:].astype(BF16)

    @pl.when(j >= 1)
    def _():
        for r in row_tiles:
            o_ref[r, :] = project(r).astype(BF16)


IN_PROJ_TN = IN_COLS // 4


def _in_proj(x2d, norm_g, w_bf16, col_scale, cos, sin, q_norm, k_norm, *, tm):
    t_rows = x2d.shape[0]
    tn = IN_PROJ_TN
    assert tn % HEAD_DIM == 0 and tn >= (A_HEADS + A_KV_HEADS) * HEAD_DIM
    pos_tiles = cos.shape[0] // tm
    return pl.pallas_call(
        functools.partial(_in_proj_kernel, sub=min(tm, 512)),
        grid=(t_rows // tm, IN_COLS // tn),
        in_specs=[
            pl.BlockSpec((tm, D_MODEL), lambda i, j: (i, 0)),
            pl.BlockSpec((1, D_MODEL), lambda i, j: (0, 0)),
            pl.BlockSpec((D_MODEL, tn), lambda i, j: (0, j)),
            pl.BlockSpec((1, tn), lambda i, j: (0, j)),
            pl.BlockSpec((tm, HEAD_DIM), lambda i, j: (i % pos_tiles, 0)),
            pl.BlockSpec((tm, HEAD_DIM), lambda i, j: (i % pos_tiles, 0)),
            pl.BlockSpec((2, HEAD_DIM), lambda i, j: (0, 0)),
            pl.BlockSpec((2, HEAD_DIM), lambda i, j: (0, 0)),
        ],
        out_specs=pl.BlockSpec((tm, tn), lambda i, j: (i, j)),
        out_shape=jax.ShapeDtypeStruct((t_rows, IN_COLS), BF16),
        scratch_shapes=[pltpu.VMEM((tm, D_MODEL), BF16)],
        compiler_params=_params(("parallel", "arbitrary")),
        name="in_proj",
    )(x2d, norm_g, w_bf16, col_scale, cos, sin, q_norm, k_norm)


def _stack_heads(q):
    return jnp.concatenate([q[:, r * HEAD_DIM:(r + 1) * HEAD_DIM] for r in range(REP)], axis=0)


def _global_attn_kernel(q_ref, k_ref, v_ref, km_ref, vm_ref, o_ref, vx_ref, vmx_ref, *, tq, ck):
    @pl.when(pl.program_id(2) == 0)
    def _():
        vx_ref[:, :HEAD_DIM] = v_ref[0]
        vx_ref[:, HEAD_DIM:] = jnp.ones((v_ref.shape[1], HEAD_DIM), BF16)
        vmx_ref[:, :HEAD_DIM] = vm_ref[0]
        vmx_ref[:, HEAD_DIM:] = jnp.ones((N_META, HEAD_DIM), BF16)

    qs = _stack_heads(q_ref[0])
    n_chunks = k_ref.shape[1] // ck

    def scores(c):
        return lax.dot_general(qs, k_ref[0, c * ck:(c + 1) * ck, :], _NT, preferred_element_type=F32)

    s = lax.dot_general(qs, km_ref[0], _NT, preferred_element_type=F32)
    s_next = scores(0)
    m = jnp.max(s, axis=-1, keepdims=True)
    acc = jnp.dot(jnp.exp2(s - m).astype(BF16), vmx_ref[...], preferred_element_type=F32)
    for c in range(n_chunks):
        s = s_next
        if c + 1 < n_chunks:
            s_next = scores(c + 1)
        m_new = jnp.maximum(m, jnp.max(s, axis=-1, keepdims=True))
        p = jnp.exp2(s - m_new).astype(BF16)
        acc = jnp.exp2(m - m_new) * acc + jnp.dot(p, vx_ref[c * ck:(c + 1) * ck, :],
                                                  preferred_element_type=F32)
        m = m_new
    o = acc[:, :HEAD_DIM] / acc[:, HEAD_DIM:]
    for r in range(REP):
        o_ref[0, :, r * HEAD_DIM:(r + 1) * HEAD_DIM] = o[r * tq:(r + 1) * tq].astype(BF16)


def _global_attn(proj, km, vm, *, tq, ck):
    b, s, _ = proj.shape
    gw = REP * HEAD_DIM
    return pl.pallas_call(
        functools.partial(_global_attn_kernel, tq=tq, ck=ck),
        grid=(b, A_KV_HEADS, s // tq),
        in_specs=[
            pl.BlockSpec((1, tq, gw), lambda bi, g, i: (bi, i, g)),
            pl.BlockSpec((1, s, HEAD_DIM), lambda bi, g, i: (bi, 0, COL_KA + g)),
            pl.BlockSpec((1, s, HEAD_DIM), lambda bi, g, i: (bi, 0, COL_VA + g)),
            pl.BlockSpec((1, N_META, HEAD_DIM), lambda bi, g, i: (g, 0, 0)),
            pl.BlockSpec((1, N_META, HEAD_DIM), lambda bi, g, i: (g, 0, 0)),
        ],
        out_specs=pl.BlockSpec((1, tq, gw), lambda bi, g, i: (bi, i, g)),
        out_shape=jax.ShapeDtypeStruct((b, s, A_WIDTH), BF16),
        scratch_shapes=[pltpu.VMEM((s, 2 * HEAD_DIM), BF16), pltpu.VMEM((N_META, 2 * HEAD_DIM), BF16)],
        compiler_params=_params(("parallel", "parallel", "arbitrary")),
        name="global_attn",
    )(proj, proj, proj, km, vm)


def _t5_bucket_np(rel):
    nb = N_BUCKETS // 2
    max_exact = nb // 2
    bucket = np.where(rel > 0, nb, 0)
    n = np.abs(rel)
    nf = np.maximum(n, 1).astype(np.float32)
    large = max_exact + (np.log(nf / np.float32(max_exact)) / np.float32(math.log(MAX_DISTANCE / max_exact))
                         * np.float32(nb - max_exact)).astype(np.int32)
    large = np.minimum(large, nb - 1)
    return (bucket + np.where(n < max_exact, n, large)).astype(np.int32)


def _bucket_maps():
    i = np.arange(Q_BLOCK)[:, None]
    j = np.arange(3 * Q_BLOCK)[None, :]
    real = []
    for off in range(3):
        rel = j - off * Q_BLOCK - i
        real.append(np.where(np.abs(rel) <= WINDOW, _t5_bucket_np(rel), -1))
    m = np.arange(N_META)[None, :]
    first = _t5_bucket_np(m - (N_META + i))
    later = _t5_bucket_np(m - (N_META + i + Q_BLOCK))
    meta = [first, later, later]
    return np.stack(real).astype(np.int32), np.stack(meta).astype(np.int32)


def _bias_table_kernel(rb_ref, bm_ref, bmm_ref, o_ref, om_ref):
    h = pl.program_id(1)
    bm = bm_ref[0]
    bmm = bmm_ref[0]
    acc = jnp.full(bm.shape, NEG_INF, F32)
    accm = jnp.full(bmm.shape, NEG_INF, F32)
    for k in range(N_BUCKETS):
        val = rb_ref[k, h] * LOG2E
        acc = jnp.where(bm == k, val, acc)
        accm = jnp.where(bmm == k, val, accm)
    o_ref[0, 0] = acc
    om_ref[0, 0] = accm


def _bias_tables(rel_bias):
    bm, bmm = _bucket_maps()
    kw = 3 * Q_BLOCK
    tab, tabm = pl.pallas_call(
        _bias_table_kernel,
        grid=(3, B_HEADS),
        in_specs=[
            pl.BlockSpec(memory_space=pltpu.SMEM),
            pl.BlockSpec((1, Q_BLOCK, kw), lambda v, h: (v, 0, 0)),
            pl.BlockSpec((1, Q_BLOCK, N_META), lambda v, h: (v, 0, 0)),
        ],
        out_specs=[
            pl.BlockSpec((1, 1, Q_BLOCK, kw), lambda v, h: (v, h, 0, 0)),
            pl.BlockSpec((1, 1, Q_BLOCK, N_META), lambda v, h: (v, h, 0, 0)),
        ],
        out_shape=[
            jax.ShapeDtypeStruct((3, B_HEADS, Q_BLOCK, kw), F32),
            jax.ShapeDtypeStruct((3, B_HEADS, Q_BLOCK, N_META), F32),
        ],
        compiler_params=_params(("arbitrary", "arbitrary")),
        name="bias_tables",
    )(rel_bias, jnp.asarray(bm), jnp.asarray(bmm))
    rows = REP * Q_BLOCK
    return (tab.reshape(3, B_KV_HEADS, rows, kw), tabm.reshape(3, B_KV_HEADS, rows, N_META))


def _window_attn_kernel(q_ref, k_ref, v_ref, km_ref, vm_ref, tab_ref, tabm_ref, sink_ref, o_ref, *, nq):
    jb = pl.program_id(2)
    nblk = k_ref.shape[1] // Q_BLOCK
    km = km_ref[0]
    vm = vm_ref[0]
    sink = sink_ref[0] * LOG2E
    for t in range(nq):
        n = jb * nq + t
        qs = _stack_heads(q_ref[0, t * Q_BLOCK:(t + 1) * Q_BLOCK, :])
        var = jnp.where(n == 0, 0, jnp.where(n == nblk - 1, 2, 1))
        start = pl.multiple_of(jnp.clip(n - 1, 0, nblk - 3) * Q_BLOCK, Q_BLOCK)
        kb = k_ref[0, pl.ds(start, 3 * Q_BLOCK), :]
        vb = v_ref[0, pl.ds(start, 3 * Q_BLOCK), :]
        s = lax.dot_general(qs, kb, _NT, preferred_element_type=F32) + tab_ref[var, 0]
        sm = lax.dot_general(qs, km, _NT, preferred_element_type=F32) + tabm_ref[var, 0]
        m = jnp.maximum(jnp.maximum(jnp.max(s, axis=-1, keepdims=True),
                                    jnp.max(sm, axis=-1, keepdims=True)), sink)
        p = jnp.exp2(s - m)
        pm = jnp.exp2(sm - m)
        l = jnp.sum(p, axis=-1, keepdims=True) + jnp.sum(pm, axis=-1, keepdims=True) + jnp.exp2(sink - m)
        o = (jnp.dot(p.astype(BF16), vb, preferred_element_type=F32)
             + jnp.dot(pm.astype(BF16), vm, preferred_element_type=F32)) / l
        for r in range(REP):
            o_ref[0, t * Q_BLOCK:(t + 1) * Q_BLOCK, r * HEAD_DIM:(r + 1) * HEAD_DIM] = (
                o[r * Q_BLOCK:(r + 1) * Q_BLOCK].astype(BF16))


def _window_attn(proj, km, vm, tab, tabm, sink_rows, *, nq):
    b, s, _ = proj.shape
    assert s // Q_BLOCK >= 3 and (s // Q_BLOCK) % nq == 0
    gw = REP * HEAD_DIM
    rows = REP * Q_BLOCK
    kw = 3 * Q_BLOCK
    tq = nq * Q_BLOCK
    return pl.pallas_call(
        functools.partial(_window_attn_kernel, nq=nq),
        grid=(b, B_KV_HEADS, s // tq),
        in_specs=[
            pl.BlockSpec((1, tq, gw), lambda bi, g, i: (bi, i, COL_QB // REP + g)),
            pl.BlockSpec((1, s, HEAD_DIM), lambda bi, g, i: (bi, 0, COL_KB + g)),
            pl.BlockSpec((1, s, HEAD_DIM), lambda bi, g, i: (bi, 0, COL_VB + g)),
            pl.BlockSpec((1, N_META, HEAD_DIM), lambda bi, g, i: (g, 0, 0)),
            pl.BlockSpec((1, N_META, HEAD_DIM), lambda bi, g, i: (g, 0, 0)),
            pl.BlockSpec((3, 1, rows, kw), lambda bi, g, i: (0, g, 0, 0)),
            pl.BlockSpec((3, 1, rows, N_META), lambda bi, g, i: (0, g, 0, 0)),
            pl.BlockSpec((1, rows, 1), lambda bi, g, i: (g, 0, 0)),
        ],
        out_specs=pl.BlockSpec((1, tq, gw), lambda bi, g, i: (bi, i, g)),
        out_shape=jax.ShapeDtypeStruct((b, s, B_WIDTH), BF16),
        compiler_params=_params(("parallel", "parallel", "arbitrary")),
        name="window_attn",
    )(proj, proj, proj, km, vm, tab, tabm, sink_rows)


def _sigmoid(x):
    return 1.0 / (1.0 + jnp.exp(-x))


def _route(logits):
    lane = lax.broadcasted_iota(jnp.int32, logits.shape, 1).astype(F32)
    ninf = jnp.float32(-jnp.inf)
    big = jnp.float32(ROUTER_COLS)
    is_g = lane < N_GROUPS
    lg = jnp.where(is_g, logits, ninf)
    mg = jnp.max(lg, axis=-1, keepdims=True)
    gidx = jnp.min(jnp.where(lg == mg, lane, big), axis=-1, keepdims=True)
    p_top = 1.0 / jnp.sum(jnp.where(is_g, jnp.exp(lg - mg), 0.0), axis=-1, keepdims=True)
    lo = N_GROUPS + EXPERTS_PER_GROUP * gidx
    sel = (lane >= lo) & (lane < lo + EXPERTS_PER_GROUP)
    le = jnp.where(sel, logits, ninf)
    v1 = jnp.max(le, axis=-1, keepdims=True)
    i1 = jnp.min(jnp.where(sel & (le == v1), lane, big), axis=-1, keepdims=True)
    rest = sel & (lane != i1)
    le2 = jnp.where(rest, logits, ninf)
    v2 = jnp.max(le2, axis=-1, keepdims=True)
    i2 = jnp.min(jnp.where(rest & (le2 == v2), lane, big), axis=-1, keepdims=True)
    t = jnp.exp(v2 - v1)
    w1 = p_top / (1.0 + t)
    w2 = p_top * t / (1.0 + t)
    return i1 - N_GROUPS, i2 - N_GROUPS, w1, w2


def _merge_kernel(oa_ref, ob_ref, ga0_ref, ga1_ref, gb0_ref, gb1_ref, x_ref, wa_ref, wb_ref, wo_ref, gn_ref,
                  wr_ref, br_ref, h2_ref, u2_ref, logits_ref, *, sub):
    for st in range(oa_ref.shape[0] // sub):
        r = slice(st * sub, (st + 1) * sub)
        ya = jnp.dot(oa_ref[r, :], wa_ref[...], preferred_element_type=F32)
        yb = jnp.dot(ob_ref[r, :], wb_ref[...], preferred_element_type=F32)
        ga = jnp.concatenate([ga0_ref[r, :], ga1_ref[r, :]], axis=1).astype(F32)
        gb = jnp.concatenate([gb0_ref[r, :], gb1_ref[r, :]], axis=1).astype(F32)
        mixed = _sigmoid(ga) * ya + _sigmoid(gb) * yb
        h2 = x_ref[r, :] + jnp.dot(mixed.astype(BF16), wo_ref[...], preferred_element_type=F32)
        h2_ref[r, :] = h2
        ms = jnp.mean(h2 * h2, axis=-1, keepdims=True)
        u = h2 * lax.rsqrt(ms + RMS_EPS) * gn_ref[...]
        u_hi = u.astype(BF16)
        u_hi32 = u_hi.astype(F32)
        u_lo = (u - u_hi32).astype(BF16)
        lg = (jnp.dot(u_hi, wr_ref[...], preferred_element_type=F32)
              + jnp.dot(u_lo, wr_ref[...], preferred_element_type=F32))
        logits_ref[r, :] = lg[:, :ROUTER_COLS] + lg[:, ROUTER_COLS:] + br_ref[...]
        half = D_MODEL // 2
        lo_bits = pltpu.bitcast(u_hi32[:, :half], jnp.uint32) >> 16
        hi_bits = pltpu.bitcast(u_hi32[:, half:], jnp.uint32)
        u2_ref[r, :] = hi_bits | lo_bits


def _merge_route(oa, ob, proj2d, x2d, wa, wb, wo, norm_ffn, wr, br, *, tm):
    t_rows = x2d.shape[0]
    gw = D_MODEL // 2
    ga_blk = COL_GA * HEAD_DIM // gw
    gb_blk = COL_GB * HEAD_DIM // gw
    return pl.pallas_call(
        functools.partial(_merge_kernel, sub=min(tm, 256)),
        grid=(t_rows // tm,),
        in_specs=[
            pl.BlockSpec((tm, A_WIDTH), lambda i: (i, 0)),
            pl.BlockSpec((tm, B_WIDTH), lambda i: (i, 0)),
            pl.BlockSpec((tm, gw), lambda i: (i, ga_blk)),
            pl.BlockSpec((tm, gw), lambda i: (i, ga_blk + 1)),
            pl.BlockSpec((tm, gw), lambda i: (i, gb_blk)),
            pl.BlockSpec((tm, gw), lambda i: (i, gb_blk + 1)),
            pl.BlockSpec((tm, D_MODEL), lambda i: (i, 0)),
            _const_spec((A_WIDTH, D_MODEL)),
            _const_spec((B_WIDTH, D_MODEL)),
            _const_spec((D_MODEL, D_MODEL)),
            _const_spec((1, D_MODEL)),
            _const_spec((D_MODEL, 2 * ROUTER_COLS)),
            _const_spec((1, ROUTER_COLS)),
        ],
        out_specs=[
            pl.BlockSpec((tm, D_MODEL), lambda i: (i, 0)),
            pl.BlockSpec((tm, D_MODEL // 2), lambda i: (i, 0)),
            pl.BlockSpec((tm, ROUTER_COLS), lambda i: (i, 0)),
        ],
        out_shape=[
            jax.ShapeDtypeStruct((t_rows, D_MODEL), F32),
            jax.ShapeDtypeStruct((t_rows, D_MODEL // 2), jnp.uint32),
            jax.ShapeDtypeStruct((t_rows, ROUTER_COLS), F32),
        ],
        compiler_params=_params(("parallel",)),
        name="merge_route",
    )(oa, ob, proj2d, proj2d, proj2d, proj2d, x2d, wa, wb, wo, norm_ffn, wr, br)


def _route_kernel(logits_ref, info_ref, e1_ref, e2_ref):
    logits = logits_ref[...]
    e1, e2, w1, w2 = _route(logits)
    lane = lax.broadcasted_iota(jnp.int32, logits.shape, 1)
    info_ref[...] = jnp.where(lane == 0, e1, jnp.where(lane == 1, e2, jnp.where(lane == 2, w1,
                              jnp.where(lane == 3, w2, 0.0))))
    row = lax.broadcasted_iota(jnp.int32, logits.shape, 0)
    on_diag = lane == (row & (LANES - 1))
    groups = logits.shape[0] // LANES
    for e, ref in ((e1, e1_ref), (e2, e2_ref)):
        ref[...] = jnp.sum(jnp.where(on_diag, e, 0.0).reshape(groups, LANES, LANES), axis=1)


def _route_call(logits, *, tm):
    t_rows = logits.shape[0]
    dense = jax.ShapeDtypeStruct((t_rows // LANES, LANES), F32)
    return pl.pallas_call(
        _route_kernel,
        grid=(t_rows // tm,),
        in_specs=[pl.BlockSpec((tm, ROUTER_COLS), lambda i: (i, 0))],
        out_specs=[pl.BlockSpec((tm, ROUTER_COLS), lambda i: (i, 0)),
                   pl.BlockSpec((tm // LANES, LANES), lambda i: (i, 0)),
                   pl.BlockSpec((tm // LANES, LANES), lambda i: (i, 0))],
        out_shape=[jax.ShapeDtypeStruct((t_rows, ROUTER_COLS), F32), dense, dense],
        compiler_params=_params(("parallel",)),
        name="route",
    )(logits)


SEG_NTILES = 2 * N_EXPERTS


def _positions_kernel(e1_ref, e2_ref, pos1_ref, pos2_ref, te_ref, seg_ref):
    e1 = e1_ref[...]
    e2 = e2_ref[...]
    rows = e1.shape[0]
    r_i = lax.broadcasted_iota(jnp.int32, (LANES, LANES), 0)
    c_i = lax.broadcasted_iota(jnp.int32, (LANES, LANES), 1)
    upper = (r_i < c_i).astype(BF16)
    rr = lax.broadcasted_iota(jnp.int32, (rows, rows), 0)
    rc = lax.broadcasted_iota(jnp.int32, (rows, rows), 1)
    lower = (rc < rr).astype(BF16)
    seg_lane = lax.broadcasted_iota(jnp.int32, seg_ref.shape, 1)
    tile_row = lax.broadcasted_iota(jnp.int32, te_ref.shape, 1).astype(F32) * MOE_TILE
    base = jnp.zeros((1, 1), F32)
    pos1 = jnp.zeros(e1.shape, F32)
    pos2 = jnp.zeros(e1.shape, F32)
    seg = jnp.zeros(seg_ref.shape, F32)
    tile_expert = jnp.zeros(te_ref.shape, F32)
    for e in range(N_EXPERTS):
        m1 = e1 == e
        m2 = e2 == e
        m = jnp.where(m1 | m2, 1.0, 0.0)
        lane_pre = jnp.dot(m.astype(BF16), upper, preferred_element_type=F32)
        row_tot = jnp.broadcast_to(jnp.sum(m, axis=-1, keepdims=True), m.shape)
        row_pre = jnp.dot(lower, row_tot.astype(BF16), preferred_element_type=F32)
        total = jnp.sum(row_tot[:, 0:1], axis=0, keepdims=True)
        p = base + row_pre + lane_pre
        pos1 = jnp.where(m1, p, pos1)
        pos2 = jnp.where(m2, p, pos2)
        padded = jnp.floor((total + (MOE_TILE - 1)) * (1.0 / MOE_TILE)) * MOE_TILE
        base = base + padded
        seg = seg + jnp.where(seg_lane == e, base, 0.0) + jnp.where(seg_lane == N_EXPERTS + e, total, 0.0)
        tile_expert = tile_expert + jnp.where(tile_row >= base, 1.0, 0.0)
    seg = seg + jnp.where(seg_lane == SEG_NTILES, base * (1.0 / MOE_TILE), 0.0)
    pos1_ref[...] = pos1.astype(jnp.int32)
    pos2_ref[...] = pos2.astype(jnp.int32)
    te_ref[...] = jnp.minimum(tile_expert, N_EXPERTS - 1).astype(jnp.int32)
    seg_ref[...] = seg.astype(jnp.int32)


def _positions(e1, e2, n_tiles):
    rows = e1.shape[0]
    ntp = -(-n_tiles // LANES) * LANES
    full = lambda shape: pl.BlockSpec(shape, lambda: (0,) * len(shape))
    return pl.pallas_call(
        _positions_kernel,
        in_specs=[full((rows, LANES)), full((rows, LANES))],
        out_specs=[full((rows, LANES)), full((rows, LANES)), full((1, ntp)), full((1, LANES))],
        out_shape=[
            jax.ShapeDtypeStruct((rows, LANES), jnp.int32),
            jax.ShapeDtypeStruct((rows, LANES), jnp.int32),
            jax.ShapeDtypeStruct((1, ntp), jnp.int32),
            jax.ShapeDtypeStruct((1, LANES), jnp.int32),
        ],
        compiler_params=pltpu.CompilerParams(vmem_limit_bytes=VMEM_LIMIT),
        name="positions",
    )(e1, e2)


def _dispatch_kernel(seg_ref, pos1_ref, pos2_ref, u_ref, xs_ref, z_ref, zsem, sem, *, tb):
    i = pl.program_id(0)

    def zero_copy(e):
        end = seg_ref[e]
        start = pl.multiple_of(end - MOE_TILE, MOE_TILE)
        return pltpu.make_async_copy(z_ref, xs_ref.at[pl.ds(start, MOE_TILE)], zsem)

    def nonempty(e):
        return seg_ref[e] > (seg_ref[e - 1] if e > 0 else 0)

    n_tiles = xs_ref.shape[0] // MOE_TILE

    def tail_copy(k):
        start = pl.multiple_of((seg_ref[SEG_NTILES] + k) * MOE_TILE, MOE_TILE)
        return pltpu.make_async_copy(z_ref, xs_ref.at[pl.ds(start, MOE_TILE)], zsem)

    def tail_exists(k):
        return seg_ref[SEG_NTILES] + k < n_tiles

    @pl.when(i == 0)
    def _():
        z_ref[...] = jnp.zeros(z_ref.shape, z_ref.dtype)
        for e in range(N_EXPERTS):
            @pl.when(nonempty(e))
            def _():
                zero_copy(e).start()

            @pl.when(tail_exists(e))
            def _():
                tail_copy(e).start()
        for e in range(N_EXPERTS):
            @pl.when(nonempty(e))
            def _():
                zero_copy(e).wait()

            @pl.when(tail_exists(e))
            def _():
                tail_copy(e).wait()

    def row_copy(t, pos_ref):
        return pltpu.make_async_copy(u_ref.at[pl.ds(t, 1)], xs_ref.at[pl.ds(pos_ref[t], 1)], sem)

    def issue(t, carry):
        row_copy(t, pos1_ref).start()
        row_copy(t, pos2_ref).start()
        return carry

    lax.fori_loop(0, tb, issue, 0, unroll=8)

    for _ in range(2):
        pltpu.make_async_copy(u_ref, xs_ref.at[pl.ds(0, tb)], sem).wait()


def _dispatch(seg, pos1, pos2, u2p, n_tiles, *, tb):
    t_rows, width = u2p.shape
    return pl.pallas_call(
        functools.partial(_dispatch_kernel, tb=tb),
        grid_spec=pltpu.PrefetchScalarGridSpec(
            num_scalar_prefetch=1,
            grid=(t_rows // tb,),
            in_specs=[
                pl.BlockSpec((tb,), lambda i, seg: (i,), memory_space=pltpu.SMEM),
                pl.BlockSpec((tb,), lambda i, seg: (i,), memory_space=pltpu.SMEM),
                pl.BlockSpec((tb, width), lambda i, seg: (i, 0)),
            ],
            out_specs=pl.BlockSpec(memory_space=pl.ANY),
            scratch_shapes=[
                pltpu.VMEM((MOE_TILE, width), u2p.dtype),
                pltpu.SemaphoreType.DMA(()),
                pltpu.SemaphoreType.DMA(()),
            ],
        ),
        out_shape=jax.ShapeDtypeStruct((n_tiles * MOE_TILE, width), u2p.dtype),
        compiler_params=_params(("arbitrary",)),
        name="dispatch",
    )(seg, pos1, pos2, u2p)


def _moe_kernel(te_ref, seg_ref, x_ref, wg_ref, wu_ref, wd_ref, y_ref):
    i = pl.program_id(0)

    @pl.when(i < seg_ref[SEG_NTILES])
    def _():
        xw = x_ref[...]
        half = D_MODEL // 2
        lo = pltpu.bitcast(xw << 16, F32).astype(BF16)
        hi = pltpu.bitcast(xw & jnp.uint32(0xFFFF0000), F32).astype(BF16)
        hg = (jnp.dot(lo, wg_ref[0, :half, :], preferred_element_type=F32)
              + jnp.dot(hi, wg_ref[0, half:, :], preferred_element_type=F32))
        hu = (jnp.dot(lo, wu_ref[0, :half, :], preferred_element_type=F32)
              + jnp.dot(hi, wu_ref[0, half:, :], preferred_element_type=F32))
        hid = hg * _sigmoid(hg) * hu
        y_ref[...] = jnp.dot(hid.astype(BF16), wd_ref[0], preferred_element_type=F32)

    @pl.when(i >= seg_ref[SEG_NTILES])
    def _():
        y_ref[...] = jnp.zeros(y_ref.shape, y_ref.dtype)


def _moe(tile_expert, seg, xs, wg, wu, wd):
    n_tiles = xs.shape[0] // MOE_TILE

    def row_map(i, te, seg):
        return (jnp.minimum(i, seg[SEG_NTILES] - 1), 0)

    def w_map(i, te, seg):
        return (te[jnp.minimum(i, seg[SEG_NTILES] - 1)], 0, 0)

    return pl.pallas_call(
        _moe_kernel,
        grid_spec=pltpu.PrefetchScalarGridSpec(
            num_scalar_prefetch=2,
            grid=(n_tiles,),
            in_specs=[
                pl.BlockSpec((MOE_TILE, D_MODEL // 2), row_map),
                pl.BlockSpec((1, D_MODEL, EXPERT_FF), w_map),
                pl.BlockSpec((1, D_MODEL, EXPERT_FF), w_map),
                pl.BlockSpec((1, EXPERT_FF, D_MODEL), w_map),
            ],
            out_specs=pl.BlockSpec((MOE_TILE, D_MODEL), lambda i, te, seg: (i, 0)),
        ),
        out_shape=jax.ShapeDtypeStruct((n_tiles * MOE_TILE, D_MODEL), F32),
        compiler_params=_params(("arbitrary",)),
        name="moe",
    )(tile_expert, seg, xs, wg, wu, wd)


def _final_kernel(pos1_ref, pos2_ref, nxt1_ref, nxt2_ref, h2_ref, info_ref, g_ref, ys_ref, o_ref, ybuf, sems, *, tm):
    i = pl.program_id(0)
    n = pl.num_programs(0)
    slot = i % 2

    def gather(p1_ref, p2_ref, s):
        def issue(t, carry):
            pltpu.make_async_copy(ys_ref.at[pl.ds(p1_ref[t], 1)], ybuf.at[s, 0, pl.ds(t, 1)], sems.at[s]).start()
            pltpu.make_async_copy(ys_ref.at[pl.ds(p2_ref[t], 1)], ybuf.at[s, 1, pl.ds(t, 1)], sems.at[s]).start()
            return carry

        lax.fori_loop(0, tm, issue, 0, unroll=8)

    @pl.when(i == 0)
    def _():
        gather(pos1_ref, pos2_ref, 0)

    @pl.when(i + 1 < n)
    def _():
        gather(nxt1_ref, nxt2_ref, 1 - slot)

    for k in range(2):
        pltpu.make_async_copy(ys_ref.at[pl.ds(0, tm)], ybuf.at[slot, k], sems.at[slot]).wait()

    info = info_ref[...]
    h = h2_ref[...] + info[:, 2:3] * ybuf[slot, 0] + info[:, 3:4] * ybuf[slot, 1]
    ms = jnp.mean(h * h, axis=-1, keepdims=True)
    o_ref[...] = h * lax.rsqrt(ms + RMS_EPS) * g_ref[...]


def _final(pos1, pos2, h2, info, final_norm, ys, *, tm):
    t_rows = h2.shape[0]
    n_steps = t_rows // tm
    nxt = lambda i: (jnp.minimum(i + 1, n_steps - 1),)
    return pl.pallas_call(
        functools.partial(_final_kernel, tm=tm),
        grid=(n_steps,),
        in_specs=[
            pl.BlockSpec((tm,), lambda i: (i,), memory_space=pltpu.SMEM),
            pl.BlockSpec((tm,), lambda i: (i,), memory_space=pltpu.SMEM),
            pl.BlockSpec((tm,), nxt, memory_space=pltpu.SMEM),
            pl.BlockSpec((tm,), nxt, memory_space=pltpu.SMEM),
            pl.BlockSpec((tm, D_MODEL), lambda i: (i, 0)),
            pl.BlockSpec((tm, ROUTER_COLS), lambda i: (i, 0)),
            pl.BlockSpec((1, D_MODEL), lambda i: (0, 0)),
            pl.BlockSpec(memory_space=pl.ANY),
        ],
        out_specs=pl.BlockSpec((tm, D_MODEL), lambda i: (i, 0)),
        out_shape=jax.ShapeDtypeStruct((t_rows, D_MODEL), F32),
        scratch_shapes=[pltpu.VMEM((2, 2, tm, D_MODEL), F32), pltpu.SemaphoreType.DMA((2,))],
        compiler_params=_params(("arbitrary",)),
        name="final",
    )(pos1, pos2, pos1, pos2, h2, info, final_norm, ys)


def _rope_tables(row, col):
    n_freq = HEAD_DIM // 4
    inv_freq = ROPE_THETA ** (-jnp.arange(n_freq, dtype=F32) / n_freq)
    ra = row.astype(F32)[:, None] * inv_freq
    ca = col.astype(F32)[:, None] * inv_freq
    cos = jnp.concatenate([jnp.cos(ra), jnp.cos(ca), jnp.cos(ra), jnp.cos(ca)], axis=-1)
    sin = jnp.concatenate([-jnp.sin(ra), -jnp.sin(ca), jnp.sin(ra), jnp.sin(ca)], axis=-1)
    return cos, sin


def _gain_rows(g):
    gs = _swap_rotary_sections(g.astype(F32))
    return jnp.stack([gs, jnp.roll(gs, HEAD_DIM // 2)])


def _split_bf16(w):
    hi = w.astype(BF16)
    lo = (w - hi.astype(F32)).astype(BF16)
    return hi, lo


def _tile(n, pref):
    t = min(n, pref)
    assert n % t == 0, (n, pref)
    return t


def _encode_group(x, shared):
    (norm_mix, w_in, col_scale, q_norm, k_norm, meta_kv, tab, tabm, sink_rows, wa, wb, wo, norm_ffn, wr, br,
     wg, wu, wd, final_norm) = shared
    b, s, _ = x.shape
    t_rows = b * s
    x2d = x.reshape(t_rows, D_MODEL)
    tok = jnp.arange(s)
    cos, sin = _rope_tables(tok // GRID_W, tok % GRID_W)
    proj2d = _in_proj(x2d, norm_mix, w_in, col_scale, cos, sin, q_norm, k_norm, tm=_tile(s, 1024))
    proj = proj2d.reshape(b, s, IN_COLS)
    ka_m, va_m, kb_m, vb_m = meta_kv
    oa = _global_attn(proj, ka_m, va_m, tq=_tile(s, 256), ck=_tile(s, 1024))
    nblk = s // Q_BLOCK
    ob = _window_attn(proj, kb_m, vb_m, tab, tabm, sink_rows, nq=math.gcd(nblk, 16))
    h2, u2p, logits = _merge_route(oa.reshape(t_rows, A_WIDTH), ob.reshape(t_rows, B_WIDTH), proj2d, x2d,
                                   wa, wb, wo, norm_ffn, wr, br, tm=_tile(t_rows, 512))
    info, e1, e2 = _route_call(logits, tm=_tile(t_rows, 2048))
    n_tiles = 2 * t_rows // MOE_TILE + N_EXPERTS
    pos1, pos2, tile_expert, seg = _positions(e1, e2, n_tiles)
    pos1 = pos1.reshape(t_rows)
    pos2 = pos2.reshape(t_rows)
    seg = seg.reshape(LANES)
    xs = _dispatch(seg, pos1, pos2, u2p, n_tiles, tb=_tile(t_rows, 4096))
    ys = _moe(tile_expert.reshape(-1)[:n_tiles], seg, xs, wg, wu, wd)
    out = _final(pos1, pos2, h2, info, final_norm, ys, tm=_tile(t_rows, 256))
    return out.reshape(b, s, D_MODEL)


def kernel(x_prompt, x_sample, meta_tokens, rel_bias, final_norm, norm_mix, w_in, q_norm, k_norm, sink,
           w_branch_a, w_branch_b, w_out, norm_ffn, w_router_g, b_router_g, w_router_e, b_router_e,
           w_gate, w_up, w_down):
    assert norm_mix.shape[0] == 1, "single-layer encoder"
    rot_cols = (A_HEADS + A_KV_HEADS) * HEAD_DIM
    w_in_b = w_in[0].astype(BF16)
    w_in_b = lax.dynamic_update_slice(w_in_b, _swap_rotary_sections(w_in_b[:, :rot_cols]), (0, 0))
    norm_mix2 = norm_mix[0].reshape(1, D_MODEL)
    q_norm2 = _gain_rows(q_norm[0])
    k_norm2 = _gain_rows(k_norm[0])
    col = jnp.arange(IN_COLS)
    is_qb = (col >= COL_QB * HEAD_DIM) & (col < COL_KB * HEAD_DIM)
    col_scale = jnp.where(is_qb, SCORE_SCALE * LOG2E, 1.0).astype(F32).reshape(1, IN_COLS)

    cos_m, sin_m = _rope_tables(jnp.full((N_META,), -1), jnp.arange(N_META))
    proj_m = _in_proj(meta_tokens, norm_mix2, w_in_b, col_scale, cos_m, sin_m, q_norm2, k_norm2, tm=N_META)

    def meta_heads(c0):
        blk = proj_m[:, c0 * HEAD_DIM:(c0 + A_KV_HEADS) * HEAD_DIM]
        return blk.reshape(N_META, A_KV_HEADS, HEAD_DIM).transpose(1, 0, 2)

    meta_kv = tuple(meta_heads(c) for c in (COL_KA, COL_VA, COL_KB, COL_VB))

    tab, tabm = _bias_tables(rel_bias)
    sink_rows = jnp.repeat(sink[0].astype(F32), Q_BLOCK).reshape(B_KV_HEADS, REP * Q_BLOCK, 1)

    wr_full = jnp.zeros((D_MODEL, ROUTER_COLS), F32)
    wr_full = wr_full.at[:, :N_GROUPS].set(w_router_g[0]).at[:, N_GROUPS:N_GROUPS + N_EXPERTS].set(w_router_e[0])
    wr_hi, wr_lo = _split_bf16(wr_full)
    wr = jnp.concatenate([wr_hi, wr_lo], axis=1)
    br = jnp.zeros((1, ROUTER_COLS), F32)
    br = br.at[0, :N_GROUPS].set(b_router_g[0]).at[0, N_GROUPS:N_GROUPS + N_EXPERTS].set(b_router_e[0])

    shared = (norm_mix2, w_in_b, col_scale, q_norm2, k_norm2, meta_kv, tab, tabm, sink_rows,
              w_branch_a[0].astype(BF16), w_branch_b[0].astype(BF16), w_out[0].astype(BF16),
              norm_ffn[0].reshape(1, D_MODEL), wr, br,
              w_gate[0].astype(BF16), w_up[0].astype(BF16), w_down[0].astype(BF16),
              final_norm.reshape(1, D_MODEL))
    return (_encode_group(x_prompt, shared), _encode_group(x_sample, shared))
```

```python
import functools
import math

import numpy as np
import jax
import jax.numpy as jnp
from jax import lax
from jax.experimental import pallas as pl
from jax.experimental.pallas import tpu as pltpu

F32 = jnp.float32
BF16 = jnp.bfloat16

D_MODEL = 2048
HEAD_DIM = 128
A_HEADS = 8
A_KV_HEADS = 2
B_HEADS = 8
B_KV_HEADS = 2
REP = A_HEADS // A_KV_HEADS
A_WIDTH = A_HEADS * HEAD_DIM
B_WIDTH = B_HEADS * HEAD_DIM
KV_WIDTH = A_KV_HEADS * HEAD_DIM
IN_COLS = A_WIDTH + 2 * KV_WIDTH + B_WIDTH + 2 * KV_WIDTH + 2 * D_MODEL
Q_BLOCK = 128
WINDOW = 128
N_META = 16
GRID_W = 64
ROPE_THETA = 10000.0
N_BUCKETS = 32
MAX_DISTANCE = 128
N_GROUPS = 4
EXPERTS_PER_GROUP = 4
N_EXPERTS = N_GROUPS * EXPERTS_PER_GROUP
EXPERT_FF = 1024
RMS_EPS = 1e-6
NEG_INF = -1e30
SCORE_SCALE = HEAD_DIM ** -0.5
LOG2E = math.log2(math.e)

COL_QA = 0
COL_KA = A_WIDTH // HEAD_DIM
COL_VA = COL_KA + A_KV_HEADS
COL_QB = COL_VA + A_KV_HEADS
COL_KB = COL_QB + B_HEADS
COL_VB = COL_KB + B_KV_HEADS
COL_GA = COL_VB + B_KV_HEADS
COL_GB = COL_GA + D_MODEL // HEAD_DIM

LANES = 128
VMEM_LIMIT = 56 * 1024 * 1024
MOE_TILE = 512
ROUTER_COLS = 128

_NT = (((1,), (1,)), ((), ()))


def _params(sem, vmem=VMEM_LIMIT):
    return pltpu.CompilerParams(dimension_semantics=sem, vmem_limit_bytes=vmem)


def _const_spec(shape):
    nd = len(shape)
    return pl.BlockSpec(shape, lambda *_: (0,) * nd, pipeline_mode=pl.Buffered(1))


def _swap_rotary_sections(w):
    lead = w.shape[:-1]
    heads = w.shape[-1] // HEAD_DIM
    nd = len(lead)
    w5 = w.reshape(lead + (heads, 2, 2, HEAD_DIM // 4))
    return jnp.swapaxes(w5, nd + 1, nd + 2).reshape(w.shape)


def _norm_rope(a, cg, sg):
    ones = jnp.ones((HEAD_DIM, HEAD_DIM), BF16)
    ssq = jnp.dot((a * a).astype(BF16), ones, preferred_element_type=F32)
    rinv = lax.rsqrt(ssq * (1.0 / HEAD_DIM) + RMS_EPS)
    return rinv * (a * cg + pltpu.roll(a, HEAD_DIM // 2, 1) * sg)


def _in_proj_kernel(x_ref, g_ref, w_ref, cs_ref, cos_ref, sin_ref, qn_ref, kn_ref, o_ref, u_ref, *, sub):
    j = pl.program_id(1)
    row_tiles = [slice(k * sub, (k + 1) * sub) for k in range(x_ref.shape[0] // sub)]

    def project(r):
        return jnp.dot(u_ref[r, :], w_ref[...], preferred_element_type=F32) * cs_ref[...]

    def rope_factors(gain_ref, scale, r):
        return cos_ref[r, :] * (gain_ref[0:1, :] * scale), sin_ref[r, :] * (gain_ref[1:2, :] * scale)

    @pl.when(j == 0)
    def _():
        rot = (A_HEADS + A_KV_HEADS) * HEAD_DIM
        for r in row_tiles:
            x = x_ref[r, :]
            ms = jnp.mean(x * x, axis=-1, keepdims=True)
            u_ref[r, :] = (x * lax.rsqrt(ms + RMS_EPS) * g_ref[...]).astype(BF16)
            acc = project(r)
            q_factors = rope_factors(qn_ref, SCORE_SCALE * LOG2E, r)
            k_factors = rope_factors(kn_ref, 1.0, r)
            for h in range(A_HEADS + A_KV_HEADS):
                sl = slice(h * HEAD_DIM, (h + 1) * HEAD_DIM)
                cg, sg = q_factors if h < A_HEADS else k_factors
                o_ref[r, sl] = _norm_rope(acc[:, sl], cg, sg).astype(BF16)
            o_ref[r, rot:] = acc[:, rot:].astype(BF16)

    @pl.when(j >= 1)
    def _():
        for r in row_tiles:
            o_ref[r, :] = project(r).astype(BF16)


IN_PROJ_TN = IN_COLS // 4


def _in_proj(x2d, norm_g, w_bf16, col_scale, cos, sin, q_norm, k_norm, *, tm):
    t_rows = x2d.shape[0]
    tn = IN_PROJ_TN
    assert tn % HEAD_DIM == 0 and tn >= (A_HEADS + A_KV_HEADS) * HEAD_DIM
    pos_tiles = cos.shape[0] // tm
    return pl.pallas_call(
        functools.partial(_in_proj_kernel, sub=min(tm, 512)),
        grid=(t_rows // tm, IN_COLS // tn),
        in_specs=[
            pl.BlockSpec((tm, D_MODEL), lambda i, j: (i, 0)),
            pl.BlockSpec((1, D_MODEL), lambda i, j: (0, 0)),
            pl.BlockSpec((D_MODEL, tn), lambda i, j: (0, j)),
            pl.BlockSpec((1, tn), lambda i, j: (0, j)),
            pl.BlockSpec((tm, HEAD_DIM), lambda i, j: (i % pos_tiles, 0)),
            pl.BlockSpec((tm, HEAD_DIM), lambda i, j: (i % pos_tiles, 0)),
            pl.BlockSpec((2, HEAD_DIM), lambda i, j: (0, 0)),
            pl.BlockSpec((2, HEAD_DIM), lambda i, j: (0, 0)),
        ],
        out_specs=pl.BlockSpec((tm, tn), lambda i, j: (i, j)),
        out_shape=jax.ShapeDtypeStruct((t_rows, IN_COLS), BF16),
        scratch_shapes=[pltpu.VMEM((tm, D_MODEL), BF16)],
        compiler_params=_params(("parallel", "arbitrary")),
        name="in_proj",
    )(x2d, norm_g, w_bf16, col_scale, cos, sin, q_norm, k_norm)


def _stack_heads(q):
    return jnp.concatenate([q[:, r * HEAD_DIM:(r + 1) * HEAD_DIM] for r in range(REP)], axis=0)


def _global_attn_kernel(q_ref, k_ref, v_ref, km_ref, vm_ref, o_ref, vx_ref, vmx_ref, *, tq, ck):
    @pl.when(pl.program_id(2) == 0)
    def _():
        vx_ref[:, :HEAD_DIM] = v_ref[0]
        vx_ref[:, HEAD_DIM:] = jnp.ones((v_ref.shape[1], HEAD_DIM), BF16)
        vmx_ref[:, :HEAD_DIM] = vm_ref[0]
        vmx_ref[:, HEAD_DIM:] = jnp.ones((N_META, HEAD_DIM), BF16)

    qs = _stack_heads(q_ref[0])
    n_chunks = k_ref.shape[1] // ck

    def scores(c):
        return lax.dot_general(qs, k_ref[0, c * ck:(c + 1) * ck, :], _NT, preferred_element_type=F32)

    s = lax.dot_general(qs, km_ref[0], _NT, preferred_element_type=F32)
    s_next = scores(0)
    m = jnp.max(s, axis=-1, keepdims=True)
    acc = jnp.dot(jnp.exp2(s - m).astype(BF16), vmx_ref[...], preferred_element_type=F32)
    for c in range(n_chunks):
        s = s_next
        if c + 1 < n_chunks:
            s_next = scores(c + 1)
        m_new = jnp.maximum(m, jnp.max(s, axis=-1, keepdims=True))
        p = jnp.exp2(s - m_new).astype(BF16)
        acc = jnp.exp2(m - m_new) * acc + jnp.dot(p, vx_ref[c * ck:(c + 1) * ck, :],
                                                  preferred_element_type=F32)
        m = m_new
    o = acc[:, :HEAD_DIM] / acc[:, HEAD_DIM:]
    for r in range(REP):
        o_ref[0, :, r * HEAD_DIM:(r + 1) * HEAD_DIM] = o[r * tq:(r + 1) * tq].astype(BF16)


def _global_attn(proj, km, vm, *, tq, ck):
    b, s, _ = proj.shape
    gw = REP * HEAD_DIM
    return pl.pallas_call(
        functools.partial(_global_attn_kernel, tq=tq, ck=ck),
        grid=(b, A_KV_HEADS, s // tq),
        in_specs=[
            pl.BlockSpec((1, tq, gw), lambda bi, g, i: (bi, i, g)),
            pl.BlockSpec((1, s, HEAD_DIM), lambda bi, g, i: (bi, 0, COL_KA + g)),
            pl.BlockSpec((1, s, HEAD_DIM), lambda bi, g, i: (bi, 0, COL_VA + g)),
            pl.BlockSpec((1, N_META, HEAD_DIM), lambda bi, g, i: (g, 0, 0)),
            pl.BlockSpec((1, N_META, HEAD_DIM), lambda bi, g, i: (g, 0, 0)),
        ],
        out_specs=pl.BlockSpec((1, tq, gw), lambda bi, g, i: (bi, i, g)),
        out_shape=jax.ShapeDtypeStruct((b, s, A_WIDTH), BF16),
        scratch_shapes=[pltpu.VMEM((s, 2 * HEAD_DIM), BF16), pltpu.VMEM((N_META, 2 * HEAD_DIM), BF16)],
        compiler_params=_params(("parallel", "parallel", "arbitrary")),
        name="global_attn",
    )(proj, proj, proj, km, vm)


def _t5_bucket_np(rel):
    nb = N_BUCKETS // 2
    max_exact = nb // 2
    bucket = np.where(rel > 0, nb, 0)
    n = np.abs(rel)
    nf = np.maximum(n, 1).astype(np.float32)
    large = max_exact + (np.log(nf / np.float32(max_exact)) / np.float32(math.log(MAX_DISTANCE / max_exact))
                         * np.float32(nb - max_exact)).astype(np.int32)
    large = np.minimum(large, nb - 1)
    return (bucket + np.where(n < max_exact, n, large)).astype(np.int32)


def _bucket_maps():
    i = np.arange(Q_BLOCK)[:, None]
    j = np.arange(3 * Q_BLOCK)[None, :]
    real = []
    for off in range(3):
        rel = j - off * Q_BLOCK - i
        real.append(np.where(np.abs(rel) <= WINDOW, _t5_bucket_np(rel), -1))
    m = np.arange(N_META)[None, :]
    first = _t5_bucket_np(m - (N_META + i))
    later = _t5_bucket_np(m - (N_META + i + Q_BLOCK))
    meta = [first, later, later]
    return np.stack(real).astype(np.int32), np.stack(meta).astype(np.int32)


def _bias_table_kernel(rb_ref, bm_ref, bmm_ref, o_ref, om_ref):
    h = pl.program_id(1)
    bm = bm_ref[0]
    bmm = bmm_ref[0]
    acc = jnp.full(bm.shape, NEG_INF, F32)
    accm = jnp.full(bmm.shape, NEG_INF, F32)
    for k in range(N_BUCKETS):
        val = rb_ref[k, h] * LOG2E
        acc = jnp.where(bm == k, val, acc)
        accm = jnp.where(bmm == k, val, accm)
    o_ref[0, 0] = acc
    om_ref[0, 0] = accm


def _bias_tables(rel_bias):
    bm, bmm = _bucket_maps()
    kw = 3 * Q_BLOCK
    tab, tabm = pl.pallas_call(
        _bias_table_kernel,
        grid=(3, B_HEADS),
        in_specs=[
            pl.BlockSpec(memory_space=pltpu.SMEM),
            pl.BlockSpec((1, Q_BLOCK, kw), lambda v, h: (v, 0, 0)),
            pl.BlockSpec((1, Q_BLOCK, N_META), lambda v, h: (v, 0, 0)),
        ],
        out_specs=[
            pl.BlockSpec((1, 1, Q_BLOCK, kw), lambda v, h: (v, h, 0, 0)),
            pl.BlockSpec((1, 1, Q_BLOCK, N_META), lambda v, h: (v, h, 0, 0)),
        ],
        out_shape=[
            jax.ShapeDtypeStruct((3, B_HEADS, Q_BLOCK, kw), F32),
            jax.ShapeDtypeStruct((3, B_HEADS, Q_BLOCK, N_META), F32),
        ],
        compiler_params=_params(("arbitrary", "arbitrary")),
        name="bias_tables",
    )(rel_bias, jnp.asarray(bm), jnp.asarray(bmm))
    rows = REP * Q_BLOCK
    return (tab.reshape(3, B_KV_HEADS, rows, kw), tabm.reshape(3, B_KV_HEADS, rows, N_META))


def _window_attn_kernel(q_ref, k_ref, v_ref, km_ref, vm_ref, tab_ref, tabm_ref, sink_ref, o_ref, *, nq):
    jb = pl.program_id(2)
    nblk = k_ref.shape[1] // Q_BLOCK
    km = km_ref[0]
    vm = vm_ref[0]
    sink = sink_ref[0] * LOG2E
    for t in range(nq):
        n = jb * nq + t
        qs = _stack_heads(q_ref[0, t * Q_BLOCK:(t + 1) * Q_BLOCK, :])
        var = jnp.where(n == 0, 0, jnp.where(n == nblk - 1, 2, 1))
        start = pl.multiple_of(jnp.clip(n - 1, 0, nblk - 3) * Q_BLOCK, Q_BLOCK)
        kb = k_ref[0, pl.ds(start, 3 * Q_BLOCK), :]
        vb = v_ref[0, pl.ds(start, 3 * Q_BLOCK), :]
        s = lax.dot_general(qs, kb, _NT, preferred_element_type=F32) + tab_ref[var, 0]
        sm = lax.dot_general(qs, km, _NT, preferred_element_type=F32) + tabm_ref[var, 0]
        m = jnp.maximum(jnp.maximum(jnp.max(s, axis=-1, keepdims=True),
                                    jnp.max(sm, axis=-1, keepdims=True)), sink)
        p = jnp.exp2(s - m)
        pm = jnp.exp2(sm - m)
        l = jnp.sum(p, axis=-1, keepdims=True) + jnp.sum(pm, axis=-1, keepdims=True) + jnp.exp2(sink - m)
        o = (jnp.dot(p.astype(BF16), vb, preferred_element_type=F32)
             + jnp.dot(pm.astype(BF16), vm, preferred_element_type=F32)) / l
        for r in range(REP):
            o_ref[0, t * Q_BLOCK:(t + 1) * Q_BLOCK, r * HEAD_DIM:(r + 1) * HEAD_DIM] = (
                o[r * Q_BLOCK:(r + 1) * Q_BLOCK].astype(BF16))


def _window_attn(proj, km, vm, tab, tabm, sink_rows, *, nq):
    b, s, _ = proj.shape
    assert s // Q_BLOCK >= 3 and (s // Q_BLOCK) % nq == 0
    gw = REP * HEAD_DIM
    rows = REP * Q_BLOCK
    kw = 3 * Q_BLOCK
    tq = nq * Q_BLOCK
    return pl.pallas_call(
        functools.partial(_window_attn_kernel, nq=nq),
        grid=(b, B_KV_HEADS, s // tq),
        in_specs=[
            pl.BlockSpec((1, tq, gw), lambda bi, g, i: (bi, i, COL_QB // REP + g)),
            pl.BlockSpec((1, s, HEAD_DIM), lambda bi, g, i: (bi, 0, COL_KB + g)),
            pl.BlockSpec((1, s, HEAD_DIM), lambda bi, g, i: (bi, 0, COL_VB + g)),
            pl.BlockSpec((1, N_META, HEAD_DIM), lambda bi, g, i: (g, 0, 0)),
            pl.BlockSpec((1, N_META, HEAD_DIM), lambda bi, g, i: (g, 0, 0)),
            pl.BlockSpec((3, 1, rows, kw), lambda bi, g, i: (0, g, 0, 0)),
            pl.BlockSpec((3, 1, rows, N_META), lambda bi, g, i: (0, g, 0, 0)),
            pl.BlockSpec((1, rows, 1), lambda bi, g, i: (g, 0, 0)),
        ],
        out_specs=pl.BlockSpec((1, tq, gw), lambda bi, g, i: (bi, i, g)),
        out_shape=jax.ShapeDtypeStruct((b, s, B_WIDTH), BF16),
        compiler_params=_params(("parallel", "parallel", "arbitrary")),
        name="window_attn",
    )(proj, proj, proj, km, vm, tab, tabm, sink_rows)


def _sigmoid(x):
    return 1.0 / (1.0 + jnp.exp(-x))


def _route(logits):
    lane = lax.broadcasted_iota(jnp.int32, logits.shape, 1).astype(F32)
    ninf = jnp.float32(-jnp.inf)
    big = jnp.float32(ROUTER_COLS)
    is_g = lane < N_GROUPS
    lg = jnp.where(is_g, logits, ninf)
    mg = jnp.max(lg, axis=-1, keepdims=True)
    gidx = jnp.min(jnp.where(lg == mg, lane, big), axis=-1, keepdims=True)
    p_top = 1.0 / jnp.sum(jnp.where(is_g, jnp.exp(lg - mg), 0.0), axis=-1, keepdims=True)
    lo = N_GROUPS + EXPERTS_PER_GROUP * gidx
    sel = (lane >= lo) & (lane < lo + EXPERTS_PER_GROUP)
    le = jnp.where(sel, logits, ninf)
    v1 = jnp.max(le, axis=-1, keepdims=True)
    i1 = jnp.min(jnp.where(sel & (le == v1), lane, big), axis=-1, keepdims=True)
    rest = sel & (lane != i1)
    le2 = jnp.where(rest, logits, ninf)
    v2 = jnp.max(le2, axis=-1, keepdims=True)
    i2 = jnp.min(jnp.where(rest & (le2 == v2), lane, big), axis=-1, keepdims=True)
    t = jnp.exp(v2 - v1)
    w1 = p_top / (1.0 + t)
    w2 = p_top * t / (1.0 + t)
    return i1 - N_GROUPS, i2 - N_GROUPS, w1, w2


def _merge_kernel(oa_ref, ob_ref, ga0_ref, ga1_ref, gb0_ref, gb1_ref, x_ref, wa_ref, wb_ref, wo_ref, gn_ref,
                  wr_ref, br_ref, h2_ref, u2_ref, logits_ref, *, sub):
    for st in range(oa_ref.shape[0] // sub):
        r = slice(st * sub, (st + 1) * sub)
        ya = jnp.dot(oa_ref[r, :], wa_ref[...], preferred_element_type=F32)
        yb = jnp.dot(ob_ref[r, :], wb_ref[...], preferred_element_type=F32)
        ga = jnp.concatenate([ga0_ref[r, :], ga1_ref[r, :]], axis=1).astype(F32)
        gb = jnp.concatenate([gb0_ref[r, :], gb1_ref[r, :]], axis=1).astype(F32)
        mixed = _sigmoid(ga) * ya + _sigmoid(gb) * yb
        h2 = x_ref[r, :] + jnp.dot(mixed.astype(BF16), wo_ref[...], preferred_element_type=F32)
        h2_ref[r, :] = h2
        ms = jnp.mean(h2 * h2, axis=-1, keepdims=True)
        u = h2 * lax.rsqrt(ms + RMS_EPS) * gn_ref[...]
        u_hi = u.astype(BF16)
        u_hi32 = u_hi.astype(F32)
        u_lo = (u - u_hi32).astype(BF16)
        lg = (jnp.dot(u_hi, wr_ref[...], preferred_element_type=F32)
              + jnp.dot(u_lo, wr_ref[...], preferred_element_type=F32))
        logits_ref[r, :] = lg[:, :ROUTER_COLS] + lg[:, ROUTER_COLS:] + br_ref[...]
        half = D_MODEL // 2
        lo_bits = pltpu.bitcast(u_hi32[:, :half], jnp.uint32) >> 16
        hi_bits = pltpu.bitcast(u_hi32[:, half:], jnp.uint32)
        u2_ref[r, :] = hi_bits | lo_bits


def _merge_route(oa, ob, proj2d, x2d, wa, wb, wo, norm_ffn, wr, br, *, tm):
    t_rows = x2d.shape[0]
    gw = D_MODEL // 2
    ga_blk = COL_GA * HEAD_DIM // gw
    gb_blk = COL_GB * HEAD_DIM // gw
    return pl.pallas_call(
        functools.partial(_merge_kernel, sub=min(tm, 256)),
        grid=(t_rows // tm,),
        in_specs=[
            pl.BlockSpec((tm, A_WIDTH), lambda i: (i, 0)),
            pl.BlockSpec((tm, B_WIDTH), lambda i: (i, 0)),
            pl.BlockSpec((tm, gw), lambda i: (i, ga_blk)),
            pl.BlockSpec((tm, gw), lambda i: (i, ga_blk + 1)),
            pl.BlockSpec((tm, gw), lambda i: (i, gb_blk)),
            pl.BlockSpec((tm, gw), lambda i: (i, gb_blk + 1)),
            pl.BlockSpec((tm, D_MODEL), lambda i: (i, 0)),
            _const_spec((A_WIDTH, D_MODEL)),
            _const_spec((B_WIDTH, D_MODEL)),
            _const_spec((D_MODEL, D_MODEL)),
            _const_spec((1, D_MODEL)),
            _const_spec((D_MODEL, 2 * ROUTER_COLS)),
            _const_spec((1, ROUTER_COLS)),
        ],
        out_specs=[
            pl.BlockSpec((tm, D_MODEL), lambda i: (i, 0)),
            pl.BlockSpec((tm, D_MODEL // 2), lambda i: (i, 0)),
            pl.BlockSpec((tm, ROUTER_COLS), lambda i: (i, 0)),
        ],
        out_shape=[
            jax.ShapeDtypeStruct((t_rows, D_MODEL), F32),
            jax.ShapeDtypeStruct((t_rows, D_MODEL // 2), jnp.uint32),
            jax.ShapeDtypeStruct((t_rows, ROUTER_COLS), F32),
        ],
        compiler_params=_params(("parallel",)),
        name="merge_route",
    )(oa, ob, proj2d, proj2d, proj2d, proj2d, x2d, wa, wb, wo, norm_ffn, wr, br)


def _route_kernel(logits_ref, info_ref, e1_ref, e2_ref):
    logits = logits_ref[...]
    e1, e2, w1, w2 = _route(logits)
    lane = lax.broadcasted_iota(jnp.int32, logits.shape, 1)
    info_ref[...] = jnp.where(lane == 0, e1, jnp.where(lane == 1, e2, jnp.where(lane == 2, w1,
                              jnp.where(lane == 3, w2, 0.0))))
    row = lax.broadcasted_iota(jnp.int32, logits.shape, 0)
    on_diag = lane == (row & (LANES - 1))
    groups = logits.shape[0] // LANES
    for e, ref in ((e1, e1_ref), (e2, e2_ref)):
        ref[...] = jnp.sum(jnp.where(on_diag, e, 0.0).reshape(groups, LANES, LANES), axis=1)


def _route_call(logits, *, tm):
    t_rows = logits.shape[0]
    dense = jax.ShapeDtypeStruct((t_rows // LANES, LANES), F32)
    return pl.pallas_call(
        _route_kernel,
        grid=(t_rows // tm,),
        in_specs=[pl.BlockSpec((tm, ROUTER_COLS), lambda i: (i, 0))],
        out_specs=[pl.BlockSpec((tm, ROUTER_COLS), lambda i: (i, 0)),
                   pl.BlockSpec((tm // LANES, LANES), lambda i: (i, 0)),
                   pl.BlockSpec((tm // LANES, LANES), lambda i: (i, 0))],
        out_shape=[jax.ShapeDtypeStruct((t_rows, ROUTER_COLS), F32), dense, dense],
        compiler_params=_params(("parallel",)),
        name="route",
    )(logits)


SEG_NTILES = 2 * N_EXPERTS


def _positions_kernel(e1_ref, e2_ref, pos1_ref, pos2_ref, te_ref, seg_ref):
    e1 = e1_ref[...]
    e2 = e2_ref[...]
    rows = e1.shape[0]
    r_i = lax.broadcasted_iota(jnp.int32, (LANES, LANES), 0)
    c_i = lax.broadcasted_iota(jnp.int32, (LANES, LANES), 1)
    upper = (r_i < c_i).astype(BF16)
    rr = lax.broadcasted_iota(jnp.int32, (rows, rows), 0)
    rc = lax.broadcasted_iota(jnp.int32, (rows, rows), 1)
    lower = (rc < rr).astype(BF16)
    seg_lane = lax.broadcasted_iota(jnp.int32, seg_ref.shape, 1)
    tile_row = lax.broadcasted_iota(jnp.int32, te_ref.shape, 1).astype(F32) * MOE_TILE
    base = jnp.zeros((1, 1), F32)
    pos1 = jnp.zeros(e1.shape, F32)
    pos2 = jnp.zeros(e1.shape, F32)
    seg = jnp.zeros(seg_ref.shape, F32)
    tile_expert = jnp.zeros(te_ref.shape, F32)
    for e in range(N_EXPERTS):
        m1 = e1 == e
        m2 = e2 == e
        m = jnp.where(m1 | m2, 1.0, 0.0)
        lane_pre = jnp.dot(m.astype(BF16), upper, preferred_element_type=F32)
        row_tot = jnp.broadcast_to(jnp.sum(m, axis=-1, keepdims=True), m.shape)
        row_pre = jnp.dot(lower, row_tot.astype(BF16), preferred_element_type=F32)
        total = jnp.sum(row_tot[:, 0:1], axis=0, keepdims=True)
        p = base + row_pre + lane_pre
        pos1 = jnp.where(m1, p, pos1)
        pos2 = jnp.where(m2, p, pos2)
        padded = jnp.floor((total + (MOE_TILE - 1)) * (1.0 / MOE_TILE)) * MOE_TILE
        base = base + padded
        seg = seg + jnp.where(seg_lane == e, base, 0.0) + jnp.where(seg_lane == N_EXPERTS + e, total, 0.0)
        tile_expert = tile_expert + jnp.where(tile_row >= base, 1.0, 0.0)
    seg = seg + jnp.where(seg_lane == SEG_NTILES, base * (1.0 / MOE_TILE), 0.0)
    pos1_ref[...] = pos1.astype(jnp.int32)
    pos2_ref[...] = pos2.astype(jnp.int32)
    te_ref[...] = jnp.minimum(tile_expert, N_EXPERTS - 1).astype(jnp.int32)
    seg_ref[...] = seg.astype(jnp.int32)


def _positions(e1, e2, n_tiles):
    rows = e1.shape[0]
    ntp = -(-n_tiles // LANES) * LANES
    full = lambda shape: pl.BlockSpec(shape, lambda: (0,) * len(shape))
    return pl.pallas_call(
        _positions_kernel,
        in_specs=[full((rows, LANES)), full((rows, LANES))],
        out_specs=[full((rows, LANES)), full((rows, LANES)), full((1, ntp)), full((1, LANES))],
        out_shape=[
            jax.ShapeDtypeStruct((rows, LANES), jnp.int32),
            jax.ShapeDtypeStruct((rows, LANES), jnp.int32),
            jax.ShapeDtypeStruct((1, ntp), jnp.int32),
            jax.ShapeDtypeStruct((1, LANES), jnp.int32),
        ],
        compiler_params=pltpu.CompilerParams(vmem_limit_bytes=VMEM_LIMIT),
        name="positions",
    )(e1, e2)


def _dispatch_kernel(seg_ref, pos1_ref, pos2_ref, u_ref, xs_ref, z_ref, zsem, sem, *, tb):
    i = pl.program_id(0)

    def zero_copy(e):
        end = seg_ref[e]
        start = pl.multiple_of(end - MOE_TILE, MOE_TILE)
        return pltpu.make_async_copy(z_ref, xs_ref.at[pl.ds(start, MOE_TILE)], zsem)

    def nonempty(e):
        return seg_ref[e] > (seg_ref[e - 1] if e > 0 else 0)

    n_tiles = xs_ref.shape[0] // MOE_TILE

    def tail_copy(k):
        start = pl.multiple_of((seg_ref[SEG_NTILES] + k) * MOE_TILE, MOE_TILE)
        return pltpu.make_async_copy(z_ref, xs_ref.at[pl.ds(start, MOE_TILE)], zsem)

    def tail_exists(k):
        return seg_ref[SEG_NTILES] + k < n_tiles

    @pl.when(i == 0)
    def _():
        z_ref[...] = jnp.zeros(z_ref.shape, z_ref.dtype)
        for e in range(N_EXPERTS):
            @pl.when(nonempty(e))
            def _():
                zero_copy(e).start()

            @pl.when(tail_exists(e))
            def _():
                tail_copy(e).start()
        for e in range(N_EXPERTS):
            @pl.when(nonempty(e))
            def _():
                zero_copy(e).wait()

            @pl.when(tail_exists(e))
            def _():
                tail_copy(e).wait()

    def row_copy(t, pos_ref):
        return pltpu.make_async_copy(u_ref.at[pl.ds(t, 1)], xs_ref.at[pl.ds(pos_ref[t], 1)], sem)

    def issue(t, carry):
        row_copy(t, pos1_ref).start(priority=0)
        row_copy(t, pos2_ref).start(priority=1)
        return carry

    lax.fori_loop(0, tb, issue, 0, unroll=8)

    for _ in range(2):
        pltpu.make_async_copy(u_ref, xs_ref.at[pl.ds(0, tb)], sem).wait()


def _dispatch(seg, pos1, pos2, u2p, n_tiles, *, tb):
    t_rows, width = u2p.shape
    return pl.pallas_call(
        functools.partial(_dispatch_kernel, tb=tb),
        grid_spec=pltpu.PrefetchScalarGridSpec(
            num_scalar_prefetch=1,
            grid=(t_rows // tb,),
            in_specs=[
                pl.BlockSpec((tb,), lambda i, seg: (i,), memory_space=pltpu.SMEM),
                pl.BlockSpec((tb,), lambda i, seg: (i,), memory_space=pltpu.SMEM),
                pl.BlockSpec((tb, width), lambda i, seg: (i, 0)),
            ],
            out_specs=pl.BlockSpec(memory_space=pl.ANY),
            scratch_shapes=[
                pltpu.VMEM((MOE_TILE, width), u2p.dtype),
                pltpu.SemaphoreType.DMA(()),
                pltpu.SemaphoreType.DMA(()),
            ],
        ),
        out_shape=jax.ShapeDtypeStruct((n_tiles * MOE_TILE, width), u2p.dtype),
        compiler_params=_params(("arbitrary",)),
        name="dispatch",
    )(seg, pos1, pos2, u2p)


def _moe_kernel(te_ref, seg_ref, x_ref, wg_ref, wu_ref, wd_ref, y_ref):
    i = pl.program_id(0)

    @pl.when(i < seg_ref[SEG_NTILES])
    def _():
        xw = x_ref[...]
        half = D_MODEL // 2
        lo = pltpu.bitcast(xw << 16, F32).astype(BF16)
        hi = pltpu.bitcast(xw & jnp.uint32(0xFFFF0000), F32).astype(BF16)
        hg = (jnp.dot(lo, wg_ref[0, :half, :], preferred_element_type=F32)
              + jnp.dot(hi, wg_ref[0, half:, :], preferred_element_type=F32))
        hu = (jnp.dot(lo, wu_ref[0, :half, :], preferred_element_type=F32)
              + jnp.dot(hi, wu_ref[0, half:, :], preferred_element_type=F32))
        hid = hg * _sigmoid(hg) * hu
        y_ref[...] = jnp.dot(hid.astype(BF16), wd_ref[0], preferred_element_type=F32)

    @pl.when(i >= seg_ref[SEG_NTILES])
    def _():
        y_ref[...] = jnp.zeros(y_ref.shape, y_ref.dtype)


def _moe(tile_expert, seg, xs, wg, wu, wd):
    n_tiles = xs.shape[0] // MOE_TILE

    def row_map(i, te, seg):
        return (jnp.minimum(i, seg[SEG_NTILES] - 1), 0)

    def w_map(i, te, seg):
        return (te[jnp.minimum(i, seg[SEG_NTILES] - 1)], 0, 0)

    return pl.pallas_call(
        _moe_kernel,
        grid_spec=pltpu.PrefetchScalarGridSpec(
            num_scalar_prefetch=2,
            grid=(n_tiles,),
            in_specs=[
                pl.BlockSpec((MOE_TILE, D_MODEL // 2), row_map),
                pl.BlockSpec((1, D_MODEL, EXPERT_FF), w_map),
                pl.BlockSpec((1, D_MODEL, EXPERT_FF), w_map),
                pl.BlockSpec((1, EXPERT_FF, D_MODEL), w_map),
            ],
            out_specs=pl.BlockSpec((MOE_TILE, D_MODEL), lambda i, te, seg: (i, 0)),
        ),
        out_shape=jax.ShapeDtypeStruct((n_tiles * MOE_TILE, D_MODEL), F32),
        compiler_params=_params(("arbitrary",)),
        name="moe",
    )(tile_expert, seg, xs, wg, wu, wd)


def _final_kernel(pos1_ref, pos2_ref, nxt1_ref, nxt2_ref, h2_ref, info_ref, g_ref, ys_ref, o_ref, ybuf, sems, *, tm):
    i = pl.program_id(0)
    n = pl.num_programs(0)
    slot = i % 2

    def gather(p1_ref, p2_ref, s):
        def issue(t, carry):
            pltpu.make_async_copy(ys_ref.at[pl.ds(p1_ref[t], 1)], ybuf.at[s, 0, pl.ds(t, 1)],
                                  sems.at[s]).start(priority=0)
            pltpu.make_async_copy(ys_ref.at[pl.ds(p2_ref[t], 1)], ybuf.at[s, 1, pl.ds(t, 1)],
                                  sems.at[s]).start(priority=1)
            return carry

        lax.fori_loop(0, tm, issue, 0, unroll=8)

    @pl.when(i == 0)
    def _():
        gather(pos1_ref, pos2_ref, 0)

    @pl.when(i + 1 < n)
    def _():
        gather(nxt1_ref, nxt2_ref, 1 - slot)

    for k in range(2):
        pltpu.make_async_copy(ys_ref.at[pl.ds(0, tm)], ybuf.at[slot, k], sems.at[slot]).wait()

    info = info_ref[...]
    h = h2_ref[...] + info[:, 2:3] * ybuf[slot, 0] + info[:, 3:4] * ybuf[slot, 1]
    ms = jnp.mean(h * h, axis=-1, keepdims=True)
    o_ref[...] = h * lax.rsqrt(ms + RMS_EPS) * g_ref[...]


def _final(pos1, pos2, h2, info, final_norm, ys, *, tm):
    t_rows = h2.shape[0]
    n_steps = t_rows // tm
    nxt = lambda i: (jnp.minimum(i + 1, n_steps - 1),)
    return pl.pallas_call(
        functools.partial(_final_kernel, tm=tm),
        grid=(n_steps,),
        in_specs=[
            pl.BlockSpec((tm,), lambda i: (i,), memory_space=pltpu.SMEM),
            pl.BlockSpec((tm,), lambda i: (i,), memory_space=pltpu.SMEM),
            pl.BlockSpec((tm,), nxt, memory_space=pltpu.SMEM),
            pl.BlockSpec((tm,), nxt, memory_space=pltpu.SMEM),
            pl.BlockSpec((tm, D_MODEL), lambda i: (i, 0)),
            pl.BlockSpec((tm, ROUTER_COLS), lambda i: (i, 0)),
            pl.BlockSpec((1, D_MODEL), lambda i: (0, 0)),
            pl.BlockSpec(memory_space=pl.ANY),
        ],
        out_specs=pl.BlockSpec((tm, D_MODEL), lambda i: (i, 0)),
        out_shape=jax.ShapeDtypeStruct((t_rows, D_MODEL), F32),
        scratch_shapes=[pltpu.VMEM((2, 2, tm, D_MODEL), F32), pltpu.SemaphoreType.DMA((2,))],
        compiler_params=_params(("arbitrary",)),
        name="final",
    )(pos1, pos2, pos1, pos2, h2, info, final_norm, ys)


def _rope_tables(row, col):
    n_freq = HEAD_DIM // 4
    inv_freq = ROPE_THETA ** (-jnp.arange(n_freq, dtype=F32) / n_freq)
    ra = row.astype(F32)[:, None] * inv_freq
    ca = col.astype(F32)[:, None] * inv_freq
    cos = jnp.concatenate([jnp.cos(ra), jnp.cos(ca), jnp.cos(ra), jnp.cos(ca)], axis=-1)
    sin = jnp.concatenate([-jnp.sin(ra), -jnp.sin(ca), jnp.sin(ra), jnp.sin(ca)], axis=-1)
    return cos, sin


def _gain_rows(g):
    gs = _swap_rotary_sections(g.astype(F32))
    return jnp.stack([gs, jnp.roll(gs, HEAD_DIM // 2)])


def _split_bf16(w):
    hi = w.astype(BF16)
    lo = (w - hi.astype(F32)).astype(BF16)
    return hi, lo


def _tile(n, pref):
    t = min(n, pref)
    assert n % t == 0, (n, pref)
    return t


def _encode_group(x, shared):
    (norm_mix, w_in, col_scale, q_norm, k_norm, meta_kv, tab, tabm, sink_rows, wa, wb, wo, norm_ffn, wr, br,
     wg, wu, wd, final_norm) = shared
    b, s, _ = x.shape
    t_rows = b * s
    x2d = x.reshape(t_rows, D_MODEL)
    tok = jnp.arange(s)
    cos, sin = _rope_tables(tok // GRID_W, tok % GRID_W)
    proj2d = _in_proj(x2d, norm_mix, w_in, col_scale, cos, sin, q_norm, k_norm, tm=_tile(s, 1024))
    proj = proj2d.reshape(b, s, IN_COLS)
    ka_m, va_m, kb_m, vb_m = meta_kv
    oa = _global_attn(proj, ka_m, va_m, tq=_tile(s, 256), ck=_tile(s, 1024))
    nblk = s // Q_BLOCK
    ob = _window_attn(proj, kb_m, vb_m, tab, tabm, sink_rows, nq=math.gcd(nblk, 16))
    h2, u2p, logits = _merge_route(oa.reshape(t_rows, A_WIDTH), ob.reshape(t_rows, B_WIDTH), proj2d, x2d,
                                   wa, wb, wo, norm_ffn, wr, br, tm=_tile(t_rows, 512))
    info, e1, e2 = _route_call(logits, tm=_tile(t_rows, 2048))
    n_tiles = 2 * t_rows // MOE_TILE + N_EXPERTS
    pos1, pos2, tile_expert, seg = _positions(e1, e2, n_tiles)
    pos1 = pos1.reshape(t_rows)
    pos2 = pos2.reshape(t_rows)
    seg = seg.reshape(LANES)
    xs = _dispatch(seg, pos1, pos2, u2p, n_tiles, tb=_tile(t_rows, 4096))
    ys = _moe(tile_expert.reshape(-1)[:n_tiles], seg, xs, wg, wu, wd)
    out = _final(pos1, pos2, h2, info, final_norm, ys, tm=_tile(t_rows, 256))
    return out.reshape(b, s, D_MODEL)


def kernel(x_prompt, x_sample, meta_tokens, rel_bias, final_norm, norm_mix, w_in, q_norm, k_norm, sink,
           w_branch_a, w_branch_b, w_out, norm_ffn, w_router_g, b_router_g, w_router_e, b_router_e,
           w_gate, w_up, w_down):
    assert norm_mix.shape[0] == 1, "single-layer encoder"
    rot_cols = (A_HEADS + A_KV_HEADS) * HEAD_DIM
    w_in_b = w_in[0].astype(BF16)
    w_in_b = lax.dynamic_update_slice(w_in_b, _swap_rotary_sections(w_in_b[:, :rot_cols]), (0, 0))
    norm_mix2 = norm_mix[0].reshape(1, D_MODEL)
    q_norm2 = _gain_rows(q_norm[0])
    k_norm2 = _gain_rows(k_norm[0])
    col = jnp.arange(IN_COLS)
    is_qb = (col >= COL_QB * HEAD_DIM) & (col < COL_KB * HEAD_DIM)
    col_scale = jnp.where(is_qb, SCORE_SCALE * LOG2E, 1.0).astype(F32).reshape(1, IN_COLS)

    cos_m, sin_m = _rope_tables(jnp.full((N_META,), -1), jnp.arange(N_META))
    proj_m = _in_proj(meta_tokens, norm_mix2, w_in_b, col_scale, cos_m, sin_m, q_norm2, k_norm2, tm=N_META)

    def meta_heads(c0):
        blk = proj_m[:, c0 * HEAD_DIM:(c0 + A_KV_HEADS) * HEAD_DIM]
        return blk.reshape(N_META, A_KV_HEADS, HEAD_DIM).transpose(1, 0, 2)

    meta_kv = tuple(meta_heads(c) for c in (COL_KA, COL_VA, COL_KB, COL_VB))

    tab, tabm = _bias_tables(rel_bias)
    sink_rows = jnp.repeat(sink[0].astype(F32), Q_BLOCK).reshape(B_KV_HEADS, REP * Q_BLOCK, 1)

    wr_full = jnp.zeros((D_MODEL, ROUTER_COLS), F32)
    wr_full = wr_full.at[:, :N_GROUPS].set(w_router_g[0]).at[:, N_GROUPS:N_GROUPS + N_EXPERTS].set(w_router_e[0])
    wr_hi, wr_lo = _split_bf16(wr_full)
    wr = jnp.concatenate([wr_hi, wr_lo], axis=1)
    br = jnp.zeros((1, ROUTER_COLS), F32)
    br = br.at[0, :N_GROUPS].set(b_router_g[0]).at[0, N_GROUPS:N_GROUPS + N_EXPERTS].set(b_router_e[0])

    shared = (norm_mix2, w_in_b, col_scale, q_norm2, k_norm2, meta_kv, tab, tabm, sink_rows,
              w_branch_a[0].astype(BF16), w_branch_b[0].astype(BF16), w_out[0].astype(BF16),
              norm_ffn[0].reshape(1, D_MODEL), wr, br,
              w_gate[0].astype(BF16), w_up[0].astype(BF16), w_down[0].astype(BF16),
              final_norm.reshape(1, D_MODEL))
    return (_encode_group(x_prompt, shared), _encode_group(x_sample, shared))
```
